```python
import jax
import jax.numpy as jnp
from jax import lax
import numpy as np

D_MODEL = 1024
BATCH = 1
SEQ = 16384
DEPTH = 2

GRID_W = 64
CTX_LEN = 256
HEAD_DIM = 64
N_Q_HEADS = 12
N_KV_HEADS = 4
Q_PER_KV = N_Q_HEADS // N_KV_HEADS
ATTN_WIDTH = N_Q_HEADS * HEAD_DIM
KV_WIDTH = N_KV_HEADS * HEAD_DIM
N_FOURIER_GROUPS = 4
FOURIER_GROUP_DIM = 64
FOURIER_WIDTH = N_FOURIER_GROUPS * FOURIER_GROUP_DIM
MIX_WIDTH = ATTN_WIDTH + FOURIER_WIDTH
EVEN_IN_WIDTH = ATTN_WIDTH + 2 * KV_WIDTH + FOURIER_WIDTH
EVEN_SPLITS = (ATTN_WIDTH, ATTN_WIDTH + KV_WIDTH, ATTN_WIDTH + 2 * KV_WIDTH)
CONV_WIDTH = D_MODEL
CONV_K = 3
D_FF = 2816
N_EXPERTS = 8
TOP_K = 2
D_EXPERT = 3584
Q_BLOCK = 128
ROPE_THETA = 10000.0
ROPE_AXIS_DIM = HEAD_DIM // 2
NORM_EPS = 1e-6
N_EVEN = (DEPTH + 1) // 2
N_ODD = DEPTH // 2

kernel_name = 'hybrid_fourier_gqa_shortconv_moe_dit'


def rmsnorm(x, g):
    xf = x.astype(jnp.float32)
    y = xf * lax.rsqrt(jnp.mean(xf * xf, axis=-1, keepdims=True) + NORM_EPS)
    return (y * g.astype(jnp.float32)).astype(x.dtype)


def modulate(x, g, shift, scale):
    return rmsnorm(x, g) * (1 + scale) + shift


def axial_rope_angles(n_tokens):
    rows = n_tokens // GRID_W
    row = jnp.repeat(jnp.arange(rows, dtype=jnp.float32), GRID_W)
    col = jnp.tile(jnp.arange(GRID_W, dtype=jnp.float32), rows)
    half = ROPE_AXIS_DIM // 2
    inv_freq = ROPE_THETA ** (-jnp.arange(half, dtype=jnp.float32) / half)
    return row[:, None] * inv_freq, col[:, None] * inv_freq


def rotate_axis(x, ang):
    cos = jnp.cos(ang)[None, :, None, :]
    sin = jnp.sin(ang)[None, :, None, :]
    x1, x2 = jnp.split(x, 2, axis=-1)
    return jnp.concatenate([x1 * cos - x2 * sin, x1 * sin + x2 * cos], axis=-1)


def rope_2d(x, ang_row, ang_col):
    xf = x.astype(jnp.float32)
    out = jnp.concatenate([rotate_axis(xf[..., :ROPE_AXIS_DIM], ang_row),
                           rotate_axis(xf[..., ROPE_AXIS_DIM:], ang_col)], axis=-1)
    return out.astype(x.dtype)


def heads(t, n):
    return t.reshape(t.shape[0], t.shape[1], n, HEAD_DIM)


def gqa_attend(q, k, v):
    b, lq = q.shape[0], q.shape[1]
    n_blk = lq // Q_BLOCK
    kf = k.astype(jnp.float32)
    qb = q.reshape(b, n_blk, Q_BLOCK, N_KV_HEADS, Q_PER_KV, HEAD_DIM).transpose(1, 0, 2, 3, 4, 5)
    scale = HEAD_DIM ** -0.5

    def one_block(q_blk):
        s = jnp.einsum('bqkgd,bskd->bkgqs', q_blk.astype(jnp.float32), kf) * scale
        p = jax.nn.softmax(s, axis=-1).astype(v.dtype)
        return jnp.einsum('bkgqs,bskd->bqkgd', p, v)

    o = lax.map(one_block, qb)
    return o.transpose(1, 0, 2, 3, 4, 5).reshape(b, lq, ATTN_WIDTH)


def fourier_mix(u):
    b, l = u.shape[0], u.shape[1]
    ug = u.reshape(b, l, N_FOURIER_GROUPS, FOURIER_GROUP_DIM).astype(jnp.float32)
    z = jnp.fft.fft2(ug, axes=(1, 3), norm='ortho')
    return jnp.real(z).reshape(b, l, FOURIER_WIDTH).astype(u.dtype)


def parallel_fourier_gqa(h_lat, h_ctx, w_in, q_gain, k_gain, w_out, ang_row, ang_col, ctx_out):
    q, k, v, f = jnp.split(h_lat @ w_in, EVEN_SPLITS, axis=-1)
    q = rope_2d(rmsnorm(heads(q, N_Q_HEADS), q_gain), ang_row, ang_col)
    k = rope_2d(rmsnorm(heads(k, N_KV_HEADS), k_gain), ang_row, ang_col)
    v = heads(v, N_KV_HEADS)
    if ctx_out:
        q_c, k_c, v_c, f_c = jnp.split(h_ctx @ w_in, EVEN_SPLITS, axis=-1)
        q_c = rmsnorm(heads(q_c, N_Q_HEADS), q_gain)
    else:
        k_c, v_c = jnp.split(h_ctx @ w_in[:, ATTN_WIDTH:ATTN_WIDTH + 2 * KV_WIDTH], 2, axis=-1)
    k_c = rmsnorm(heads(k_c, N_KV_HEADS), k_gain)
    v_c = heads(v_c, N_KV_HEADS)
    attn = gqa_attend(q, jnp.concatenate([k, k_c], axis=1), jnp.concatenate([v, v_c], axis=1))
    y_lat = jnp.concatenate([attn, fourier_mix(f)], axis=-1) @ w_out
    y_ctx = None
    if ctx_out:
        attn_c = gqa_attend(q_c, k_c, v_c)
        y_ctx = jnp.concatenate([attn_c, fourier_mix(f_c)], axis=-1) @ w_out
    return y_lat, y_ctx


def short_conv3(u, w):
    up = jnp.pad(u, ((0, 0), (1, 1), (0, 0)))
    return up[:, :-2] * w[0] + up[:, 1:-1] * w[1] + up[:, 2:] * w[2]


def gated_short_conv(h, w_in, conv_w, w_out):
    b_gate, c_gate, u = jnp.split(h @ w_in, 3, axis=-1)
    return (b_gate * short_conv3(c_gate * u, conv_w)) @ w_out


def swiglu(h, w_gate, w_up, w_down):
    return (jax.nn.silu(h @ w_gate) * (h @ w_up)) @ w_down


def moe_swiglu(h, router, w_gate, w_up, w_down):
    logits = (h @ router).astype(jnp.float32)
    probs = jax.nn.softmax(logits, axis=-1)
    vals, idx = lax.top_k(probs, TOP_K)
    wts = vals / jnp.sum(vals, axis=-1, keepdims=True)
    combine = jnp.sum(jax.nn.one_hot(idx, N_EXPERTS, dtype=jnp.float32) * wts[..., None], axis=-2)
    combine = combine.astype(h.dtype)
    out = jnp.zeros_like(h)
    for e in range(N_EXPERTS):
        out = out + combine[..., e:e + 1] * swiglu(h, w_gate[e], w_up[e], w_down[e])
    return out


def _normal(key, shape, fan_in):
    return jax.random.normal(key, shape, jnp.float32) * (fan_in ** -0.5)


def setup_inputs(seed: int = 0) -> dict:
    key = jax.random.key(seed)
    ks = jax.random.split(key, 21)
    D = D_MODEL
    f32 = jnp.float32
    return {
        'x': jax.random.normal(ks[0], (BATCH, SEQ, D), f32),
        'c': jax.random.normal(ks[1], (BATCH, D), f32),
        'ctx': jax.random.normal(ks[2], (BATCH, CTX_LEN, D), f32),
        'c_ctx': jax.random.normal(ks[3], (D,), f32),
        'ada_w': _normal(ks[4], (DEPTH, D, 6 * D), D),
        'ada_b': 0.02 * jax.random.normal(ks[5], (DEPTH, 6 * D), f32),
        'norm_w': 1.0 + 0.05 * jax.random.normal(ks[6], (DEPTH, 4, D), f32),
        'e_w_in': _normal(ks[7], (N_EVEN, D, EVEN_IN_WIDTH), D),
        'e_q_gain': 1.0 + 0.05 * jax.random.normal(ks[8], (N_EVEN, HEAD_DIM), f32),
        'e_k_gain': 1.0 + 0.05 * jax.random.normal(ks[9], (N_EVEN, HEAD_DIM), f32),
        'e_w_out': _normal(ks[10], (N_EVEN, MIX_WIDTH, D), MIX_WIDTH),
        'e_ffn_gate': _normal(ks[11], (N_EVEN, D, D_FF), D),
        'e_ffn_up': _normal(ks[12], (N_EVEN, D, D_FF), D),
        'e_ffn_down': _normal(ks[13], (N_EVEN, D_FF, D), D_FF),
        'o_w_in': _normal(ks[14], (N_ODD, D, 3 * CONV_WIDTH), D),
        'o_conv_w': _normal(ks[15], (N_ODD, CONV_K, CONV_WIDTH), CONV_K),
        'o_w_out': _normal(ks[16], (N_ODD, CONV_WIDTH, D), CONV_WIDTH),
        'o_router': _normal(ks[17], (N_ODD, D, N_EXPERTS), D),
        'o_exp_gate': _normal(ks[18], (N_ODD, N_EXPERTS, D, D_EXPERT), D),
        'o_exp_up': _normal(ks[19], (N_ODD, N_EXPERTS, D, D_EXPERT), D),
        'o_exp_down': _normal(ks[20], (N_ODD, N_EXPERTS, D_EXPERT, D), D_EXPERT),
    }


def reference(x, c, ctx, c_ctx, ada_w, ada_b, norm_w, e_w_in, e_q_gain, e_k_gain, e_w_out,
              e_ffn_gate, e_ffn_up, e_ffn_down, o_w_in, o_conv_w, o_w_out, o_router,
              o_exp_gate, o_exp_up, o_exp_down):
    ang_row, ang_col = axial_rope_angles(x.shape[1])
    s_c = jax.nn.silu(c)
    s_ctx = jax.nn.silu(c_ctx)
    for i in range(DEPTH):
        even = (i % 2 == 0)
        ctx_out = any(j % 2 == 0 for j in range(i + 1, DEPTH))
        ctx_in = even or ctx_out
        g_pre_mix, g_post_mix, g_pre_ch, g_post_ch = norm_w[i, 0], norm_w[i, 1], norm_w[i, 2], norm_w[i, 3]
        sh1, sc1, gt1, sh2, sc2, gt2 = jnp.split((s_c @ ada_w[i] + ada_b[i])[:, None, :], 6, axis=-1)
        h_lat = modulate(x, g_pre_mix, sh1, sc1)
        h_ctx = None
        if ctx_in:
            csh1, csc1, cgt1, csh2, csc2, cgt2 = jnp.split(s_ctx @ ada_w[i] + ada_b[i], 6, axis=-1)
            h_ctx = modulate(ctx, g_pre_mix, csh1, csc1)
        if even:
            e = i // 2
            y_lat, y_ctx = parallel_fourier_gqa(h_lat, h_ctx, e_w_in[e], e_q_gain[e], e_k_gain[e],
                                                e_w_out[e], ang_row, ang_col, ctx_out)
        else:
            o = i // 2
            y_lat = gated_short_conv(h_lat, o_w_in[o], o_conv_w[o], o_w_out[o])
            y_ctx = gated_short_conv(h_ctx, o_w_in[o], o_conv_w[o], o_w_out[o]) if ctx_out else None
        x = x + gt1 * rmsnorm(y_lat, g_post_mix)
        if ctx_out:
            ctx = ctx + cgt1 * rmsnorm(y_ctx, g_post_mix)
        h_lat = modulate(x, g_pre_ch, sh2, sc2)
        if even:
            y_lat = swiglu(h_lat, e_ffn_gate[e], e_ffn_up[e], e_ffn_down[e])
        else:
            y_lat = moe_swiglu(h_lat, o_router[o], o_exp_gate[o], o_exp_up[o], o_exp_down[o])
        x = x + gt2 * rmsnorm(y_lat, g_post_ch)
        if ctx_out:
            h_ctx = modulate(ctx, g_pre_ch, csh2, csc2)
            if even:
                y_ctx = swiglu(h_ctx, e_ffn_gate[e], e_ffn_up[e], e_ffn_down[e])
            else:
                y_ctx = moe_swiglu(h_ctx, o_router[o], o_exp_gate[o], o_exp_up[o], o_exp_down[o])
            ctx = ctx + cgt2 * rmsnorm(y_ctx, g_post_ch)
    return x
```

```python
import functools

import numpy as np
import jax
import jax.numpy as jnp
from jax import lax
from jax.experimental import pallas as pl
from jax.experimental.pallas import tpu as pltpu

D_MODEL = 1024
GRID_W = 64
HEAD_DIM = 64
N_Q_HEADS = 12
N_KV_HEADS = 4
Q_PER_KV = N_Q_HEADS // N_KV_HEADS
ATTN_WIDTH = N_Q_HEADS * HEAD_DIM
KV_WIDTH = N_KV_HEADS * HEAD_DIM
QK_WIDTH = ATTN_WIDTH + KV_WIDTH
N_FOURIER_GROUPS = 4
FOURIER_GROUP_DIM = 64
FOURIER_WIDTH = N_FOURIER_GROUPS * FOURIER_GROUP_DIM
EVEN_IN_WIDTH = ATTN_WIDTH + 2 * KV_WIDTH + FOURIER_WIDTH
D_FF = 2816
N_EXPERTS = 8
D_EXPERT = 3584
ROPE_THETA = 10000.0
ROPE_HALF = HEAD_DIM // 4
NORM_EPS = 1e-6

LANES = 128
VMEM_LIMIT = 56 * 1024 * 1024

BF16 = jnp.bfloat16
F32 = jnp.float32


def _cparams(semantics, vmem=VMEM_LIMIT):
    return pltpu.CompilerParams(dimension_semantics=semantics, vmem_limit_bytes=vmem)


def _dot(a, b):
    return jnp.dot(a, b, preferred_element_type=F32)


def _split_bf16(x):
    hi = x.astype(BF16)
    lo = (x - hi.astype(F32)).astype(BF16)
    return hi, lo


def _rms(x, g):
    return x * lax.rsqrt(jnp.mean(x * x, axis=-1, keepdims=True) + NORM_EPS) * g


def _mod_rows(mod_ref, row):
    return [mod_ref[0, row:row + 1, i * D_MODEL:(i + 1) * D_MODEL] for i in range(6)]


@functools.lru_cache(maxsize=None)
def _rope_tables(n_tokens):
    t = np.arange(n_tokens)
    row = (t // GRID_W).astype(np.float64)
    col = (t % GRID_W).astype(np.float64)
    inv = ROPE_THETA ** (-np.arange(ROPE_HALF, dtype=np.float64) / ROPE_HALF)
    ar, ac = row[:, None] * inv, col[:, None] * inv
    cos = np.concatenate([np.cos(ar), np.cos(ar), np.cos(ac), np.cos(ac)], axis=-1)
    sin = np.concatenate([-np.sin(ar), np.sin(ar), -np.sin(ac), np.sin(ac)], axis=-1)
    reps = LANES // HEAD_DIM
    return (np.tile(cos, (1, reps)).astype(np.float32), np.tile(sin, (1, reps)).astype(np.float32))


@functools.lru_cache(maxsize=None)
def _head_matrices():
    head = np.arange(QK_WIDTH) // HEAD_DIM
    red = (head[:, None] == np.arange(LANES)[None, :]).astype(np.float32) / HEAD_DIM
    exp = (np.arange(LANES)[:, None] == head[None, :]).astype(np.float32)
    return red, exp


@functools.lru_cache(maxsize=None)
def _fourier_tables(n_tokens, kb):
    n2 = LANES
    n1 = n_tokens // n2
    c = np.arange(FOURIER_GROUP_DIM)
    ang = 2 * np.pi * np.outer(c, c) / FOURIER_GROUP_DIM
    eye = np.eye(N_FOURIER_GROUPS)
    cs = np.concatenate([np.kron(eye, np.cos(ang)), np.kron(eye, np.sin(ang))], axis=1)
    k1 = np.arange(n1)
    th = 2 * np.pi * np.outer(k1, k1) / n1
    cr, ci = np.cos(th), -np.sin(th)
    base = np.block([[cr, ci], [ci, -cr]])
    psi = 2 * np.pi * np.outer(k1, np.arange(n2)) / n_tokens
    twr = np.cos(psi).reshape(n1, n2 // kb, kb).transpose(1, 0, 2)
    twi = (-np.sin(psi)).reshape(n1, n2 // kb, kb).transpose(1, 0, 2)
    k2 = np.arange(n2)
    ph = 2 * np.pi * np.outer(k2, k2) / n2
    fr, fi = np.cos(ph), -np.sin(ph)
    scale = 1.0 / np.sqrt(n_tokens * FOURIER_GROUP_DIM)
    m3 = np.stack([fr, -fi], axis=-1) * scale
    wb = np.einsum('knr,uv->kunrv', m3, np.eye(kb)).reshape(n2 * kb, n2 * 2 * kb)
    f32 = np.float32
    return cs.astype(f32), base.astype(f32), twr.astype(f32), twi.astype(f32), wb.astype(f32)


def _ada_kernel(cb_ref, w_ref, b_ref, o_ref):
    tn = o_ref.shape[-1]
    o_ref[...] = jnp.zeros(o_ref.shape, F32)
    for r in range(2):
        cb = cb_ref[r]
        s = cb * jax.nn.sigmoid(cb)
        for j in range(tn // LANES):
            sl = slice(j * LANES, (j + 1) * LANES)
            col = jnp.sum(s * w_ref[0, :, sl], axis=0, keepdims=True)
            o_ref[0, r:r + 1, sl] = col + b_ref[0, :, sl]


def _ada(c, c_ctx, ada_w, ada_b):
    depth = ada_w.shape[0]
    n = ada_w.shape[-1]
    tn = 1536
    cb = jnp.stack([jnp.broadcast_to(c[0][:, None], (D_MODEL, LANES)),
                    jnp.broadcast_to(c_ctx[:, None], (D_MODEL, LANES))])
    return pl.pallas_call(
        _ada_kernel,
        out_shape=jax.ShapeDtypeStruct((depth, 8, n), F32),
        grid=(depth, n // tn),
        in_specs=[pl.BlockSpec((2, D_MODEL, LANES), lambda i, j: (0, 0, 0)),
                  pl.BlockSpec((1, D_MODEL, tn), lambda i, j: (i, 0, j)),
                  pl.BlockSpec((1, 1, tn), lambda i, j: (i, 0, j))],
        out_specs=pl.BlockSpec((1, 8, tn), lambda i, j: (i, 0, j)),
        compiler_params=_cparams(("parallel", "parallel")),
        name="ada",
    )(cb, ada_w, ada_b[:, None, :])


def _evenproj_kernel(*refs, row, latent):
    if latent:
        (x_ref, nw_ref, mod_ref, w_ref, gain_ref, red_ref, exp_ref, cos_ref, sin_ref,
         qT_ref, k_ref, vT_ref, f_ref) = refs
    else:
        x_ref, nw_ref, mod_ref, w_ref, gain_ref, red_ref, exp_ref, k_ref, vT_ref = refs
    sh, sc = _mod_rows(mod_ref, row)[:2]
    h = (_rms(x_ref[...], nw_ref[0, 0:1, :]) * (1.0 + sc) + sh).astype(BF16)
    z = _dot(h, w_ref[...])
    zqk = z[:, :QK_WIDTH]
    hi, lo = _split_bf16(zqk * zqk)
    red = red_ref[...].astype(BF16)
    ms = _dot(hi, red) + _dot(lo, red)
    rhi, rlo = _split_bf16(lax.rsqrt(ms + NORM_EPS))
    expm = exp_ref[...].astype(BF16)
    yn = zqk * (_dot(rhi, expm) + _dot(rlo, expm)) * gain_ref[...]
    if latent:
        lane = lax.broadcasted_iota(jnp.int32, (1, LANES), 1)
        first_half = (lane // ROPE_HALF) % 2 == 0
        cos, sin = cos_ref[...], sin_ref[...]
        chunks = []
        for c in range(QK_WIDTH // LANES):
            yc = yn[:, c * LANES:(c + 1) * LANES]
            partner = jnp.where(first_half, pltpu.roll(yc, LANES - ROPE_HALF, axis=1),
                                pltpu.roll(yc, ROPE_HALF, axis=1))
            chunks.append(yc * cos + partner * sin)
        yn = jnp.concatenate(chunks, axis=1)
        qT_ref[...] = yn[:, :ATTN_WIDTH].T.astype(BF16)
        f_ref[...] = z[:, QK_WIDTH + KV_WIDTH:].astype(BF16)
    for g in range(N_KV_HEADS):
        k_ref[g] = yn[:, ATTN_WIDTH + g * HEAD_DIM:ATTN_WIDTH + (g + 1) * HEAD_DIM].astype(BF16)
    vT_ref[...] = z[:, QK_WIDTH:QK_WIDTH + KV_WIDTH].T.astype(BF16)


def _evenproj(x2d, nw, mod, w_bf, gain, *, latent, tm):
    n = x2d.shape[0]
    red, expm = _head_matrices()
    const = lambda i: (0, 0)
    in_specs = [pl.BlockSpec((tm, D_MODEL), lambda i: (i, 0)),
                pl.BlockSpec((1, 4, D_MODEL), lambda i: (0, 0, 0)),
                pl.BlockSpec((1, 8, 6 * D_MODEL), lambda i: (0, 0, 0)),
                pl.BlockSpec((D_MODEL, EVEN_IN_WIDTH), const),
                pl.BlockSpec((1, QK_WIDTH), const),
                pl.BlockSpec((QK_WIDTH, LANES), const),
                pl.BlockSpec((LANES, QK_WIDTH), const)]
    args = [x2d, nw, mod, w_bf, gain, jnp.asarray(red), jnp.asarray(expm)]
    k_shape = jax.ShapeDtypeStruct((N_KV_HEADS, n, HEAD_DIM), BF16)
    vT_shape = jax.ShapeDtypeStruct((KV_WIDTH, n), BF16)
    k_spec = pl.BlockSpec((N_KV_HEADS, tm, HEAD_DIM), lambda i: (0, i, 0))
    vT_spec = pl.BlockSpec((KV_WIDTH, tm), lambda i: (0, i))
    if latent:
        cos, sin = _rope_tables(n)
        in_specs += [pl.BlockSpec((tm, LANES), lambda i: (i, 0))] * 2
        args += [jnp.asarray(cos), jnp.asarray(sin)]
        out_shape = (jax.ShapeDtypeStruct((ATTN_WIDTH, n), BF16), k_shape, vT_shape,
                     jax.ShapeDtypeStruct((n, FOURIER_WIDTH), BF16))
        out_specs = (pl.BlockSpec((ATTN_WIDTH, tm), lambda i: (0, i)), k_spec, vT_spec,
                     pl.BlockSpec((tm, FOURIER_WIDTH), lambda i: (i, 0)))
    else:
        out_shape = (k_shape, vT_shape)
        out_specs = (k_spec, vT_spec)
    return pl.pallas_call(
        functools.partial(_evenproj_kernel, row=0 if latent else 1, latent=latent),
        out_shape=out_shape, grid=(n // tm,), in_specs=in_specs, out_specs=out_specs,
        compiler_params=_cparams(("parallel",)),
        name="evenproj_lat" if latent else "evenproj_ctx",
    )(*args)


def _attn_kernel(qT_ref, k_ref, vT_ref, kc_ref, vcT_ref, o_ref, m_sc, l_sc, acc_sc, *, tk):
    n_lat = k_ref.shape[1]
    m_sc[...] = jnp.full(m_sc.shape, -jnp.inf, F32)
    l_sc[...] = jnp.zeros(l_sc.shape, F32)
    acc_sc[...] = jnp.zeros(acc_sc.shape, F32)

    def visit(kt, vt):
        for j in range(Q_PER_KV):
            s = _dot(kt, qT_ref[j * HEAD_DIM:(j + 1) * HEAD_DIM, :])
            m_old = m_sc[j]
            m_new = jnp.maximum(m_old, jnp.max(s, axis=0, keepdims=True))
            alpha = jnp.exp(m_old - m_new)
            p = jnp.exp(s - m_new)
            l_sc[j] = alpha * l_sc[j] + jnp.sum(p, axis=0, keepdims=True)
            acc_sc[j] = alpha * acc_sc[j] + _dot(vt, p.astype(BF16))
            m_sc[j] = m_new

    def body(c, carry):
        off = pl.multiple_of(c * tk, tk)
        visit(k_ref[0, pl.ds(off, tk), :], vT_ref[:, pl.ds(off, tk)])
        return carry

    lax.fori_loop(0, n_lat // tk, body, 0)
    visit(kc_ref[0], vcT_ref[...])
    for j in range(Q_PER_KV):
        o_ref[j * HEAD_DIM:(j + 1) * HEAD_DIM, :] = (acc_sc[j] / l_sc[j]).astype(BF16)


def _attention(qT, k, vT, kc, vcT, *, tq, tk):
    n = qT.shape[1]
    n_ctx = kc.shape[1]
    gw = Q_PER_KV * HEAD_DIM
    return pl.pallas_call(
        functools.partial(_attn_kernel, tk=tk),
        out_shape=jax.ShapeDtypeStruct((ATTN_WIDTH, n), BF16),
        grid=(N_KV_HEADS, n // tq),
        in_specs=[pl.BlockSpec((gw, tq), lambda g, i: (g, i)),
                  pl.BlockSpec((1, n, HEAD_DIM), lambda g, i: (g, 0, 0)),
                  pl.BlockSpec((HEAD_DIM, n), lambda g, i: (g, 0)),
                  pl.BlockSpec((1, n_ctx, HEAD_DIM), lambda g, i: (g, 0, 0)),
                  pl.BlockSpec((HEAD_DIM, n_ctx), lambda g, i: (g, 0))],
        out_specs=pl.BlockSpec((gw, tq), lambda g, i: (g, i)),
        scratch_shapes=[pltpu.VMEM((Q_PER_KV, 1, tq), F32), pltpu.VMEM((Q_PER_KV, 1, tq), F32),
                        pltpu.VMEM((Q_PER_KV, HEAD_DIM, tq), F32)],
        compiler_params=_cparams(("parallel", "parallel")),
        name="attn",
    )(qT, k, vT, kc, vcT)


def _four_a_kernel(f_ref, cs_ref, base_ref, twr_ref, twi_ref, y_ref, *, nb):
    n1 = f_ref.shape[0]
    cs = cs_ref[...].astype(BF16)
    base = base_ref[...].astype(BF16)
    for u in range(nb):
        xb = f_ref[:, u * FOURIER_WIDTH:(u + 1) * FOURIER_WIDTH]
        ab = _dot(xb, cs)
        stacked = jnp.concatenate([ab[:, :FOURIER_WIDTH], ab[:, FOURIER_WIDTH:]], axis=0)
        p = _dot(base, stacked.astype(BF16))
        pr, pi = p[:n1], p[n1:]
        tr = twr_ref[0, :, u:u + 1]
        ti = twi_ref[0, :, u:u + 1]
        y_ref[u, 0] = tr * pr - ti * pi
        y_ref[u, 1] = tr * pi + ti * pr


def _four_b_kernel(y_ref, wb_ref, o_ref):
    n2, _, kb, w = y_ref.shape
    y = y_ref[...].reshape(n2 * 2 * kb, w).astype(BF16)
    o_ref[...] = _dot(wb_ref[...].astype(BF16), y).reshape(n2, kb, w)


def _fourier(f):
    n = f.shape[0]
    n2 = LANES
    n1 = n // n2
    nb = kb = 8
    cs, base, twr, twi, wb = (jnp.asarray(t) for t in _fourier_tables(n, kb))
    f2d = f.reshape(n1, n2 * FOURIER_WIDTH)
    y = pl.pallas_call(
        functools.partial(_four_a_kernel, nb=nb),
        out_shape=jax.ShapeDtypeStruct((n2, 2, n1, FOURIER_WIDTH), F32),
        grid=(n2 // nb,),
        in_specs=[pl.BlockSpec((n1, nb * FOURIER_WIDTH), lambda s: (0, s)),
                  pl.BlockSpec(cs.shape, lambda s: (0, 0)),
                  pl.BlockSpec(base.shape, lambda s: (0, 0)),
                  pl.BlockSpec((1, n1, nb), lambda s: (s, 0, 0)),
                  pl.BlockSpec((1, n1, nb), lambda s: (s, 0, 0))],
        out_specs=pl.BlockSpec((nb, 2, n1, FOURIER_WIDTH), lambda s: (s, 0, 0, 0)),
        compiler_params=_cparams(("parallel",)),
        name="four_a",
    )(f2d, cs, base, twr, twi)
    out = pl.pallas_call(
        _four_b_kernel,
        out_shape=jax.ShapeDtypeStruct((n2, n1, FOURIER_WIDTH), F32),
        grid=(n1 // kb,),
        in_specs=[pl.BlockSpec((n2, 2, kb, FOURIER_WIDTH), lambda s: (0, 0, s, 0)),
                  pl.BlockSpec(wb.shape, lambda s: (0, 0))],
        out_specs=pl.BlockSpec((n2, kb, FOURIER_WIDTH), lambda s: (0, s, 0)),
        compiler_params=_cparams(("parallel",)),
        name="four_b",
    )(y, wb)
    return out.reshape(n, FOURIER_WIDTH)


def _outproj_kernel(aT_ref, four_ref, x_ref, wa_ref, wf_ref, nw_ref, mod_ref, x1_ref, h_ref):
    _, _, gt1, sh2, sc2, _ = _mod_rows(mod_ref, 0)
    y = lax.dot_general(aT_ref[...], wa_ref[...], (((0,), (0,)), ((), ())), preferred_element_type=F32)
    y = y + _dot(four_ref[...].astype(BF16), wf_ref[...])
    x1 = x_ref[...] + gt1 * _rms(y, nw_ref[0, 1:2, :])
    x1_ref[...] = x1
    h_ref[...] = (_rms(x1, nw_ref[0, 2:3, :]) * (1.0 + sc2) + sh2).astype(BF16)


def _outproj(attnT, four, x2d, wa, wf, nw, mod, *, tm):
    n = x2d.shape[0]
    const = lambda i: (0, 0)
    return pl.pallas_call(
        _outproj_kernel,
        out_shape=(jax.ShapeDtypeStruct((n, D_MODEL), F32), jax.ShapeDtypeStruct((n, D_MODEL), BF16)),
        grid=(n // tm,),
        in_specs=[pl.BlockSpec((ATTN_WIDTH, tm), lambda i: (0, i)),
                  pl.BlockSpec((tm, FOURIER_WIDTH), lambda i: (i, 0)),
                  pl.BlockSpec((tm, D_MODEL), lambda i: (i, 0)),
                  pl.BlockSpec((ATTN_WIDTH, D_MODEL), const),
                  pl.BlockSpec((FOURIER_WIDTH, D_MODEL), const),
                  pl.BlockSpec((1, 4, D_MODEL), lambda i: (0, 0, 0)),
                  pl.BlockSpec((1, 8, 6 * D_MODEL), lambda i: (0, 0, 0))],
        out_specs=(pl.BlockSpec((tm, D_MODEL), lambda i: (i, 0)),
                   pl.BlockSpec((tm, D_MODEL), lambda i: (i, 0))),
        compiler_params=_cparams(("parallel",)),
        name="outproj",
    )(attnT, four, x2d, wa, wf, nw, mod)


def _ffn_kernel(h_ref, x_ref, wg_ref, wu_ref, wd_ref, nw_ref, mod_ref, nw1_ref, mod1_ref,
                x2_ref, h3_ref, acc_ref):
    c = pl.program_id(1)
    h = h_ref[...]
    g = _dot(h, wg_ref[...])
    u = _dot(h, wu_ref[...])
    part = _dot((g * jax.nn.sigmoid(g) * u).astype(BF16), wd_ref[...])

    @pl.when(c == 0)
    def _():
        acc_ref[...] = part

    @pl.when(c > 0)
    def _():
        acc_ref[...] += part

    @pl.when(c == pl.num_programs(1) - 1)
    def _():
        gt2 = _mod_rows(mod_ref, 0)[5]
        sh, sc = _mod_rows(mod1_ref, 0)[:2]
        x2 = x_ref[...] + gt2 * _rms(acc_ref[...], nw_ref[0, 3:4, :])
        x2_ref[...] = x2
        h3_ref[...] = (_rms(x2, nw1_ref[0, 0:1, :]) * (1.0 + sc) + sh).astype(BF16)


def _ffn(h, x1, wg, wu, wd, nw, mod, nw1, mod1, *, tm, tf):
    n = h.shape[0]
    nwspec = pl.BlockSpec((1, 4, D_MODEL), lambda i, c: (0, 0, 0))
    modspec = pl.BlockSpec((1, 8, 6 * D_MODEL), lambda i, c: (0, 0, 0))
    return pl.pallas_call(
        _ffn_kernel,
        out_shape=(jax.ShapeDtypeStruct((n, D_MODEL), F32), jax.ShapeDtypeStruct((n, D_MODEL), BF16)),
        grid=(n // tm, D_FF // tf),
        in_specs=[pl.BlockSpec((tm, D_MODEL), lambda i, c: (i, 0)),
                  pl.BlockSpec((tm, D_MODEL), lambda i, c: (i, 0)),
                  pl.BlockSpec((D_MODEL, tf), lambda i, c: (0, c)),
                  pl.BlockSpec((D_MODEL, tf), lambda i, c: (0, c)),
                  pl.BlockSpec((tf, D_MODEL), lambda i, c: (c, 0)),
                  nwspec, modspec, nwspec, modspec],
        out_specs=(pl.BlockSpec((tm, D_MODEL), lambda i, c: (i, 0)),
                   pl.BlockSpec((tm, D_MODEL), lambda i, c: (i, 0))),
        scratch_shapes=[pltpu.VMEM((tm, D_MODEL), F32)],
        compiler_params=_cparams(("parallel", "arbitrary")),
        name="ffn",
    )(h, x1, wg, wu, wd, nw, mod, nw1, mod1)


EDGE_ROWS = 16


def _convedge_kernel(hf_ref, hl_ref, wc_ref, wu_ref, vf_ref, vl_ref):
    nt = hf_ref.shape[0]
    for h_ref, v_ref in ((hf_ref, vf_ref), (hl_ref, vl_ref)):
        h = h_ref[...].reshape(nt * EDGE_ROWS, D_MODEL)
        v = _dot(h, wc_ref[...]) * _dot(h, wu_ref[...])
        v_ref[...] = v.reshape(nt, EDGE_ROWS, D_MODEL)


def _convedge(h3, w_in_bf, *, tm):
    n = h3.shape[0]
    nt = n // tm
    h3t = h3.reshape(nt, tm, D_MODEL)
    last = tm // EDGE_ROWS - 1
    shape = jax.ShapeDtypeStruct((nt, EDGE_ROWS, D_MODEL), F32)
    return pl.pallas_call(
        _convedge_kernel,
        out_shape=(shape, shape),
        grid=(1,),
        in_specs=[pl.BlockSpec((nt, EDGE_ROWS, D_MODEL), lambda i: (0, 0, 0)),
                  pl.BlockSpec((nt, EDGE_ROWS, D_MODEL), lambda i: (0, last, 0)),
                  pl.BlockSpec((D_MODEL, D_MODEL), lambda i: (0, 1)),
                  pl.BlockSpec((D_MODEL, D_MODEL), lambda i: (0, 2))],
        out_specs=(pl.BlockSpec((nt, EDGE_ROWS, D_MODEL), lambda i: (0, 0, 0)),
                   pl.BlockSpec((nt, EDGE_ROWS, D_MODEL), lambda i: (0, 0, 0))),
        compiler_params=_cparams(("arbitrary",)),
        name="convedge",
    )(h3t, h3t, w_in_bf, w_in_bf)


def _convmix_kernel(h_ref, x_ref, vl_ref, vf_ref, win_ref, cw_ref, wout_ref, nw_ref, mod_ref, r_ref,
                    x3_ref, h4_ref, comb_ref):
    i = pl.program_id(0)
    tm = h_ref.shape[0]
    _, _, gt1, sh2, sc2, _ = _mod_rows(mod_ref, 0)
    z = _dot(h_ref[...], win_ref[...])
    b = z[:, :D_MODEL]
    v = z[:, D_MODEL:2 * D_MODEL] * z[:, 2 * D_MODEL:]
    has_prev = (i > 0).astype(F32)
    has_next = (i < pl.num_programs(0) - 1).astype(F32)
    prev_row = vl_ref[0, EDGE_ROWS - 1:EDGE_ROWS, :] * has_prev
    next_row = vf_ref[0, 0:1, :] * has_next
    rows = lax.broadcasted_iota(jnp.int32, (tm, 1), 0)
    v_dn = jnp.where(rows == 0, prev_row, pltpu.roll(v, 1, axis=0))
    v_up = jnp.where(rows == tm - 1, next_row, pltpu.roll(v, tm - 1, axis=0))
    conv = v_dn * cw_ref[0, 0:1, :] + v * cw_ref[0, 1:2, :] + v_up * cw_ref[0, 2:3, :]
    y = _dot((b * conv).astype(BF16), wout_ref[...])
    x3 = x_ref[...] + gt1 * _rms(y, nw_ref[0, 1:2, :])
    x3_ref[...] = x3
    h4 = _rms(x3, nw_ref[0, 2:3, :]) * (1.0 + sc2) + sh2
    h4_ref[...] = h4.astype(BF16)
    hhi, hlo = _split_bf16(h4)
    rhi, rlo = _split_bf16(r_ref[...])
    logits = _dot(hhi, rhi) + (_dot(hlo, rhi) + _dot(hhi, rlo))
    lane = lax.broadcasted_iota(jnp.int32, logits.shape, 1)
    logits = jnp.where(lane < N_EXPERTS, logits, -jnp.inf)
    e = jnp.exp(logits - jnp.max(logits, axis=-1, keepdims=True))
    probs = e / jnp.sum(e, axis=-1, keepdims=True)
    v1 = jnp.max(probs, axis=-1, keepdims=True)
    i1 = jnp.min(jnp.where(probs == v1, lane, LANES), axis=-1, keepdims=True)
    rest = jnp.where(lane == i1, -1.0, probs)
    v2 = jnp.max(rest, axis=-1, keepdims=True)
    i2 = jnp.min(jnp.where(rest == v2, lane, LANES), axis=-1, keepdims=True)
    tot = v1 + v2
    comb_ref[...] = jnp.where(lane == i1, v1 / tot, 0.0) + jnp.where(lane == i2, v2 / tot, 0.0)


def _convmix(h3, x2, v_first, v_last, w_in_bf, conv_w, w_out_bf, nw, mod, router_pad, *, tm):
    n = h3.shape[0]
    nt = n // tm
    const = lambda i: (0, 0)
    return pl.pallas_call(
        _convmix_kernel,
        out_shape=(jax.ShapeDtypeStruct((n, D_MODEL), F32), jax.ShapeDtypeStruct((n, D_MODEL), BF16),
                   jax.ShapeDtypeStruct((n, LANES), F32)),
        grid=(nt,),
        in_specs=[pl.BlockSpec((tm, D_MODEL), lambda i: (i, 0)),
                  pl.BlockSpec((tm, D_MODEL), lambda i: (i, 0)),
                  pl.BlockSpec((1, EDGE_ROWS, D_MODEL), lambda i: (jnp.maximum(i - 1, 0), 0, 0)),
                  pl.BlockSpec((1, EDGE_ROWS, D_MODEL), lambda i: (jnp.minimum(i + 1, nt - 1), 0, 0)),
                  pl.BlockSpec((D_MODEL, 3 * D_MODEL), const),
                  pl.BlockSpec((1, 3, D_MODEL), lambda i: (0, 0, 0)),
                  pl.BlockSpec((D_MODEL, D_MODEL), const),
                  pl.BlockSpec((1, 4, D_MODEL), lambda i: (0, 0, 0)),
                  pl.BlockSpec((1, 8, 6 * D_MODEL), lambda i: (0, 0, 0)),
                  pl.BlockSpec((D_MODEL, LANES), const)],
        out_specs=(pl.BlockSpec((tm, D_MODEL), lambda i: (i, 0)),
                   pl.BlockSpec((tm, D_MODEL), lambda i: (i, 0)),
                   pl.BlockSpec((tm, LANES), lambda i: (i, 0))),
        compiler_params=_cparams(("parallel",)),
        name="convmix",
    )(h3, x2, v_last, v_first, w_in_bf, conv_w, w_out_bf, nw, mod, router_pad)


def _moe_kernel(h_ref, comb_ref, x_ref, wg_ref, wu_ref, wd_ref, nw_ref, mod_ref, o_ref, acc_ref):
    e = pl.program_id(1)
    c = pl.program_id(2)
    first = jnp.logical_and(e == 0, c == 0)
    last = jnp.logical_and(e == pl.num_programs(1) - 1, c == pl.num_programs(2) - 1)
    h = h_ref[...]
    comb = comb_ref[...]
    lane = lax.broadcasted_iota(jnp.int32, comb.shape, 1)
    ce = jnp.sum(jnp.where(lane == e, comb, 0.0), axis=-1, keepdims=True)
    g = _dot(h, wg_ref[0, 0].astype(BF16))
    u = _dot(h, wu_ref[0, 0].astype(BF16))
    part = _dot((g * jax.nn.sigmoid(g) * u * ce).astype(BF16), wd_ref[0, 0].astype(BF16))

    @pl.when(first)
    def _():
        acc_ref[...] = part

    @pl.when(jnp.logical_not(first))
    def _():
        acc_ref[...] += part

    @pl.when(last)
    def _():
        gt2 = _mod_rows(mod_ref, 0)[5]
        o_ref[...] = x_ref[...] + gt2 * _rms(acc_ref[...], nw_ref[0, 3:4, :])


def _moe(h4, comb, x3, wg, wu, wd, nw, mod, *, tm, tf):
    n = h4.shape[0]
    return pl.pallas_call(
        _moe_kernel,
        out_shape=jax.ShapeDtypeStruct((n, D_MODEL), F32),
        grid=(n // tm, N_EXPERTS, D_EXPERT // tf),
        in_specs=[pl.BlockSpec((tm, D_MODEL), lambda i, e, c: (i, 0)),
                  pl.BlockSpec((tm, LANES), lambda i, e, c: (i, 0)),
                  pl.BlockSpec((tm, D_MODEL), lambda i, e, c: (i, 0)),
                  pl.BlockSpec((1, 1, D_MODEL, tf), lambda i, e, c: (0, e, 0, c)),
                  pl.BlockSpec((1, 1, D_MODEL, tf), lambda i, e, c: (0, e, 0, c)),
                  pl.BlockSpec((1, 1, tf, D_MODEL), lambda i, e, c: (0, e, c, 0)),
                  pl.BlockSpec((1, 4, D_MODEL), lambda i, e, c: (0, 0, 0)),
                  pl.BlockSpec((1, 8, 6 * D_MODEL), lambda i, e, c: (0, 0, 0))],
        out_specs=pl.BlockSpec((tm, D_MODEL), lambda i, e, c: (i, 0)),
        scratch_shapes=[pltpu.VMEM((tm, D_MODEL), F32)],
        compiler_params=_cparams(("parallel", "arbitrary", "arbitrary")),
        name="moe",
    )(h4, comb, x3, wg, wu, wd, nw, mod)


def kernel(x, c, ctx, c_ctx, ada_w, ada_b, norm_w, e_w_in, e_q_gain, e_k_gain, e_w_out, e_ffn_gate,
           e_ffn_up, e_ffn_down, o_w_in, o_conv_w, o_w_out, o_router, o_exp_gate, o_exp_up, o_exp_down):
    assert x.shape[0] == 1 and x.shape[2] == D_MODEL and ada_w.shape[0] == 2
    n = x.shape[1]
    x2d = x[0]
    ctx2d = ctx[0]
    mod = _ada(c, c_ctx, ada_w, ada_b)
    mod0, mod1 = mod[0:1], mod[1:2]
    nw0, nw1 = norm_w[0:1], norm_w[1:2]

    w_in = e_w_in[0].astype(BF16)
    scale = HEAD_DIM ** -0.5
    gain = jnp.concatenate([jnp.tile(e_q_gain[0], N_Q_HEADS) * scale,
                            jnp.tile(e_k_gain[0], N_KV_HEADS)])[None, :]
    qT, k, vT, f = _evenproj(x2d, nw0, mod0, w_in, gain, latent=True, tm=512)
    kc, vcT = _evenproj(ctx2d, nw0, mod0, w_in, gain, latent=False, tm=ctx2d.shape[0])
    attnT = _attention(qT, k, vT, kc, vcT, tq=512, tk=512)
    four = _fourier(f)
    w_out = e_w_out[0].astype(BF16)
    x1, h2 = _outproj(attnT, four, x2d, w_out[:ATTN_WIDTH], w_out[ATTN_WIDTH:], nw0, mod0, tm=512)
    x2, h3 = _ffn(h2, x1, e_ffn_gate[0].astype(BF16), e_ffn_up[0].astype(BF16),
                  e_ffn_down[0].astype(BF16), nw0, mod0, nw1, mod1, tm=1024, tf=1408)

    conv_tm = 512
    ow_in = o_w_in[0].astype(BF16)
    v_first, v_last = _convedge(h3, ow_in, tm=conv_tm)
    router_pad = jnp.pad(o_router[0], ((0, 0), (0, LANES - N_EXPERTS)))
    x3, h4, comb = _convmix(h3, x2, v_first, v_last, ow_in, o_conv_w, o_w_out[0].astype(BF16),
                            nw1, mod1, router_pad, tm=conv_tm)
    out = _moe(h4, comb, x3, o_exp_gate, o_exp_up, o_exp_down, nw1, mod1, tm=1024, tf=512)
    return out[None]
```

```python
import functools

import numpy as np
import jax
import jax.numpy as jnp
from jax import lax
from jax.experimental import pallas as pl
from jax.experimental.pallas import tpu as pltpu

D_MODEL = 1024
GRID_W = 64
HEAD_DIM = 64
N_Q_HEADS = 12
N_KV_HEADS = 4
Q_PER_KV = N_Q_HEADS // N_KV_HEADS
ATTN_WIDTH = N_Q_HEADS * HEAD_DIM
KV_WIDTH = N_KV_HEADS * HEAD_DIM
QK_WIDTH = ATTN_WIDTH + KV_WIDTH
N_FOURIER_GROUPS = 4
FOURIER_GROUP_DIM = 64
FOURIER_WIDTH = N_FOURIER_GROUPS * FOURIER_GROUP_DIM
EVEN_IN_WIDTH = ATTN_WIDTH + 2 * KV_WIDTH + FOURIER_WIDTH
D_FF = 2816
N_EXPERTS = 8
D_EXPERT = 3584
ONES_ROWS = 16
V_ROWS = HEAD_DIM + ONES_ROWS
EXP2_SAFE_BOUND = 60.0
ROPE_THETA = 10000.0
ROPE_HALF = HEAD_DIM // 4
NORM_EPS = 1e-6

LANES = 128
VMEM_LIMIT = 56 * 1024 * 1024

BF16 = jnp.bfloat16
F32 = jnp.float32


def _cparams(semantics, vmem=VMEM_LIMIT):
    return pltpu.CompilerParams(dimension_semantics=semantics, vmem_limit_bytes=vmem)


def _dot(a, b):
    return jnp.dot(a, b, preferred_element_type=F32)


def _split_bf16(x):
    hi = x.astype(BF16)
    lo = (x - hi.astype(F32)).astype(BF16)
    return hi, lo


def _rms(x, g):
    return x * lax.rsqrt(jnp.mean(x * x, axis=-1, keepdims=True) + NORM_EPS) * g


def _mod_rows(mod_ref, row):
    return [mod_ref[0, row:row + 1, i * D_MODEL:(i + 1) * D_MODEL] for i in range(6)]


@functools.lru_cache(maxsize=None)
def _rope_tables(n_tokens):
    t = np.arange(n_tokens)
    row = (t // GRID_W).astype(np.float64)
    col = (t % GRID_W).astype(np.float64)
    inv = ROPE_THETA ** (-np.arange(ROPE_HALF, dtype=np.float64) / ROPE_HALF)
    ar, ac = row[:, None] * inv, col[:, None] * inv
    cos = np.concatenate([np.cos(ar), np.cos(ar), np.cos(ac), np.cos(ac)], axis=-1)
    sin = np.concatenate([-np.sin(ar), np.sin(ar), -np.sin(ac), np.sin(ac)], axis=-1)
    reps = LANES // HEAD_DIM
    return (np.tile(cos, (1, reps)).astype(np.float32), np.tile(sin, (1, reps)).astype(np.float32))


@functools.lru_cache(maxsize=None)
def _head_matrices():
    head = np.arange(QK_WIDTH) // HEAD_DIM
    red = (head[:, None] == np.arange(LANES)[None, :]).astype(np.float32) / HEAD_DIM
    exp = (np.arange(LANES)[:, None] == head[None, :]).astype(np.float32)
    return red, exp


@functools.lru_cache(maxsize=None)
def _fourier_tables(n_tokens, kb):
    n2 = LANES
    n1 = n_tokens // n2
    c = np.arange(FOURIER_GROUP_DIM)
    ang = 2 * np.pi * np.outer(c, c) / FOURIER_GROUP_DIM
    eye = np.eye(N_FOURIER_GROUPS)
    cs = np.concatenate([np.kron(eye, np.cos(ang)), np.kron(eye, np.sin(ang))], axis=1)
    k1 = np.arange(n1)
    th = 2 * np.pi * np.outer(k1, k1) / n1
    cr, ci = np.cos(th), -np.sin(th)
    base = np.block([[cr, ci], [ci, -cr]])
    psi = 2 * np.pi * np.outer(k1, np.arange(n2)) / n_tokens
    twr = np.cos(psi).reshape(n1, n2 // kb, kb).transpose(1, 0, 2)
    twi = (-np.sin(psi)).reshape(n1, n2 // kb, kb).transpose(1, 0, 2)
    k2 = np.arange(n2)
    ph = 2 * np.pi * np.outer(k2, k2) / n2
    fr, fi = np.cos(ph), -np.sin(ph)
    scale = 1.0 / np.sqrt(n_tokens * FOURIER_GROUP_DIM)
    m3 = np.stack([fr, -fi], axis=-1) * scale
    wb = np.einsum('knr,uv->kunrv', m3, np.eye(kb)).reshape(n2 * kb, n2 * 2 * kb)
    f32 = np.float32
    return cs.astype(f32), base.astype(f32), twr.astype(f32), twi.astype(f32), wb.astype(f32)


def _ada_kernel(cb_ref, w_ref, b_ref, o_ref):
    tn = o_ref.shape[-1]
    o_ref[...] = jnp.zeros(o_ref.shape, F32)
    for r in range(2):
        cb = cb_ref[r]
        s = cb * jax.nn.sigmoid(cb)
        for j in range(tn // LANES):
            sl = slice(j * LANES, (j + 1) * LANES)
            col = jnp.sum(s * w_ref[0, :, sl], axis=0, keepdims=True)
            o_ref[0, r:r + 1, sl] = col + b_ref[0, :, sl]


def _ada(c, c_ctx, ada_w, ada_b):
    depth = ada_w.shape[0]
    n = ada_w.shape[-1]
    tn = 1536
    cb = jnp.stack([jnp.broadcast_to(c[0][:, None], (D_MODEL, LANES)),
                    jnp.broadcast_to(c_ctx[:, None], (D_MODEL, LANES))])
    return pl.pallas_call(
        _ada_kernel,
        out_shape=jax.ShapeDtypeStruct((depth, 8, n), F32),
        grid=(depth, n // tn),
        in_specs=[pl.BlockSpec((2, D_MODEL, LANES), lambda i, j: (0, 0, 0)),
                  pl.BlockSpec((1, D_MODEL, tn), lambda i, j: (i, 0, j)),
                  pl.BlockSpec((1, 1, tn), lambda i, j: (i, 0, j))],
        out_specs=pl.BlockSpec((1, 8, tn), lambda i, j: (i, 0, j)),
        compiler_params=_cparams(("parallel", "parallel")),
        name="ada",
    )(cb, ada_w, ada_b[:, None, :])


def _evenproj_kernel(*refs, row, latent):
    if latent:
        (x_ref, nw_ref, mod_ref, w_ref, gain_ref, red_ref, exp_ref, cos_ref, sin_ref,
         qT_ref, k_ref, vT_ref, f_ref) = refs
    else:
        x_ref, nw_ref, mod_ref, w_ref, gain_ref, red_ref, exp_ref, k_ref, vT_ref = refs
    sh, sc = _mod_rows(mod_ref, row)[:2]
    h = (_rms(x_ref[...], nw_ref[0, 0:1, :]) * (1.0 + sc) + sh).astype(BF16)
    z = _dot(h, w_ref[...])
    zqk = z[:, :QK_WIDTH]
    hi, lo = _split_bf16(zqk * zqk)
    red = red_ref[...].astype(BF16)
    ms = _dot(hi, red) + _dot(lo, red)
    rhi, rlo = _split_bf16(lax.rsqrt(ms + NORM_EPS))
    expm = exp_ref[...].astype(BF16)
    yn = zqk * (_dot(rhi, expm) + _dot(rlo, expm)) * gain_ref[...]
    if latent:
        lane = lax.broadcasted_iota(jnp.int32, (1, LANES), 1)
        first_half = (lane // ROPE_HALF) % 2 == 0
        cos, sin = cos_ref[...], sin_ref[...]
        chunks = []
        for c in range(QK_WIDTH // LANES):
            yc = yn[:, c * LANES:(c + 1) * LANES]
            partner = jnp.where(first_half, pltpu.roll(yc, LANES - ROPE_HALF, axis=1),
                                pltpu.roll(yc, ROPE_HALF, axis=1))
            chunks.append(yc * cos + partner * sin)
        yn = jnp.concatenate(chunks, axis=1)
        qT_ref[...] = yn[:, :ATTN_WIDTH].T.astype(BF16)
        f_ref[...] = z[:, QK_WIDTH + KV_WIDTH:].astype(BF16)
    for g in range(N_KV_HEADS):
        k_ref[g] = yn[:, ATTN_WIDTH + g * HEAD_DIM:ATTN_WIDTH + (g + 1) * HEAD_DIM].astype(BF16)
    vT = z[:, QK_WIDTH:QK_WIDTH + KV_WIDTH].T.astype(BF16)
    ones = jnp.ones((ONES_ROWS, vT.shape[1]), BF16)
    for g in range(N_KV_HEADS):
        vT_ref[g * V_ROWS:g * V_ROWS + HEAD_DIM, :] = vT[g * HEAD_DIM:(g + 1) * HEAD_DIM]
        vT_ref[g * V_ROWS + HEAD_DIM:(g + 1) * V_ROWS, :] = ones


def _evenproj(x2d, nw, mod, w_bf, gain, *, latent, tm):
    n = x2d.shape[0]
    red, expm = _head_matrices()
    const = lambda i: (0, 0)
    in_specs = [pl.BlockSpec((tm, D_MODEL), lambda i: (i, 0)),
                pl.BlockSpec((1, 4, D_MODEL), lambda i: (0, 0, 0)),
                pl.BlockSpec((1, 8, 6 * D_MODEL), lambda i: (0, 0, 0)),
                pl.BlockSpec((D_MODEL, EVEN_IN_WIDTH), const),
                pl.BlockSpec((1, QK_WIDTH), const),
                pl.BlockSpec((QK_WIDTH, LANES), const),
                pl.BlockSpec((LANES, QK_WIDTH), const)]
    args = [x2d, nw, mod, w_bf, gain, jnp.asarray(red), jnp.asarray(expm)]
    k_shape = jax.ShapeDtypeStruct((N_KV_HEADS, n, HEAD_DIM), BF16)
    vT_shape = jax.ShapeDtypeStruct((N_KV_HEADS * V_ROWS, n), BF16)
    k_spec = pl.BlockSpec((N_KV_HEADS, tm, HEAD_DIM), lambda i: (0, i, 0))
    vT_spec = pl.BlockSpec((N_KV_HEADS * V_ROWS, tm), lambda i: (0, i))
    if latent:
        cos, sin = _rope_tables(n)
        in_specs += [pl.BlockSpec((tm, LANES), lambda i: (i, 0))] * 2
        args += [jnp.asarray(cos), jnp.asarray(sin)]
        out_shape = (jax.ShapeDtypeStruct((ATTN_WIDTH, n), BF16), k_shape, vT_shape,
                     jax.ShapeDtypeStruct((n, FOURIER_WIDTH), BF16))
        out_specs = (pl.BlockSpec((ATTN_WIDTH, tm), lambda i: (0, i)), k_spec, vT_spec,
                     pl.BlockSpec((tm, FOURIER_WIDTH), lambda i: (i, 0)))
    else:
        out_shape = (k_shape, vT_shape)
        out_specs = (k_spec, vT_spec)
    return pl.pallas_call(
        functools.partial(_evenproj_kernel, row=0 if latent else 1, latent=latent),
        out_shape=out_shape, grid=(n // tm,), in_specs=in_specs, out_specs=out_specs,
        compiler_params=_cparams(("parallel",)),
        name="evenproj_lat" if latent else "evenproj_ctx",
    )(*args)


def _visit_all(visit, k_ref, vT_ref, kc_ref, vcT_ref, tk):
    def body(c, carry):
        off = pl.multiple_of(c * tk, tk)
        visit(k_ref[0, pl.ds(off, tk), :], vT_ref[:, pl.ds(off, tk)])
        return carry

    lax.fori_loop(0, k_ref.shape[1] // tk, body, 0)
    visit(kc_ref[0], vcT_ref[...])


def _attn_bounded_kernel(qT_ref, k_ref, vT_ref, kc_ref, vcT_ref, o_ref, acc_sc, s_sc, *, tk):
    n_tiles = k_ref.shape[1] // tk
    acc_sc[...] = jnp.zeros(acc_sc.shape, F32)

    def q(j):
        return qT_ref[j * HEAD_DIM:(j + 1) * HEAD_DIM, :]

    def keys(c):
        return k_ref[0, pl.ds(pl.multiple_of(c * tk, tk), tk), :]

    def consume(j, s, vt):
        acc_sc[j] += _dot(vt, jnp.exp2(s).astype(BF16))

    s_sc[...] = _dot(keys(0), q(0))

    def body(c, carry):
        kt = keys(c)
        vt = vT_ref[:, pl.ds(pl.multiple_of(c * tk, tk), tk)]
        s = s_sc[...]
        for j in range(Q_PER_KV):
            if j + 1 < Q_PER_KV:
                s_next = _dot(kt, q(j + 1))
            else:
                s_next = _dot(keys(jnp.minimum(c + 1, n_tiles - 1)), q(0))
            consume(j, s, vt)
            s = s_next
        s_sc[...] = s
        return carry

    lax.fori_loop(0, n_tiles, body, 0)
    kc, vc = kc_ref[0], vcT_ref[...]
    s = _dot(kc, q(0))
    for j in range(Q_PER_KV):
        s_next = _dot(kc, q(j + 1)) if j + 1 < Q_PER_KV else None
        consume(j, s, vc)
        s = s_next
    for j in range(Q_PER_KV):
        acc = acc_sc[j]
        o_ref[j * HEAD_DIM:(j + 1) * HEAD_DIM, :] = (acc[:HEAD_DIM] / acc[HEAD_DIM:HEAD_DIM + 1]).astype(BF16)


def _attn_online_kernel(qT_ref, k_ref, vT_ref, kc_ref, vcT_ref, o_ref, m_sc, acc_sc, *, tk):
    m_sc[...] = jnp.full(m_sc.shape, -jnp.inf, F32)
    acc_sc[...] = jnp.zeros(acc_sc.shape, F32)

    def visit(kt, vt):
        for j in range(Q_PER_KV):
            s = _dot(kt, qT_ref[j * HEAD_DIM:(j + 1) * HEAD_DIM, :])
            m_old = m_sc[j]
            m_new = jnp.maximum(m_old, jnp.max(s, axis=0, keepdims=True))
            p = jnp.exp2(s - m_new).astype(BF16)
            acc_sc[j] = jnp.exp2(m_old - m_new) * acc_sc[j] + _dot(vt, p)
            m_sc[j] = m_new

    _visit_all(visit, k_ref, vT_ref, kc_ref, vcT_ref, tk)
    for j in range(Q_PER_KV):
        acc = acc_sc[j]
        o_ref[j * HEAD_DIM:(j + 1) * HEAD_DIM, :] = (acc[:HEAD_DIM] / acc[HEAD_DIM:HEAD_DIM + 1]).astype(BF16)


def _attention(qT, k, vT, kc, vcT, score_bound, *, tq, tk):
    n = qT.shape[1]
    n_ctx = kc.shape[1]
    gw = Q_PER_KV * HEAD_DIM
    common = dict(
        out_shape=jax.ShapeDtypeStruct((ATTN_WIDTH, n), BF16),
        grid=(N_KV_HEADS, n // tq),
        in_specs=[pl.BlockSpec((gw, tq), lambda g, i: (g, i)),
                  pl.BlockSpec((1, n, HEAD_DIM), lambda g, i: (g, 0, 0)),
                  pl.BlockSpec((V_ROWS, n), lambda g, i: (g, 0)),
                  pl.BlockSpec((1, n_ctx, HEAD_DIM), lambda g, i: (g, 0, 0)),
                  pl.BlockSpec((V_ROWS, n_ctx), lambda g, i: (g, 0))],
        out_specs=pl.BlockSpec((gw, tq), lambda g, i: (g, i)),
        compiler_params=_cparams(("parallel", "parallel")),
    )
    acc = pltpu.VMEM((Q_PER_KV, V_ROWS, tq), F32)
    bounded = pl.pallas_call(functools.partial(_attn_bounded_kernel, tk=tk),
                             scratch_shapes=[acc, pltpu.VMEM((tk, tq), F32)], name="attn_bounded", **common)
    online = pl.pallas_call(functools.partial(_attn_online_kernel, tk=tk),
                            scratch_shapes=[pltpu.VMEM((Q_PER_KV, 1, tq), F32), acc],
                            name="attn_online", **common)
    return lax.cond(score_bound <= EXP2_SAFE_BOUND, bounded, online, qT, k, vT, kc, vcT)


def _four_a_kernel(f_ref, cs_ref, base_ref, twr_ref, twi_ref, y_ref, *, nb):
    n1 = f_ref.shape[0]
    cs = cs_ref[...].astype(BF16)
    base = base_ref[...].astype(BF16)
    for u in range(nb):
        xb = f_ref[:, u * FOURIER_WIDTH:(u + 1) * FOURIER_WIDTH]
        ab = _dot(xb, cs)
        stacked = jnp.concatenate([ab[:, :FOURIER_WIDTH], ab[:, FOURIER_WIDTH:]], axis=0)
        p = _dot(base, stacked.astype(BF16))
        pr, pi = p[:n1], p[n1:]
        tr = twr_ref[0, :, u:u + 1]
        ti = twi_ref[0, :, u:u + 1]
        y_ref[u, 0] = tr * pr - ti * pi
        y_ref[u, 1] = tr * pi + ti * pr


def _four_b_kernel(y_ref, wb_ref, o_ref):
    n2, _, kb, w = y_ref.shape
    y = y_ref[...].reshape(n2 * 2 * kb, w).astype(BF16)
    o_ref[...] = _dot(wb_ref[...].astype(BF16), y).reshape(n2, kb, w)


def _fourier(f):
    n = f.shape[0]
    n2 = LANES
    n1 = n // n2
    nb = kb = 8
    cs, base, twr, twi, wb = (jnp.asarray(t) for t in _fourier_tables(n, kb))
    f2d = f.reshape(n1, n2 * FOURIER_WIDTH)
    y = pl.pallas_call(
        functools.partial(_four_a_kernel, nb=nb),
        out_shape=jax.ShapeDtypeStruct((n2, 2, n1, FOURIER_WIDTH), F32),
        grid=(n2 // nb,),
        in_specs=[pl.BlockSpec((n1, nb * FOURIER_WIDTH), lambda s: (0, s)),
                  pl.BlockSpec(cs.shape, lambda s: (0, 0)),
                  pl.BlockSpec(base.shape, lambda s: (0, 0)),
                  pl.BlockSpec((1, n1, nb), lambda s: (s, 0, 0)),
                  pl.BlockSpec((1, n1, nb), lambda s: (s, 0, 0))],
        out_specs=pl.BlockSpec((nb, 2, n1, FOURIER_WIDTH), lambda s: (s, 0, 0, 0)),
        compiler_params=_cparams(("parallel",)),
        name="four_a",
    )(f2d, cs, base, twr, twi)
    out = pl.pallas_call(
        _four_b_kernel,
        out_shape=jax.ShapeDtypeStruct((n2, n1, FOURIER_WIDTH), F32),
        grid=(n1 // kb,),
        in_specs=[pl.BlockSpec((n2, 2, kb, FOURIER_WIDTH), lambda s: (0, 0, s, 0)),
                  pl.BlockSpec(wb.shape, lambda s: (0, 0))],
        out_specs=pl.BlockSpec((n2, kb, FOURIER_WIDTH), lambda s: (0, s, 0)),
        compiler_params=_cparams(("parallel",)),
        name="four_b",
    )(y, wb)
    return out.reshape(n, FOURIER_WIDTH)


def _outproj_kernel(aT_ref, four_ref, x_ref, wa_ref, wf_ref, nw_ref, mod_ref, x1_ref, h_ref):
    _, _, gt1, sh2, sc2, _ = _mod_rows(mod_ref, 0)
    y = lax.dot_general(aT_ref[...], wa_ref[...], (((0,), (0,)), ((), ())), preferred_element_type=F32)
    y = y + _dot(four_ref[...].astype(BF16), wf_ref[...])
    x1 = x_ref[...] + gt1 * _rms(y, nw_ref[0, 1:2, :])
    x1_ref[...] = x1
    h_ref[...] = (_rms(x1, nw_ref[0, 2:3, :]) * (1.0 + sc2) + sh2).astype(BF16)


def _outproj(attnT, four, x2d, wa, wf, nw, mod, *, tm):
    n = x2d.shape[0]
    const = lambda i: (0, 0)
    return pl.pallas_call(
        _outproj_kernel,
        out_shape=(jax.ShapeDtypeStruct((n, D_MODEL), F32), jax.ShapeDtypeStruct((n, D_MODEL), BF16)),
        grid=(n // tm,),
        in_specs=[pl.BlockSpec((ATTN_WIDTH, tm), lambda i: (0, i)),
                  pl.BlockSpec((tm, FOURIER_WIDTH), lambda i: (i, 0)),
                  pl.BlockSpec((tm, D_MODEL), lambda i: (i, 0)),
                  pl.BlockSpec((ATTN_WIDTH, D_MODEL), const),
                  pl.BlockSpec((FOURIER_WIDTH, D_MODEL), const),
                  pl.BlockSpec((1, 4, D_MODEL), lambda i: (0, 0, 0)),
                  pl.BlockSpec((1, 8, 6 * D_MODEL), lambda i: (0, 0, 0))],
        out_specs=(pl.BlockSpec((tm, D_MODEL), lambda i: (i, 0)),
                   pl.BlockSpec((tm, D_MODEL), lambda i: (i, 0))),
        compiler_params=_cparams(("parallel",)),
        name="outproj",
    )(attnT, four, x2d, wa, wf, nw, mod)


def _ffn_kernel(h_ref, x_ref, wg_ref, wu_ref, wd_ref, nw_ref, mod_ref, nw1_ref, mod1_ref,
                x2_ref, h3_ref, acc_ref):
    c = pl.program_id(1)
    h = h_ref[...]
    g = _dot(h, wg_ref[...])
    u = _dot(h, wu_ref[...])
    part = _dot((g * jax.nn.sigmoid(g) * u).astype(BF16), wd_ref[...])

    @pl.when(c == 0)
    def _():
        acc_ref[...] = part

    @pl.when(c > 0)
    def _():
        acc_ref[...] += part

    @pl.when(c == pl.num_programs(1) - 1)
    def _():
        gt2 = _mod_rows(mod_ref, 0)[5]
        sh, sc = _mod_rows(mod1_ref, 0)[:2]
        x2 = x_ref[...] + gt2 * _rms(acc_ref[...], nw_ref[0, 3:4, :])
        x2_ref[...] = x2
        h3_ref[...] = (_rms(x2, nw1_ref[0, 0:1, :]) * (1.0 + sc) + sh).astype(BF16)


def _ffn(h, x1, wg, wu, wd, nw, mod, nw1, mod1, *, tm, tf):
    n = h.shape[0]
    nwspec = pl.BlockSpec((1, 4, D_MODEL), lambda i, c: (0, 0, 0))
    modspec = pl.BlockSpec((1, 8, 6 * D_MODEL), lambda i, c: (0, 0, 0))
    return pl.pallas_call(
        _ffn_kernel,
        out_shape=(jax.ShapeDtypeStruct((n, D_MODEL), F32), jax.ShapeDtypeStruct((n, D_MODEL), BF16)),
        grid=(n // tm, D_FF // tf),
        in_specs=[pl.BlockSpec((tm, D_MODEL), lambda i, c: (i, 0)),
                  pl.BlockSpec((tm, D_MODEL), lambda i, c: (i, 0)),
                  pl.BlockSpec((D_MODEL, tf), lambda i, c: (0, c)),
                  pl.BlockSpec((D_MODEL, tf), lambda i, c: (0, c)),
                  pl.BlockSpec((tf, D_MODEL), lambda i, c: (c, 0)),
                  nwspec, modspec, nwspec, modspec],
        out_specs=(pl.BlockSpec((tm, D_MODEL), lambda i, c: (i, 0)),
                   pl.BlockSpec((tm, D_MODEL), lambda i, c: (i, 0))),
        scratch_shapes=[pltpu.VMEM((tm, D_MODEL), F32)],
        compiler_params=_cparams(("parallel", "arbitrary")),
        name="ffn",
    )(h, x1, wg, wu, wd, nw, mod, nw1, mod1)


EDGE_ROWS = 16


def _convedge_kernel(hf_ref, hl_ref, wc_ref, wu_ref, vf_ref, vl_ref):
    nt = hf_ref.shape[0]
    for h_ref, v_ref in ((hf_ref, vf_ref), (hl_ref, vl_ref)):
        h = h_ref[...].reshape(nt * EDGE_ROWS, D_MODEL)
        v = _dot(h, wc_ref[...]) * _dot(h, wu_ref[...])
        v_ref[...] = v.reshape(nt, EDGE_ROWS, D_MODEL)


def _convedge(h3, w_in_bf, *, tm):
    n = h3.shape[0]
    nt = n // tm
    h3t = h3.reshape(nt, tm, D_MODEL)
    last = tm // EDGE_ROWS - 1
    shape = jax.ShapeDtypeStruct((nt, EDGE_ROWS, D_MODEL), F32)
    return pl.pallas_call(
        _convedge_kernel,
        out_shape=(shape, shape),
        grid=(1,),
        in_specs=[pl.BlockSpec((nt, EDGE_ROWS, D_MODEL), lambda i: (0, 0, 0)),
                  pl.BlockSpec((nt, EDGE_ROWS, D_MODEL), lambda i: (0, last, 0)),
                  pl.BlockSpec((D_MODEL, D_MODEL), lambda i: (0, 1)),
                  pl.BlockSpec((D_MODEL, D_MODEL), lambda i: (0, 2))],
        out_specs=(pl.BlockSpec((nt, EDGE_ROWS, D_MODEL), lambda i: (0, 0, 0)),
                   pl.BlockSpec((nt, EDGE_ROWS, D_MODEL), lambda i: (0, 0, 0))),
        compiler_params=_cparams(("arbitrary",)),
        name="convedge",
    )(h3t, h3t, w_in_bf, w_in_bf)


def _convmix_kernel(h_ref, x_ref, vl_ref, vf_ref, win_ref, cw_ref, wout_ref, nw_ref, mod_ref, r_ref,
                    x3_ref, h4_ref, comb_ref):
    i = pl.program_id(0)
    tm = h_ref.shape[0]
    _, _, gt1, sh2, sc2, _ = _mod_rows(mod_ref, 0)
    z = _dot(h_ref[...], win_ref[...])
    b = z[:, :D_MODEL]
    v = z[:, D_MODEL:2 * D_MODEL] * z[:, 2 * D_MODEL:]
    has_prev = (i > 0).astype(F32)
    has_next = (i < pl.num_programs(0) - 1).astype(F32)
    prev_row = vl_ref[0, EDGE_ROWS - 1:EDGE_ROWS, :] * has_prev
    next_row = vf_ref[0, 0:1, :] * has_next
    rows = lax.broadcasted_iota(jnp.int32, (tm, 1), 0)
    v_dn = jnp.where(rows == 0, prev_row, pltpu.roll(v, 1, axis=0))
    v_up = jnp.where(rows == tm - 1, next_row, pltpu.roll(v, tm - 1, axis=0))
    conv = v_dn * cw_ref[0, 0:1, :] + v * cw_ref[0, 1:2, :] + v_up * cw_ref[0, 2:3, :]
    y = _dot((b * conv).astype(BF16), wout_ref[...])
    x3 = x_ref[...] + gt1 * _rms(y, nw_ref[0, 1:2, :])
    x3_ref[...] = x3
    h4 = _rms(x3, nw_ref[0, 2:3, :]) * (1.0 + sc2) + sh2
    h4_ref[...] = h4.astype(BF16)
    hhi, hlo = _split_bf16(h4)
    rhi, rlo = _split_bf16(r_ref[...])
    logits = _dot(hhi, rhi) + (_dot(hlo, rhi) + _dot(hhi, rlo))
    lane = lax.broadcasted_iota(jnp.int32, logits.shape, 1)
    logits = jnp.where(lane < N_EXPERTS, logits, -jnp.inf)
    e = jnp.exp(logits - jnp.max(logits, axis=-1, keepdims=True))
    probs = e / jnp.sum(e, axis=-1, keepdims=True)
    v1 = jnp.max(probs, axis=-1, keepdims=True)
    i1 = jnp.min(jnp.where(probs == v1, lane, LANES), axis=-1, keepdims=True)
    rest = jnp.where(lane == i1, -1.0, probs)
    v2 = jnp.max(rest, axis=-1, keepdims=True)
    i2 = jnp.min(jnp.where(rest == v2, lane, LANES), axis=-1, keepdims=True)
    tot = v1 + v2
    comb_ref[...] = jnp.where(lane == i1, v1 / tot, 0.0) + jnp.where(lane == i2, v2 / tot, 0.0)


def _convmix(h3, x2, v_first, v_last, w_in_bf, conv_w, w_out_bf, nw, mod, router_pad, *, tm):
    n = h3.shape[0]
    nt = n // tm
    const = lambda i: (0, 0)
    return pl.pallas_call(
        _convmix_kernel,
        out_shape=(jax.ShapeDtypeStruct((n, D_MODEL), F32), jax.ShapeDtypeStruct((n, D_MODEL), BF16),
                   jax.ShapeDtypeStruct((n, LANES), F32)),
        grid=(nt,),
        in_specs=[pl.BlockSpec((tm, D_MODEL), lambda i: (i, 0)),
                  pl.BlockSpec((tm, D_MODEL), lambda i: (i, 0)),
                  pl.BlockSpec((1, EDGE_ROWS, D_MODEL), lambda i: (jnp.maximum(i - 1, 0), 0, 0)),
                  pl.BlockSpec((1, EDGE_ROWS, D_MODEL), lambda i: (jnp.minimum(i + 1, nt - 1), 0, 0)),
                  pl.BlockSpec((D_MODEL, 3 * D_MODEL), const),
                  pl.BlockSpec((1, 3, D_MODEL), lambda i: (0, 0, 0)),
                  pl.BlockSpec((D_MODEL, D_MODEL), const),
                  pl.BlockSpec((1, 4, D_MODEL), lambda i: (0, 0, 0)),
                  pl.BlockSpec((1, 8, 6 * D_MODEL), lambda i: (0, 0, 0)),
                  pl.BlockSpec((D_MODEL, LANES), const)],
        out_specs=(pl.BlockSpec((tm, D_MODEL), lambda i: (i, 0)),
                   pl.BlockSpec((tm, D_MODEL), lambda i: (i, 0)),
                   pl.BlockSpec((tm, LANES), lambda i: (i, 0))),
        compiler_params=_cparams(("parallel",)),
        name="convmix",
    )(h3, x2, v_last, v_first, w_in_bf, conv_w, w_out_bf, nw, mod, router_pad)


def _moe_kernel(h_ref, comb_ref, x_ref, wg_ref, wu_ref, wd_ref, nw_ref, mod_ref, o_ref, acc_ref):
    e = pl.program_id(1)
    c = pl.program_id(2)
    first = jnp.logical_and(e == 0, c == 0)
    last = jnp.logical_and(e == pl.num_programs(1) - 1, c == pl.num_programs(2) - 1)
    h = h_ref[...]
    comb = comb_ref[...]
    lane = lax.broadcasted_iota(jnp.int32, comb.shape, 1)
    ce = jnp.sum(jnp.where(lane == e, comb, 0.0), axis=-1, keepdims=True)
    g = _dot(h, wg_ref[0, 0].astype(BF16))
    u = _dot(h, wu_ref[0, 0].astype(BF16))
    part = _dot((g * jax.nn.sigmoid(g) * u * ce).astype(BF16), wd_ref[0, 0].astype(BF16))

    @pl.when(first)
    def _():
        acc_ref[...] = part

    @pl.when(jnp.logical_not(first))
    def _():
        acc_ref[...] += part

    @pl.when(last)
    def _():
        gt2 = _mod_rows(mod_ref, 0)[5]
        o_ref[...] = x_ref[...] + gt2 * _rms(acc_ref[...], nw_ref[0, 3:4, :])


def _moe(h4, comb, x3, wg, wu, wd, nw, mod, *, tm, tf):
    n = h4.shape[0]
    return pl.pallas_call(
        _moe_kernel,
        out_shape=jax.ShapeDtypeStruct((n, D_MODEL), F32),
        grid=(n // tm, N_EXPERTS, D_EXPERT // tf),
        in_specs=[pl.BlockSpec((tm, D_MODEL), lambda i, e, c: (i, 0)),
                  pl.BlockSpec((tm, LANES), lambda i, e, c: (i, 0)),
                  pl.BlockSpec((tm, D_MODEL), lambda i, e, c: (i, 0)),
                  pl.BlockSpec((1, 1, D_MODEL, tf), lambda i, e, c: (0, e, 0, c)),
                  pl.BlockSpec((1, 1, D_MODEL, tf), lambda i, e, c: (0, e, 0, c)),
                  pl.BlockSpec((1, 1, tf, D_MODEL), lambda i, e, c: (0, e, c, 0)),
                  pl.BlockSpec((1, 4, D_MODEL), lambda i, e, c: (0, 0, 0)),
                  pl.BlockSpec((1, 8, 6 * D_MODEL), lambda i, e, c: (0, 0, 0))],
        out_specs=pl.BlockSpec((tm, D_MODEL), lambda i, e, c: (i, 0)),
        scratch_shapes=[pltpu.VMEM((tm, D_MODEL), F32)],
        compiler_params=_cparams(("parallel", "arbitrary", "arbitrary")),
        name="moe",
    )(h4, comb, x3, wg, wu, wd, nw, mod)


def kernel(x, c, ctx, c_ctx, ada_w, ada_b, norm_w, e_w_in, e_q_gain, e_k_gain, e_w_out, e_ffn_gate,
           e_ffn_up, e_ffn_down, o_w_in, o_conv_w, o_w_out, o_router, o_exp_gate, o_exp_up, o_exp_down):
    assert x.shape[0] == 1 and x.shape[2] == D_MODEL and ada_w.shape[0] == 2
    n = x.shape[1]
    x2d = x[0]
    ctx2d = ctx[0]
    mod = _ada(c, c_ctx, ada_w, ada_b)
    mod0, mod1 = mod[0:1], mod[1:2]
    nw0, nw1 = norm_w[0:1], norm_w[1:2]

    w_in = e_w_in[0].astype(BF16)
    scale = HEAD_DIM ** -0.5 * np.log2(np.e)
    gain = jnp.concatenate([jnp.tile(e_q_gain[0], N_Q_HEADS) * scale,
                            jnp.tile(e_k_gain[0], N_KV_HEADS)])[None, :]
    score_bound = 1.02 * HEAD_DIM * scale * jnp.max(jnp.abs(e_q_gain[0])) * jnp.max(jnp.abs(e_k_gain[0]))
    qT, k, vT, f = _evenproj(x2d, nw0, mod0, w_in, gain, latent=True, tm=512)
    kc, vcT = _evenproj(ctx2d, nw0, mod0, w_in, gain, latent=False, tm=ctx2d.shape[0])
    attnT = _attention(qT, k, vT, kc, vcT, score_bound, tq=512, tk=512)
    four = _fourier(f)
    w_out = e_w_out[0].astype(BF16)
    x1, h2 = _outproj(attnT, four, x2d, w_out[:ATTN_WIDTH], w_out[ATTN_WIDTH:], nw0, mod0, tm=512)
    x2, h3 = _ffn(h2, x1, e_ffn_gate[0].astype(BF16), e_ffn_up[0].astype(BF16),
                  e_ffn_down[0].astype(BF16), nw0, mod0, nw1, mod1, tm=1024, tf=1408)

    conv_tm = 512
    ow_in = o_w_in[0].astype(BF16)
    v_first, v_last = _convedge(h3, ow_in, tm=conv_tm)
    router_pad = jnp.pad(o_router[0], ((0, 0), (0, LANES - N_EXPERTS)))
    x3, h4, comb = _convmix(h3, x2, v_first, v_last, ow_in, o_conv_w, o_w_out[0].astype(BF16),
                            nw1, mod1, router_pad, tm=conv_tm)
    out = _moe(h4, comb, x3, o_exp_gate, o_exp_up, o_exp_down, nw1, mod1, tm=1024, tf=512)
    return out[None]
```

```python
import functools

import numpy as np
import jax
import jax.numpy as jnp
from jax import lax
from jax.experimental import pallas as pl
from jax.experimental.pallas import tpu as pltpu

D_MODEL = 1024
GRID_W = 64
HEAD_DIM = 64
N_Q_HEADS = 12
N_KV_HEADS = 4
Q_PER_KV = N_Q_HEADS // N_KV_HEADS
ATTN_WIDTH = N_Q_HEADS * HEAD_DIM
KV_WIDTH = N_KV_HEADS * HEAD_DIM
QK_WIDTH = ATTN_WIDTH + KV_WIDTH
N_FOURIER_GROUPS = 4
FOURIER_GROUP_DIM = 64
FOURIER_WIDTH = N_FOURIER_GROUPS * FOURIER_GROUP_DIM
EVEN_IN_WIDTH = ATTN_WIDTH + 2 * KV_WIDTH + FOURIER_WIDTH
D_FF = 2816
N_EXPERTS = 8
D_EXPERT = 3584
ONES_ROWS = 16
V_ROWS = HEAD_DIM + ONES_ROWS
EXP2_SAFE_BOUND = 60.0
ROPE_THETA = 10000.0
ROPE_HALF = HEAD_DIM // 4
NORM_EPS = 1e-6

LANES = 128
VMEM_LIMIT = 56 * 1024 * 1024

BF16 = jnp.bfloat16
F32 = jnp.float32


def _cparams(semantics, vmem=VMEM_LIMIT):
    return pltpu.CompilerParams(dimension_semantics=semantics, vmem_limit_bytes=vmem)


def _dot(a, b):
    return jnp.dot(a, b, preferred_element_type=F32)


def _split_bf16(x):
    hi = x.astype(BF16)
    lo = (x - hi.astype(F32)).astype(BF16)
    return hi, lo


def _rms(x, g):
    return x * lax.rsqrt(jnp.mean(x * x, axis=-1, keepdims=True) + NORM_EPS) * g


def _mod_rows(mod_ref, row):
    return [mod_ref[0, row:row + 1, i * D_MODEL:(i + 1) * D_MODEL] for i in range(6)]


@functools.lru_cache(maxsize=None)
def _rope_tables(n_tokens):
    t = np.arange(n_tokens)
    row = (t // GRID_W).astype(np.float64)
    col = (t % GRID_W).astype(np.float64)
    inv = ROPE_THETA ** (-np.arange(ROPE_HALF, dtype=np.float64) / ROPE_HALF)
    ar, ac = row[:, None] * inv, col[:, None] * inv
    cos = np.concatenate([np.cos(ar), np.cos(ar), np.cos(ac), np.cos(ac)], axis=-1)
    sin = np.concatenate([-np.sin(ar), np.sin(ar), -np.sin(ac), np.sin(ac)], axis=-1)
    reps = LANES // HEAD_DIM
    return (np.tile(cos, (1, reps)).astype(np.float32), np.tile(sin, (1, reps)).astype(np.float32))


@functools.lru_cache(maxsize=None)
def _head_matrices():
    head = np.arange(QK_WIDTH) // HEAD_DIM
    red = (head[:, None] == np.arange(LANES)[None, :]).astype(np.float32) / HEAD_DIM
    exp = (np.arange(LANES)[:, None] == head[None, :]).astype(np.float32)
    return red, exp


@functools.lru_cache(maxsize=None)
def _fourier_tables(n_tokens, kb):
    n2 = LANES
    n1 = n_tokens // n2
    c = np.arange(FOURIER_GROUP_DIM)
    ang = 2 * np.pi * np.outer(c, c) / FOURIER_GROUP_DIM
    eye = np.eye(N_FOURIER_GROUPS)
    cs = np.concatenate([np.kron(eye, np.cos(ang)), np.kron(eye, np.sin(ang))], axis=1)
    k1 = np.arange(n1)
    th = 2 * np.pi * np.outer(k1, k1) / n1
    cr, ci = np.cos(th), -np.sin(th)
    base = np.block([[cr, ci], [ci, -cr]])
    psi = 2 * np.pi * np.outer(k1, np.arange(n2)) / n_tokens
    twr = np.cos(psi).reshape(n1, n2 // kb, kb).transpose(1, 0, 2)
    twi = (-np.sin(psi)).reshape(n1, n2 // kb, kb).transpose(1, 0, 2)
    k2 = np.arange(n2)
    ph = 2 * np.pi * np.outer(k2, k2) / n2
    fr, fi = np.cos(ph), -np.sin(ph)
    scale = 1.0 / np.sqrt(n_tokens * FOURIER_GROUP_DIM)
    m3 = np.stack([fr, -fi], axis=-1) * scale
    wb = np.einsum('knr,uv->kunrv', m3, np.eye(kb)).reshape(n2 * kb, n2 * 2 * kb)
    f32 = np.float32
    return cs.astype(f32), base.astype(f32), twr.astype(f32), twi.astype(f32), wb.astype(f32)


def _ada_kernel(cb_ref, w_ref, b_ref, o_ref):
    tn = o_ref.shape[-1]
    o_ref[...] = jnp.zeros(o_ref.shape, F32)
    for r in range(2):
        cb = cb_ref[r]
        s = cb * jax.nn.sigmoid(cb)
        for j in range(tn // LANES):
            sl = slice(j * LANES, (j + 1) * LANES)
            col = jnp.sum(s * w_ref[0, :, sl], axis=0, keepdims=True)
            o_ref[0, r:r + 1, sl] = col + b_ref[0, :, sl]


def _ada(c, c_ctx, ada_w, ada_b):
    depth = ada_w.shape[0]
    n = ada_w.shape[-1]
    tn = 1536
    cb = jnp.stack([jnp.broadcast_to(c[0][:, None], (D_MODEL, LANES)),
                    jnp.broadcast_to(c_ctx[:, None], (D_MODEL, LANES))])
    return pl.pallas_call(
        _ada_kernel,
        out_shape=jax.ShapeDtypeStruct((depth, 8, n), F32),
        grid=(depth, n // tn),
        in_specs=[pl.BlockSpec((2, D_MODEL, LANES), lambda i, j: (0, 0, 0)),
                  pl.BlockSpec((1, D_MODEL, tn), lambda i, j: (i, 0, j)),
                  pl.BlockSpec((1, 1, tn), lambda i, j: (i, 0, j))],
        out_specs=pl.BlockSpec((1, 8, tn), lambda i, j: (i, 0, j)),
        compiler_params=_cparams(("parallel", "parallel")),
        name="ada",
    )(cb, ada_w, ada_b[:, None, :])


def _evenproj_kernel(*refs, row, latent):
    if latent:
        (x_ref, nw_ref, mod_ref, w_ref, gain_ref, red_ref, exp_ref, cos_ref, sin_ref,
         qT_ref, k_ref, vT_ref, f_ref) = refs
    else:
        x_ref, nw_ref, mod_ref, w_ref, gain_ref, red_ref, exp_ref, k_ref, vT_ref = refs
    sh, sc = _mod_rows(mod_ref, row)[:2]
    h = (_rms(x_ref[...], nw_ref[0, 0:1, :]) * (1.0 + sc) + sh).astype(BF16)
    z = _dot(h, w_ref[...])
    zqk = z[:, :QK_WIDTH]
    hi, lo = _split_bf16(zqk * zqk)
    red = red_ref[...].astype(BF16)
    ms = _dot(hi, red) + _dot(lo, red)
    rhi, rlo = _split_bf16(lax.rsqrt(ms + NORM_EPS))
    expm = exp_ref[...].astype(BF16)
    yn = zqk * (_dot(rhi, expm) + _dot(rlo, expm)) * gain_ref[...]
    if latent:
        lane = lax.broadcasted_iota(jnp.int32, (1, LANES), 1)
        first_half = (lane // ROPE_HALF) % 2 == 0
        cos, sin = cos_ref[...], sin_ref[...]
        chunks = []
        for c in range(QK_WIDTH // LANES):
            yc = yn[:, c * LANES:(c + 1) * LANES]
            partner = jnp.where(first_half, pltpu.roll(yc, LANES - ROPE_HALF, axis=1),
                                pltpu.roll(yc, ROPE_HALF, axis=1))
            chunks.append(yc * cos + partner * sin)
        yn = jnp.concatenate(chunks, axis=1)
        qT_ref[...] = yn[:, :ATTN_WIDTH].T.astype(BF16)
        f_ref[...] = z[:, QK_WIDTH + KV_WIDTH:].astype(BF16)
    for g in range(N_KV_HEADS):
        k_ref[g] = yn[:, ATTN_WIDTH + g * HEAD_DIM:ATTN_WIDTH + (g + 1) * HEAD_DIM].astype(BF16)
    vT = z[:, QK_WIDTH:QK_WIDTH + KV_WIDTH].T.astype(BF16)
    ones = jnp.ones((ONES_ROWS, vT.shape[1]), BF16)
    for g in range(N_KV_HEADS):
        vT_ref[g * V_ROWS:g * V_ROWS + HEAD_DIM, :] = vT[g * HEAD_DIM:(g + 1) * HEAD_DIM]
        vT_ref[g * V_ROWS + HEAD_DIM:(g + 1) * V_ROWS, :] = ones


def _evenproj(x2d, nw, mod, w_bf, gain, *, latent, tm):
    n = x2d.shape[0]
    red, expm = _head_matrices()
    const = lambda i: (0, 0)
    in_specs = [pl.BlockSpec((tm, D_MODEL), lambda i: (i, 0)),
                pl.BlockSpec((1, 4, D_MODEL), lambda i: (0, 0, 0)),
                pl.BlockSpec((1, 8, 6 * D_MODEL), lambda i: (0, 0, 0)),
                pl.BlockSpec((D_MODEL, EVEN_IN_WIDTH), const),
                pl.BlockSpec((1, QK_WIDTH), const),
                pl.BlockSpec((QK_WIDTH, LANES), const),
                pl.BlockSpec((LANES, QK_WIDTH), const)]
    args = [x2d, nw, mod, w_bf, gain, jnp.asarray(red), jnp.asarray(expm)]
    k_shape = jax.ShapeDtypeStruct((N_KV_HEADS, n, HEAD_DIM), BF16)
    vT_shape = jax.ShapeDtypeStruct((N_KV_HEADS * V_ROWS, n), BF16)
    k_spec = pl.BlockSpec((N_KV_HEADS, tm, HEAD_DIM), lambda i: (0, i, 0))
    vT_spec = pl.BlockSpec((N_KV_HEADS * V_ROWS, tm), lambda i: (0, i))
    if latent:
        cos, sin = _rope_tables(n)
        in_specs += [pl.BlockSpec((tm, LANES), lambda i: (i, 0))] * 2
        args += [jnp.asarray(cos), jnp.asarray(sin)]
        out_shape = (jax.ShapeDtypeStruct((ATTN_WIDTH, n), BF16), k_shape, vT_shape,
                     jax.ShapeDtypeStruct((n, FOURIER_WIDTH), BF16))
        out_specs = (pl.BlockSpec((ATTN_WIDTH, tm), lambda i: (0, i)), k_spec, vT_spec,
                     pl.BlockSpec((tm, FOURIER_WIDTH), lambda i: (i, 0)))
    else:
        out_shape = (k_shape, vT_shape)
        out_specs = (k_spec, vT_spec)
    return pl.pallas_call(
        functools.partial(_evenproj_kernel, row=0 if latent else 1, latent=latent),
        out_shape=out_shape, grid=(n // tm,), in_specs=in_specs, out_specs=out_specs,
        compiler_params=_cparams(("parallel",)),
        name="evenproj_lat" if latent else "evenproj_ctx",
    )(*args)


def _visit_all(visit, k_ref, vT_ref, kc_ref, vcT_ref, tk):
    def body(c, carry):
        off = pl.multiple_of(c * tk, tk)
        visit(k_ref[0, pl.ds(off, tk), :], vT_ref[:, pl.ds(off, tk)])
        return carry

    lax.fori_loop(0, k_ref.shape[1] // tk, body, 0)
    visit(kc_ref[0], vcT_ref[...])


def _attn_bounded_kernel(qT_ref, k_ref, vT_ref, kc_ref, vcT_ref, o_ref, acc_sc, s_sc, *, tk):
    n_tiles = k_ref.shape[1] // tk
    acc_sc[...] = jnp.zeros(acc_sc.shape, F32)

    def q(j):
        return qT_ref[j * HEAD_DIM:(j + 1) * HEAD_DIM, :]

    def keys(c):
        return k_ref[0, pl.ds(pl.multiple_of(c * tk, tk), tk), :]

    def consume(j, s, vt):
        acc_sc[j] += _dot(vt, jnp.exp2(s).astype(BF16))

    s_sc[...] = _dot(keys(0), q(0))

    def body(c, carry):
        kt = keys(c)
        vt = vT_ref[:, pl.ds(pl.multiple_of(c * tk, tk), tk)]
        s = s_sc[...]
        for j in range(Q_PER_KV):
            if j + 1 < Q_PER_KV:
                s_next = _dot(kt, q(j + 1))
            else:
                s_next = _dot(keys(jnp.minimum(c + 1, n_tiles - 1)), q(0))
            consume(j, s, vt)
            s = s_next
        s_sc[...] = s
        return carry

    lax.fori_loop(0, n_tiles, body, 0)
    kc, vc = kc_ref[0], vcT_ref[...]
    s = _dot(kc, q(0))
    for j in range(Q_PER_KV):
        s_next = _dot(kc, q(j + 1)) if j + 1 < Q_PER_KV else None
        consume(j, s, vc)
        s = s_next
    for j in range(Q_PER_KV):
        acc = acc_sc[j]
        o_ref[j * HEAD_DIM:(j + 1) * HEAD_DIM, :] = (acc[:HEAD_DIM] / acc[HEAD_DIM:HEAD_DIM + 1]).astype(BF16)


def _attn_online_kernel(qT_ref, k_ref, vT_ref, kc_ref, vcT_ref, o_ref, m_sc, acc_sc, *, tk):
    m_sc[...] = jnp.full(m_sc.shape, -jnp.inf, F32)
    acc_sc[...] = jnp.zeros(acc_sc.shape, F32)

    def visit(kt, vt):
        for j in range(Q_PER_KV):
            s = _dot(kt, qT_ref[j * HEAD_DIM:(j + 1) * HEAD_DIM, :])
            m_old = m_sc[j]
            m_new = jnp.maximum(m_old, jnp.max(s, axis=0, keepdims=True))
            p = jnp.exp2(s - m_new).astype(BF16)
            acc_sc[j] = jnp.exp2(m_old - m_new) * acc_sc[j] + _dot(vt, p)
            m_sc[j] = m_new

    _visit_all(visit, k_ref, vT_ref, kc_ref, vcT_ref, tk)
    for j in range(Q_PER_KV):
        acc = acc_sc[j]
        o_ref[j * HEAD_DIM:(j + 1) * HEAD_DIM, :] = (acc[:HEAD_DIM] / acc[HEAD_DIM:HEAD_DIM + 1]).astype(BF16)


def _attention(qT, k, vT, kc, vcT, score_bound, *, tq, tk):
    n = qT.shape[1]
    n_ctx = kc.shape[1]
    gw = Q_PER_KV * HEAD_DIM
    common = dict(
        out_shape=jax.ShapeDtypeStruct((ATTN_WIDTH, n), BF16),
        grid=(N_KV_HEADS, n // tq),
        in_specs=[pl.BlockSpec((gw, tq), lambda g, i: (g, i)),
                  pl.BlockSpec((1, n, HEAD_DIM), lambda g, i: (g, 0, 0)),
                  pl.BlockSpec((V_ROWS, n), lambda g, i: (g, 0)),
                  pl.BlockSpec((1, n_ctx, HEAD_DIM), lambda g, i: (g, 0, 0)),
                  pl.BlockSpec((V_ROWS, n_ctx), lambda g, i: (g, 0))],
        out_specs=pl.BlockSpec((gw, tq), lambda g, i: (g, i)),
        compiler_params=_cparams(("parallel", "parallel")),
    )
    acc = pltpu.VMEM((Q_PER_KV, V_ROWS, tq), F32)
    bounded = pl.pallas_call(functools.partial(_attn_bounded_kernel, tk=tk),
                             scratch_shapes=[acc, pltpu.VMEM((tk, tq), F32)], name="attn_bounded", **common)
    online = pl.pallas_call(functools.partial(_attn_online_kernel, tk=tk),
                            scratch_shapes=[pltpu.VMEM((Q_PER_KV, 1, tq), F32), acc],
                            name="attn_online", **common)
    return lax.cond(score_bound <= EXP2_SAFE_BOUND, bounded, online, qT, k, vT, kc, vcT)


def _four_a_kernel(f_ref, cs_ref, base_ref, twr_ref, twi_ref, y_ref, *, nb):
    n1 = f_ref.shape[0]
    cs = cs_ref[...].astype(BF16)
    base = base_ref[...].astype(BF16)
    for u in range(nb):
        xb = f_ref[:, u * FOURIER_WIDTH:(u + 1) * FOURIER_WIDTH]
        ab = _dot(xb, cs)
        stacked = jnp.concatenate([ab[:, :FOURIER_WIDTH], ab[:, FOURIER_WIDTH:]], axis=0)
        p = _dot(base, stacked.astype(BF16))
        pr, pi = p[:n1], p[n1:]
        tr = twr_ref[0, :, u:u + 1]
        ti = twi_ref[0, :, u:u + 1]
        y_ref[u, 0] = tr * pr - ti * pi
        y_ref[u, 1] = tr * pi + ti * pr


def _four_b_kernel(y_ref, wb_ref, o_ref):
    n2, _, kb, w = y_ref.shape
    y = y_ref[...].reshape(n2 * 2 * kb, w).astype(BF16)
    o_ref[...] = _dot(wb_ref[...].astype(BF16), y).reshape(n2, kb, w)


def _fourier(f):
    n = f.shape[0]
    n2 = LANES
    n1 = n // n2
    nb = kb = 8
    cs, base, twr, twi, wb = (jnp.asarray(t) for t in _fourier_tables(n, kb))
    f2d = f.reshape(n1, n2 * FOURIER_WIDTH)
    y = pl.pallas_call(
        functools.partial(_four_a_kernel, nb=nb),
        out_shape=jax.ShapeDtypeStruct((n2, 2, n1, FOURIER_WIDTH), F32),
        grid=(n2 // nb,),
        in_specs=[pl.BlockSpec((n1, nb * FOURIER_WIDTH), lambda s: (0, s)),
                  pl.BlockSpec(cs.shape, lambda s: (0, 0)),
                  pl.BlockSpec(base.shape, lambda s: (0, 0)),
                  pl.BlockSpec((1, n1, nb), lambda s: (s, 0, 0)),
                  pl.BlockSpec((1, n1, nb), lambda s: (s, 0, 0))],
        out_specs=pl.BlockSpec((nb, 2, n1, FOURIER_WIDTH), lambda s: (s, 0, 0, 0)),
        compiler_params=_cparams(("parallel",)),
        name="four_a",
    )(f2d, cs, base, twr, twi)
    out = pl.pallas_call(
        _four_b_kernel,
        out_shape=jax.ShapeDtypeStruct((n2, n1, FOURIER_WIDTH), F32),
        grid=(n1 // kb,),
        in_specs=[pl.BlockSpec((n2, 2, kb, FOURIER_WIDTH), lambda s: (0, 0, s, 0)),
                  pl.BlockSpec(wb.shape, lambda s: (0, 0))],
        out_specs=pl.BlockSpec((n2, kb, FOURIER_WIDTH), lambda s: (0, s, 0)),
        compiler_params=_cparams(("parallel",)),
        name="four_b",
    )(y, wb)
    return out.reshape(n, FOURIER_WIDTH)


def _outproj_kernel(aT_ref, four_ref, x_ref, wa_ref, wf_ref, nw_ref, mod_ref, x1_ref, h_ref):
    _, _, gt1, sh2, sc2, _ = _mod_rows(mod_ref, 0)
    y = lax.dot_general(aT_ref[...], wa_ref[...], (((0,), (0,)), ((), ())), preferred_element_type=F32)
    y = y + _dot(four_ref[...].astype(BF16), wf_ref[...])
    x1 = x_ref[...] + gt1 * _rms(y, nw_ref[0, 1:2, :])
    x1_ref[...] = x1
    h_ref[...] = (_rms(x1, nw_ref[0, 2:3, :]) * (1.0 + sc2) + sh2).astype(BF16)


def _outproj(attnT, four, x2d, wa, wf, nw, mod, *, tm):
    n = x2d.shape[0]
    const = lambda i: (0, 0)
    return pl.pallas_call(
        _outproj_kernel,
        out_shape=(jax.ShapeDtypeStruct((n, D_MODEL), F32), jax.ShapeDtypeStruct((n, D_MODEL), BF16)),
        grid=(n // tm,),
        in_specs=[pl.BlockSpec((ATTN_WIDTH, tm), lambda i: (0, i)),
                  pl.BlockSpec((tm, FOURIER_WIDTH), lambda i: (i, 0)),
                  pl.BlockSpec((tm, D_MODEL), lambda i: (i, 0)),
                  pl.BlockSpec((ATTN_WIDTH, D_MODEL), const),
                  pl.BlockSpec((FOURIER_WIDTH, D_MODEL), const),
                  pl.BlockSpec((1, 4, D_MODEL), lambda i: (0, 0, 0)),
                  pl.BlockSpec((1, 8, 6 * D_MODEL), lambda i: (0, 0, 0))],
        out_specs=(pl.BlockSpec((tm, D_MODEL), lambda i: (i, 0)),
                   pl.BlockSpec((tm, D_MODEL), lambda i: (i, 0))),
        compiler_params=_cparams(("parallel",)),
        name="outproj",
    )(attnT, four, x2d, wa, wf, nw, mod)


def _ffn_kernel(h_ref, x_ref, wg_ref, wu_ref, wd_ref, nw_ref, mod_ref, nw1_ref, mod1_ref,
                x2_ref, h3_ref, acc_ref):
    c = pl.program_id(1)
    h = h_ref[...]
    g = _dot(h, wg_ref[...])
    u = _dot(h, wu_ref[...])
    part = _dot((g * jax.nn.sigmoid(g) * u).astype(BF16), wd_ref[...])

    @pl.when(c == 0)
    def _():
        acc_ref[...] = part

    @pl.when(c > 0)
    def _():
        acc_ref[...] += part

    @pl.when(c == pl.num_programs(1) - 1)
    def _():
        gt2 = _mod_rows(mod_ref, 0)[5]
        sh, sc = _mod_rows(mod1_ref, 0)[:2]
        x2 = x_ref[...] + gt2 * _rms(acc_ref[...], nw_ref[0, 3:4, :])
        x2_ref[...] = x2
        h3_ref[...] = (_rms(x2, nw1_ref[0, 0:1, :]) * (1.0 + sc) + sh).astype(BF16)


def _ffn(h, x1, wg, wu, wd, nw, mod, nw1, mod1, *, tm, tf):
    n = h.shape[0]
    nwspec = pl.BlockSpec((1, 4, D_MODEL), lambda i, c: (0, 0, 0))
    modspec = pl.BlockSpec((1, 8, 6 * D_MODEL), lambda i, c: (0, 0, 0))
    return pl.pallas_call(
        _ffn_kernel,
        out_shape=(jax.ShapeDtypeStruct((n, D_MODEL), F32), jax.ShapeDtypeStruct((n, D_MODEL), BF16)),
        grid=(n // tm, D_FF // tf),
        in_specs=[pl.BlockSpec((tm, D_MODEL), lambda i, c: (i, 0)),
                  pl.BlockSpec((tm, D_MODEL), lambda i, c: (i, 0)),
                  pl.BlockSpec((D_MODEL, tf), lambda i, c: (0, c)),
                  pl.BlockSpec((D_MODEL, tf), lambda i, c: (0, c)),
                  pl.BlockSpec((tf, D_MODEL), lambda i, c: (c, 0)),
                  nwspec, modspec, nwspec, modspec],
        out_specs=(pl.BlockSpec((tm, D_MODEL), lambda i, c: (i, 0)),
                   pl.BlockSpec((tm, D_MODEL), lambda i, c: (i, 0))),
        scratch_shapes=[pltpu.VMEM((tm, D_MODEL), F32)],
        compiler_params=_cparams(("parallel", "arbitrary")),
        name="ffn",
    )(h, x1, wg, wu, wd, nw, mod, nw1, mod1)


EDGE_ROWS = 16


def _convedge_kernel(hf_ref, hl_ref, wc_ref, wu_ref, vf_ref, vl_ref):
    nt = hf_ref.shape[0]
    for h_ref, v_ref in ((hf_ref, vf_ref), (hl_ref, vl_ref)):
        h = h_ref[...].reshape(nt * EDGE_ROWS, D_MODEL)
        v = _dot(h, wc_ref[...]) * _dot(h, wu_ref[...])
        v_ref[...] = v.reshape(nt, EDGE_ROWS, D_MODEL)


def _convedge(h3, w_in_bf, *, tm):
    n = h3.shape[0]
    nt = n // tm
    h3t = h3.reshape(nt, tm, D_MODEL)
    last = tm // EDGE_ROWS - 1
    shape = jax.ShapeDtypeStruct((nt, EDGE_ROWS, D_MODEL), F32)
    return pl.pallas_call(
        _convedge_kernel,
        out_shape=(shape, shape),
        grid=(1,),
        in_specs=[pl.BlockSpec((nt, EDGE_ROWS, D_MODEL), lambda i: (0, 0, 0)),
                  pl.BlockSpec((nt, EDGE_ROWS, D_MODEL), lambda i: (0, last, 0)),
                  pl.BlockSpec((D_MODEL, D_MODEL), lambda i: (0, 1)),
                  pl.BlockSpec((D_MODEL, D_MODEL), lambda i: (0, 2))],
        out_specs=(pl.BlockSpec((nt, EDGE_ROWS, D_MODEL), lambda i: (0, 0, 0)),
                   pl.BlockSpec((nt, EDGE_ROWS, D_MODEL), lambda i: (0, 0, 0))),
        compiler_params=_cparams(("arbitrary",)),
        name="convedge",
    )(h3t, h3t, w_in_bf, w_in_bf)


def _convmix_kernel(h_ref, x_ref, vl_ref, vf_ref, win_ref, cw_ref, wout_ref, nw_ref, mod_ref, r_ref,
                    x3_ref, h4_ref, comb_ref, sel_ref, rank_ref):
    i = pl.program_id(0)
    tm = h_ref.shape[0]
    _, _, gt1, sh2, sc2, _ = _mod_rows(mod_ref, 0)
    z = _dot(h_ref[...], win_ref[...])
    b = z[:, :D_MODEL]
    v = z[:, D_MODEL:2 * D_MODEL] * z[:, 2 * D_MODEL:]
    has_prev = (i > 0).astype(F32)
    has_next = (i < pl.num_programs(0) - 1).astype(F32)
    prev_row = vl_ref[0, EDGE_ROWS - 1:EDGE_ROWS, :] * has_prev
    next_row = vf_ref[0, 0:1, :] * has_next
    rows = lax.broadcasted_iota(jnp.int32, (tm, 1), 0)
    v_dn = jnp.where(rows == 0, prev_row, pltpu.roll(v, 1, axis=0))
    v_up = jnp.where(rows == tm - 1, next_row, pltpu.roll(v, tm - 1, axis=0))
    conv = v_dn * cw_ref[0, 0:1, :] + v * cw_ref[0, 1:2, :] + v_up * cw_ref[0, 2:3, :]
    y = _dot((b * conv).astype(BF16), wout_ref[...])
    x3 = x_ref[...] + gt1 * _rms(y, nw_ref[0, 1:2, :])
    x3_ref[...] = x3
    h4 = _rms(x3, nw_ref[0, 2:3, :]) * (1.0 + sc2) + sh2
    h4_ref[...] = h4.astype(BF16)
    hhi, hlo = _split_bf16(h4)
    rhi, rlo = _split_bf16(r_ref[...])
    logits = _dot(hhi, rhi) + (_dot(hlo, rhi) + _dot(hhi, rlo))
    lane = lax.broadcasted_iota(jnp.int32, logits.shape, 1)
    logits = jnp.where(lane < N_EXPERTS, logits, -jnp.inf)
    e = jnp.exp(logits - jnp.max(logits, axis=-1, keepdims=True))
    probs = e / jnp.sum(e, axis=-1, keepdims=True)
    v1 = jnp.max(probs, axis=-1, keepdims=True)
    i1 = jnp.min(jnp.where(probs == v1, lane, LANES), axis=-1, keepdims=True)
    rest = jnp.where(lane == i1, -1.0, probs)
    v2 = jnp.max(rest, axis=-1, keepdims=True)
    i2 = jnp.min(jnp.where(rest == v2, lane, LANES), axis=-1, keepdims=True)
    tot = v1 + v2
    comb_ref[...] = jnp.where(lane == i1, v1 / tot, 0.0) + jnp.where(lane == i2, v2 / tot, 0.0)
    sel = jnp.where(jnp.logical_or(lane == i1, lane == i2), 1.0, 0.0)
    sel_ref[...] = sel
    earlier = lax.broadcasted_iota(jnp.int32, (tm, tm), 1) < lax.broadcasted_iota(jnp.int32, (tm, tm), 0)
    rank_ref[...] = _dot(jnp.where(earlier, 1.0, 0.0).astype(BF16), sel.astype(BF16))


def _convmix(h3, x2, v_first, v_last, w_in_bf, conv_w, w_out_bf, nw, mod, router_pad, *, tm):
    n = h3.shape[0]
    nt = n // tm
    const = lambda i: (0, 0)
    lanes_shape = jax.ShapeDtypeStruct((n, LANES), F32)
    lanes_spec = pl.BlockSpec((tm, LANES), lambda i: (i, 0))
    return pl.pallas_call(
        _convmix_kernel,
        out_shape=(jax.ShapeDtypeStruct((n, D_MODEL), F32), jax.ShapeDtypeStruct((n, D_MODEL), BF16),
                   lanes_shape, lanes_shape, lanes_shape),
        grid=(nt,),
        in_specs=[pl.BlockSpec((tm, D_MODEL), lambda i: (i, 0)),
                  pl.BlockSpec((tm, D_MODEL), lambda i: (i, 0)),
                  pl.BlockSpec((1, EDGE_ROWS, D_MODEL), lambda i: (jnp.maximum(i - 1, 0), 0, 0)),
                  pl.BlockSpec((1, EDGE_ROWS, D_MODEL), lambda i: (jnp.minimum(i + 1, nt - 1), 0, 0)),
                  pl.BlockSpec((D_MODEL, 3 * D_MODEL), const),
                  pl.BlockSpec((1, 3, D_MODEL), lambda i: (0, 0, 0)),
                  pl.BlockSpec((D_MODEL, D_MODEL), const),
                  pl.BlockSpec((1, 4, D_MODEL), lambda i: (0, 0, 0)),
                  pl.BlockSpec((1, 8, 6 * D_MODEL), lambda i: (0, 0, 0)),
                  pl.BlockSpec((D_MODEL, LANES), const)],
        out_specs=(pl.BlockSpec((tm, D_MODEL), lambda i: (i, 0)),
                   pl.BlockSpec((tm, D_MODEL), lambda i: (i, 0)),
                   lanes_spec, lanes_spec, lanes_spec),
        compiler_params=_cparams(("parallel",)),
        name="convmix",
    )(h3, x2, v_last, v_first, w_in_bf, conv_w, w_out_bf, nw, mod, router_pad)


ROW_TILE = 512
FLAG_ACTIVE, FLAG_FIRST, FLAG_LAST = 1, 2, 4


def _count_le(sorted_vals, x):
    return jnp.sum((sorted_vals[None, :] <= x[:, None]).astype(jnp.int32), axis=1)


def _ragged_items(n_per_group, lo_per_group, n_items):
    off_end = jnp.cumsum(n_per_group)
    off = off_end - n_per_group
    total = off_end[-1]
    k = jnp.minimum(jnp.arange(n_items, dtype=jnp.int32), total - 1)
    grp = jnp.minimum(_count_le(off_end, k), n_per_group.shape[0] - 1)
    member = lo_per_group[grp] + k - off[grp]
    active = jnp.arange(n_items, dtype=jnp.int32) < total
    return grp, member, active


def _flags(active, key):
    prev_differs = jnp.concatenate([jnp.array([True]), key[1:] != key[:-1]])
    next_active = jnp.concatenate([active[1:], jnp.array([False])])
    next_differs = jnp.concatenate([key[1:] != key[:-1], jnp.array([True])])
    last = jnp.logical_or(next_differs, jnp.logical_not(next_active))
    a = active.astype(jnp.int32)
    return a * (FLAG_ACTIVE + FLAG_FIRST * prev_differs.astype(jnp.int32) + FLAG_LAST * last.astype(jnp.int32))


def _route(sel, lrank, n_row_tiles):
    n = sel.shape[0]
    n_tok_tiles = n // ROW_TILE
    i32 = jnp.int32
    sel8 = sel[:, :N_EXPERTS].astype(i32)
    cnt_tile = jnp.sum(sel8.reshape(n_tok_tiles, ROW_TILE, N_EXPERTS), axis=1)
    cum_end = jnp.cumsum(cnt_tile, axis=0)
    cum_beg = cum_end - cnt_tile
    cnt = cum_end[-1]
    cnt_pad = (cnt + ROW_TILE - 1) // ROW_TILE * ROW_TILE
    grp_end = jnp.cumsum(cnt_pad)
    start = grp_end - cnt_pad
    rank = lrank[:, :N_EXPERTS].astype(i32) + jnp.repeat(cum_beg, ROW_TILE, axis=0)
    pos = jnp.where(sel8 > 0, start[None, :] + rank, -1).astype(F32)
    r = jnp.arange(n_row_tiles, dtype=i32)
    base = r * ROW_TILE
    tile_valid = base < grp_end[-1]
    n_valid = grp_end[-1] // ROW_TILE
    tile_exp = jnp.minimum(_count_le(grp_end, base), N_EXPERTS - 1)
    last_exp = tile_exp[n_valid - 1]
    tile_exp = jnp.where(tile_valid, tile_exp, last_exp)
    tile_src = jnp.where(tile_valid, r, n_valid - 1)
    tile_first = jnp.logical_and(tile_valid, base == start[tile_exp])
    rho0 = base - start[tile_exp]
    rho1 = jnp.minimum(rho0 + ROW_TILE, cnt[tile_exp])
    ends = cum_end.T[tile_exp]
    c_lo = jnp.sum((ends <= rho0[:, None]).astype(i32), axis=1)
    c_hi = jnp.sum((ends <= (rho1 - 1)[:, None]).astype(i32), axis=1)
    n_chunks = jnp.where(tile_valid, c_hi - c_lo + 1, 0)
    n_items = n_row_tiles + N_EXPERTS * n_tok_tiles
    g_tile, g_chunk, g_active = _ragged_items(n_chunks, c_lo, n_items)
    gather_items = (g_tile, g_chunk, tile_exp[g_tile], _flags(g_active, g_tile))
    row_lo = (start[None, :] + cum_beg).reshape(-1)
    row_hi = (start[None, :] + cum_end - 1).reshape(-1)
    n_rt = jnp.where(cnt_tile.reshape(-1) > 0, row_hi // ROW_TILE - row_lo // ROW_TILE + 1, 0)
    c_pair, c_row, c_active = _ragged_items(n_rt, row_lo // ROW_TILE, n_items)
    c_tok = c_pair // N_EXPERTS
    combine_items = (c_tok, c_row, c_pair % N_EXPERTS, _flags(c_active, c_tok))
    tiles = (tile_exp, tile_src, tile_valid.astype(i32), tile_first.astype(i32))
    return pos, gather_items, tiles, combine_items


def _gather_kernel(it_tile, it_chunk, it_exp, it_flag, pos_ref, h_ref, w_ref, xs_ref, ws_ref, accx, accw):
    k = pl.program_id(0)
    flag = it_flag[k]

    @pl.when(flag & FLAG_ACTIVE != 0)
    def _():
        rows = it_tile[k] * ROW_TILE + lax.broadcasted_iota(jnp.int32, (ROW_TILE, 1), 0)
        onehot = jnp.where(pos_ref[0] == rows.astype(F32), 1.0, 0.0).astype(BF16)
        px = _dot(onehot, h_ref[...])
        pw = _dot(onehot, w_ref[...])

        @pl.when(flag & FLAG_FIRST != 0)
        def _():
            accx[...] = px
            accw[...] = pw

        @pl.when(flag & FLAG_FIRST == 0)
        def _():
            accx[...] += px
            accw[...] += pw

        @pl.when(flag & FLAG_LAST != 0)
        def _():
            xs_ref[...] = accx[...].astype(BF16)
            ws_ref[...] = accw[...]


def _gather_rows(items, pos_t, h4, w_split, n_rows):
    n_items = items[0].shape[0]
    grid_spec = pltpu.PrefetchScalarGridSpec(
        num_scalar_prefetch=4, grid=(n_items,),
        in_specs=[pl.BlockSpec((1, 1, ROW_TILE), lambda k, t, c, e, f: (e[k], 0, c[k])),
                  pl.BlockSpec((ROW_TILE, D_MODEL), lambda k, t, c, e, f: (c[k], 0)),
                  pl.BlockSpec((ROW_TILE, LANES), lambda k, t, c, e, f: (c[k], 0))],
        out_specs=(pl.BlockSpec((ROW_TILE, D_MODEL), lambda k, t, c, e, f: (t[k], 0)),
                   pl.BlockSpec((ROW_TILE, LANES), lambda k, t, c, e, f: (t[k], 0))),
        scratch_shapes=[pltpu.VMEM((ROW_TILE, D_MODEL), F32), pltpu.VMEM((ROW_TILE, LANES), F32)])
    return pl.pallas_call(
        _gather_kernel, grid_spec=grid_spec,
        out_shape=(jax.ShapeDtypeStruct((n_rows, D_MODEL), BF16), jax.ShapeDtypeStruct((n_rows, LANES), F32)),
        compiler_params=_cparams(("arbitrary",)),
        name="moe_gather",
    )(*items, pos_t, h4, w_split)


def _experts_kernel(t_exp, t_src, t_valid, t_first, xs_ref, ws_ref, wg_ref, wu_ref, wd_ref, ys_ref,
                    wg_c, wu_c, wd_c, acc_ref):
    r = pl.program_id(0)
    c = pl.program_id(1)
    last = c == pl.num_programs(1) - 1

    @pl.when(t_first[r] != 0)
    def _():
        wg_c[c] = wg_ref[0, 0].astype(BF16)
        wu_c[c] = wu_ref[0, 0].astype(BF16)
        wd_c[c] = wd_ref[0, 0].astype(BF16)

    @pl.when(t_valid[r] != 0)
    def _():
        e = t_exp[r]
        ws = ws_ref[...]
        lane = lax.broadcasted_iota(jnp.int32, ws.shape, 1)
        mine = jnp.logical_or(lane == e, lane == e + N_EXPERTS)
        w = jnp.sum(jnp.where(mine, ws, 0.0), axis=-1, keepdims=True)
        x = xs_ref[...]
        g = _dot(x, wg_c[c])
        u = _dot(x, wu_c[c])
        part = _dot((g * jax.nn.sigmoid(g) * u * w).astype(BF16), wd_c[c])

        @pl.when(c == 0)
        def _():
            acc_ref[...] = part

        @pl.when(c > 0)
        def _():
            acc_ref[...] += part

        @pl.when(last)
        def _():
            ys_ref[...] = acc_ref[...].astype(BF16)

    @pl.when(jnp.logical_and(t_valid[r] == 0, last))
    def _():
        ys_ref[...] = jnp.zeros(ys_ref.shape, BF16)


def _experts(tiles, xs, ws, wg, wu, wd, *, tf):
    n_rows = xs.shape[0]
    n_ch = D_EXPERT // tf

    def w_in(r, c, e, s, v, f):
        return (0, e[r], 0, jnp.where(f[r] != 0, c, n_ch - 1))

    def w_dn(r, c, e, s, v, f):
        return (0, e[r], jnp.where(f[r] != 0, c, n_ch - 1), 0)

    grid_spec = pltpu.PrefetchScalarGridSpec(
        num_scalar_prefetch=4, grid=(n_rows // ROW_TILE, n_ch),
        in_specs=[pl.BlockSpec((ROW_TILE, D_MODEL), lambda r, c, e, s, v, f: (s[r], 0)),
                  pl.BlockSpec((ROW_TILE, LANES), lambda r, c, e, s, v, f: (s[r], 0)),
                  pl.BlockSpec((1, 1, D_MODEL, tf), w_in),
                  pl.BlockSpec((1, 1, D_MODEL, tf), w_in),
                  pl.BlockSpec((1, 1, tf, D_MODEL), w_dn)],
        out_specs=pl.BlockSpec((ROW_TILE, D_MODEL), lambda r, c, e, s, v, f: (r, 0)),
        scratch_shapes=[pltpu.VMEM((n_ch, D_MODEL, tf), BF16), pltpu.VMEM((n_ch, D_MODEL, tf), BF16),
                        pltpu.VMEM((n_ch, tf, D_MODEL), BF16), pltpu.VMEM((ROW_TILE, D_MODEL), F32)])
    return pl.pallas_call(
        _experts_kernel, grid_spec=grid_spec,
        out_shape=jax.ShapeDtypeStruct((n_rows, D_MODEL), BF16),
        compiler_params=_cparams(("arbitrary", "arbitrary")),
        name="moe_experts",
    )(*tiles, xs, ws, wg, wu, wd)


def _combine_kernel(it_tok, it_row, it_exp, it_flag, pos_ref, ys_ref, x_ref, nw_ref, mod_ref, o_ref, acc_ref):
    k = pl.program_id(0)
    flag = it_flag[k]

    @pl.when(flag & FLAG_ACTIVE != 0)
    def _():
        pos = pos_ref[...]
        lane = lax.broadcasted_iota(jnp.int32, pos.shape, 1)
        mine = jnp.sum(jnp.where(lane == it_exp[k], pos, 0.0), axis=-1, keepdims=True)
        cols = it_row[k] * ROW_TILE + lax.broadcasted_iota(jnp.int32, (1, ROW_TILE), 1)
        onehot = jnp.where(mine == cols.astype(F32), 1.0, 0.0).astype(BF16)
        part = _dot(onehot, ys_ref[...])

        @pl.when(flag & FLAG_FIRST != 0)
        def _():
            acc_ref[...] = part

        @pl.when(flag & FLAG_FIRST == 0)
        def _():
            acc_ref[...] += part

        @pl.when(flag & FLAG_LAST != 0)
        def _():
            gt2 = _mod_rows(mod_ref, 0)[5]
            o_ref[...] = x_ref[...] + gt2 * _rms(acc_ref[...], nw_ref[0, 3:4, :])


def _combine(items, pos_pad, ys, x3, nw, mod):
    n = x3.shape[0]
    n_items = items[0].shape[0]
    grid_spec = pltpu.PrefetchScalarGridSpec(
        num_scalar_prefetch=4, grid=(n_items,),
        in_specs=[pl.BlockSpec((ROW_TILE, LANES), lambda k, t, r, e, f: (t[k], 0)),
                  pl.BlockSpec((ROW_TILE, D_MODEL), lambda k, t, r, e, f: (r[k], 0)),
                  pl.BlockSpec((ROW_TILE, D_MODEL), lambda k, t, r, e, f: (t[k], 0)),
                  pl.BlockSpec((1, 4, D_MODEL), lambda k, t, r, e, f: (0, 0, 0)),
                  pl.BlockSpec((1, 8, 6 * D_MODEL), lambda k, t, r, e, f: (0, 0, 0))],
        out_specs=pl.BlockSpec((ROW_TILE, D_MODEL), lambda k, t, r, e, f: (t[k], 0)),
        scratch_shapes=[pltpu.VMEM((ROW_TILE, D_MODEL), F32)])
    return pl.pallas_call(
        _combine_kernel, grid_spec=grid_spec,
        out_shape=jax.ShapeDtypeStruct((n, D_MODEL), F32),
        compiler_params=_cparams(("arbitrary",)),
        name="moe_combine",
    )(*items, pos_pad, ys, x3, nw, mod)


def _moe(h4, comb, sel, lrank, x3, wg, wu, wd, nw, mod):
    n = h4.shape[0]
    n_rows = 2 * n + N_EXPERTS * ROW_TILE
    pos, gather_items, tiles, combine_items = _route(sel, lrank, n_rows // ROW_TILE)
    pos_t = pos.T.reshape(N_EXPERTS, 1, n)
    pos_pad = jnp.pad(pos, ((0, 0), (0, LANES - N_EXPERTS)), constant_values=-1.0)
    comb8 = comb[:, :N_EXPERTS]
    hi = comb8.astype(BF16)
    lo = (comb8 - hi.astype(F32)).astype(BF16)
    w_split = jnp.pad(jnp.concatenate([hi, lo], axis=1), ((0, 0), (0, LANES - 2 * N_EXPERTS)))
    xs, ws = _gather_rows(gather_items, pos_t, h4, w_split, n_rows)
    ys = _experts(tiles, xs, ws, wg, wu, wd, tf=512)
    return _combine(combine_items, pos_pad, ys, x3, nw, mod)


def kernel(x, c, ctx, c_ctx, ada_w, ada_b, norm_w, e_w_in, e_q_gain, e_k_gain, e_w_out, e_ffn_gate,
           e_ffn_up, e_ffn_down, o_w_in, o_conv_w, o_w_out, o_router, o_exp_gate, o_exp_up, o_exp_down):
    assert x.shape[0] == 1 and x.shape[2] == D_MODEL and ada_w.shape[0] == 2
    n = x.shape[1]
    x2d = x[0]
    ctx2d = ctx[0]
    mod = _ada(c, c_ctx, ada_w, ada_b)
    mod0, mod1 = mod[0:1], mod[1:2]
    nw0, nw1 = norm_w[0:1], norm_w[1:2]

    w_in = e_w_in[0].astype(BF16)
    scale = HEAD_DIM ** -0.5 * np.log2(np.e)
    gain = jnp.concatenate([jnp.tile(e_q_gain[0], N_Q_HEADS) * scale,
                            jnp.tile(e_k_gain[0], N_KV_HEADS)])[None, :]
    score_bound = 1.02 * HEAD_DIM * scale * jnp.max(jnp.abs(e_q_gain[0])) * jnp.max(jnp.abs(e_k_gain[0]))
    qT, k, vT, f = _evenproj(x2d, nw0, mod0, w_in, gain, latent=True, tm=512)
    kc, vcT = _evenproj(ctx2d, nw0, mod0, w_in, gain, latent=False, tm=ctx2d.shape[0])
    attnT = _attention(qT, k, vT, kc, vcT, score_bound, tq=512, tk=512)
    four = _fourier(f)
    w_out = e_w_out[0].astype(BF16)
    x1, h2 = _outproj(attnT, four, x2d, w_out[:ATTN_WIDTH], w_out[ATTN_WIDTH:], nw0, mod0, tm=512)
    x2, h3 = _ffn(h2, x1, e_ffn_gate[0].astype(BF16), e_ffn_up[0].astype(BF16),
                  e_ffn_down[0].astype(BF16), nw0, mod0, nw1, mod1, tm=1024, tf=1408)

    ow_in = o_w_in[0].astype(BF16)
    v_first, v_last = _convedge(h3, ow_in, tm=ROW_TILE)
    router_pad = jnp.pad(o_router[0], ((0, 0), (0, LANES - N_EXPERTS)))
    x3, h4, comb, sel, lrank = _convmix(h3, x2, v_first, v_last, ow_in, o_conv_w, o_w_out[0].astype(BF16),
                                        nw1, mod1, router_pad, tm=ROW_TILE)
    out = _moe(h4, comb, sel, lrank, x3, o_exp_gate, o_exp_up, o_exp_down, nw1, mod1)
    return out[None]
```

```python
import functools

import numpy as np
import jax
import jax.numpy as jnp
from jax import lax
from jax.experimental import pallas as pl
from jax.experimental.pallas import tpu as pltpu

D_MODEL = 1024
GRID_W = 64
HEAD_DIM = 64
N_Q_HEADS = 12
N_KV_HEADS = 4
Q_PER_KV = N_Q_HEADS // N_KV_HEADS
ATTN_WIDTH = N_Q_HEADS * HEAD_DIM
KV_WIDTH = N_KV_HEADS * HEAD_DIM
QK_WIDTH = ATTN_WIDTH + KV_WIDTH
N_FOURIER_GROUPS = 4
FOURIER_GROUP_DIM = 64
FOURIER_WIDTH = N_FOURIER_GROUPS * FOURIER_GROUP_DIM
EVEN_IN_WIDTH = ATTN_WIDTH + 2 * KV_WIDTH + FOURIER_WIDTH
D_FF = 2816
N_EXPERTS = 8
D_EXPERT = 3584
ONES_ROWS = 16
V_ROWS = HEAD_DIM + ONES_ROWS
EXP2_SAFE_BOUND = 60.0
ROPE_THETA = 10000.0
ROPE_HALF = HEAD_DIM // 4
NORM_EPS = 1e-6

LANES = 128
VMEM_LIMIT = 56 * 1024 * 1024

BF16 = jnp.bfloat16
QK_DTYPE = jnp.float8_e4m3fn
F32 = jnp.float32


def _cparams(semantics, vmem=VMEM_LIMIT):
    return pltpu.CompilerParams(dimension_semantics=semantics, vmem_limit_bytes=vmem)


def _dot(a, b):
    return jnp.dot(a, b, preferred_element_type=F32)


def _split_bf16(x):
    hi = x.astype(BF16)
    lo = (x - hi.astype(F32)).astype(BF16)
    return hi, lo


def _rms(x, g):
    return x * lax.rsqrt(jnp.mean(x * x, axis=-1, keepdims=True) + NORM_EPS) * g


def _mod_rows(mod_ref, row):
    return [mod_ref[0, row:row + 1, i * D_MODEL:(i + 1) * D_MODEL] for i in range(6)]


@functools.lru_cache(maxsize=None)
def _rope_tables(n_tokens):
    t = np.arange(n_tokens)
    row = (t // GRID_W).astype(np.float64)
    col = (t % GRID_W).astype(np.float64)
    inv = ROPE_THETA ** (-np.arange(ROPE_HALF, dtype=np.float64) / ROPE_HALF)
    ar, ac = row[:, None] * inv, col[:, None] * inv
    cos = np.concatenate([np.cos(ar), np.cos(ar), np.cos(ac), np.cos(ac)], axis=-1)
    sin = np.concatenate([-np.sin(ar), np.sin(ar), -np.sin(ac), np.sin(ac)], axis=-1)
    reps = LANES // HEAD_DIM
    return (np.tile(cos, (1, reps)).astype(np.float32), np.tile(sin, (1, reps)).astype(np.float32))


@functools.lru_cache(maxsize=None)
def _head_matrices():
    head = np.arange(QK_WIDTH) // HEAD_DIM
    red = (head[:, None] == np.arange(LANES)[None, :]).astype(np.float32) / HEAD_DIM
    exp = (np.arange(LANES)[:, None] == head[None, :]).astype(np.float32)
    return red, exp


@functools.lru_cache(maxsize=None)
def _fourier_tables(n_tokens, kb):
    n2 = LANES
    n1 = n_tokens // n2
    c = np.arange(FOURIER_GROUP_DIM)
    ang = 2 * np.pi * np.outer(c, c) / FOURIER_GROUP_DIM
    eye = np.eye(N_FOURIER_GROUPS)
    cs = np.concatenate([np.kron(eye, np.cos(ang)), np.kron(eye, np.sin(ang))], axis=1)
    k1 = np.arange(n1)
    th = 2 * np.pi * np.outer(k1, k1) / n1
    cr, ci = np.cos(th), -np.sin(th)
    base = np.block([[cr, ci], [ci, -cr]])
    psi = 2 * np.pi * np.outer(k1, np.arange(n2)) / n_tokens
    twr = np.cos(psi).reshape(n1, n2 // kb, kb).transpose(1, 0, 2)
    twi = (-np.sin(psi)).reshape(n1, n2 // kb, kb).transpose(1, 0, 2)
    k2 = np.arange(n2)
    ph = 2 * np.pi * np.outer(k2, k2) / n2
    fr, fi = np.cos(ph), -np.sin(ph)
    scale = 1.0 / np.sqrt(n_tokens * FOURIER_GROUP_DIM)
    m3 = np.stack([fr, -fi], axis=-1) * scale
    wb = np.einsum('knr,uv->kunrv', m3, np.eye(kb)).reshape(n2 * kb, n2 * 2 * kb)
    f32 = np.float32
    return cs.astype(f32), base.astype(f32), twr.astype(f32), twi.astype(f32), wb.astype(f32)


def _ada_kernel(cb_ref, w_ref, b_ref, o_ref):
    tn = o_ref.shape[-1]
    o_ref[...] = jnp.zeros(o_ref.shape, F32)
    for r in range(2):
        cb = cb_ref[r]
        s = cb * jax.nn.sigmoid(cb)
        for j in range(tn // LANES):
            sl = slice(j * LANES, (j + 1) * LANES)
            col = jnp.sum(s * w_ref[0, :, sl], axis=0, keepdims=True)
            o_ref[0, r:r + 1, sl] = col + b_ref[0, :, sl]


def _ada(c, c_ctx, ada_w, ada_b):
    depth = ada_w.shape[0]
    n = ada_w.shape[-1]
    tn = 1536
    cb = jnp.stack([jnp.broadcast_to(c[0][:, None], (D_MODEL, LANES)),
                    jnp.broadcast_to(c_ctx[:, None], (D_MODEL, LANES))])
    return pl.pallas_call(
        _ada_kernel,
        out_shape=jax.ShapeDtypeStruct((depth, 8, n), F32),
        grid=(depth, n // tn),
        in_specs=[pl.BlockSpec((2, D_MODEL, LANES), lambda i, j: (0, 0, 0)),
                  pl.BlockSpec((1, D_MODEL, tn), lambda i, j: (i, 0, j)),
                  pl.BlockSpec((1, 1, tn), lambda i, j: (i, 0, j))],
        out_specs=pl.BlockSpec((1, 8, tn), lambda i, j: (i, 0, j)),
        compiler_params=_cparams(("parallel", "parallel")),
        name="ada",
    )(cb, ada_w, ada_b[:, None, :])


def _evenproj_kernel(*refs, row, latent):
    if latent:
        (x_ref, nw_ref, mod_ref, w_ref, gain_ref, red_ref, exp_ref, cos_ref, sin_ref,
         qT_ref, k_ref, vT_ref, f_ref) = refs
    else:
        x_ref, nw_ref, mod_ref, w_ref, gain_ref, red_ref, exp_ref, k_ref, vT_ref = refs
    sh, sc = _mod_rows(mod_ref, row)[:2]
    h = (_rms(x_ref[...], nw_ref[0, 0:1, :]) * (1.0 + sc) + sh).astype(BF16)
    z = _dot(h, w_ref[...])
    zqk = z[:, :QK_WIDTH]
    hi, lo = _split_bf16(zqk * zqk)
    red = red_ref[...].astype(BF16)
    ms = _dot(hi, red) + _dot(lo, red)
    rhi, rlo = _split_bf16(lax.rsqrt(ms + NORM_EPS))
    expm = exp_ref[...].astype(BF16)
    yn = zqk * (_dot(rhi, expm) + _dot(rlo, expm)) * gain_ref[...]
    if latent:
        lane = lax.broadcasted_iota(jnp.int32, (1, LANES), 1)
        first_half = (lane // ROPE_HALF) % 2 == 0
        cos, sin = cos_ref[...], sin_ref[...]
        chunks = []
        for c in range(QK_WIDTH // LANES):
            yc = yn[:, c * LANES:(c + 1) * LANES]
            partner = jnp.where(first_half, pltpu.roll(yc, LANES - ROPE_HALF, axis=1),
                                pltpu.roll(yc, ROPE_HALF, axis=1))
            chunks.append(yc * cos + partner * sin)
        yn = jnp.concatenate(chunks, axis=1)
        qT_ref[...] = yn[:, :ATTN_WIDTH].T.astype(QK_DTYPE)
        f_ref[...] = z[:, QK_WIDTH + KV_WIDTH:].astype(BF16)
    for g in range(N_KV_HEADS):
        k_ref[g] = yn[:, ATTN_WIDTH + g * HEAD_DIM:ATTN_WIDTH + (g + 1) * HEAD_DIM].astype(QK_DTYPE)
    vT = z[:, QK_WIDTH:QK_WIDTH + KV_WIDTH].T.astype(BF16)
    ones = jnp.ones((ONES_ROWS, vT.shape[1]), BF16)
    for g in range(N_KV_HEADS):
        vT_ref[g * V_ROWS:g * V_ROWS + HEAD_DIM, :] = vT[g * HEAD_DIM:(g + 1) * HEAD_DIM]
        vT_ref[g * V_ROWS + HEAD_DIM:(g + 1) * V_ROWS, :] = ones


def _evenproj(x2d, nw, mod, w_bf, gain, *, latent, tm):
    n = x2d.shape[0]
    red, expm = _head_matrices()
    const = lambda i: (0, 0)
    in_specs = [pl.BlockSpec((tm, D_MODEL), lambda i: (i, 0)),
                pl.BlockSpec((1, 4, D_MODEL), lambda i: (0, 0, 0)),
                pl.BlockSpec((1, 8, 6 * D_MODEL), lambda i: (0, 0, 0)),
                pl.BlockSpec((D_MODEL, EVEN_IN_WIDTH), const),
                pl.BlockSpec((1, QK_WIDTH), const),
                pl.BlockSpec((QK_WIDTH, LANES), const),
                pl.BlockSpec((LANES, QK_WIDTH), const)]
    args = [x2d, nw, mod, w_bf, gain, jnp.asarray(red), jnp.asarray(expm)]
    k_shape = jax.ShapeDtypeStruct((N_KV_HEADS, n, HEAD_DIM), QK_DTYPE)
    vT_shape = jax.ShapeDtypeStruct((N_KV_HEADS * V_ROWS, n), BF16)
    k_spec = pl.BlockSpec((N_KV_HEADS, tm, HEAD_DIM), lambda i: (0, i, 0))
    vT_spec = pl.BlockSpec((N_KV_HEADS * V_ROWS, tm), lambda i: (0, i))
    if latent:
        cos, sin = _rope_tables(n)
        in_specs += [pl.BlockSpec((tm, LANES), lambda i: (i, 0))] * 2
        args += [jnp.asarray(cos), jnp.asarray(sin)]
        out_shape = (jax.ShapeDtypeStruct((ATTN_WIDTH, n), QK_DTYPE), k_shape, vT_shape,
                     jax.ShapeDtypeStruct((n, FOURIER_WIDTH), BF16))
        out_specs = (pl.BlockSpec((ATTN_WIDTH, tm), lambda i: (0, i)), k_spec, vT_spec,
                     pl.BlockSpec((tm, FOURIER_WIDTH), lambda i: (i, 0)))
    else:
        out_shape = (k_shape, vT_shape)
        out_specs = (k_spec, vT_spec)
    return pl.pallas_call(
        functools.partial(_evenproj_kernel, row=0 if latent else 1, latent=latent),
        out_shape=out_shape, grid=(n // tm,), in_specs=in_specs, out_specs=out_specs,
        compiler_params=_cparams(("parallel",)),
        name="evenproj_lat" if latent else "evenproj_ctx",
    )(*args)


def _visit_all(visit, k_ref, vT_ref, kc_ref, vcT_ref, tk):
    def body(c, carry):
        off = pl.multiple_of(c * tk, tk)
        visit(k_ref[0, pl.ds(off, tk), :], vT_ref[:, pl.ds(off, tk)])
        return carry

    lax.fori_loop(0, k_ref.shape[1] // tk, body, 0)
    visit(kc_ref[0], vcT_ref[...])


def _attn_bounded_kernel(qT_ref, k_ref, vT_ref, kc_ref, vcT_ref, o_ref, acc_sc, s_sc, *, tk):
    n_tiles = k_ref.shape[1] // tk
    acc_sc[...] = jnp.zeros(acc_sc.shape, F32)

    def q(j):
        return qT_ref[j * HEAD_DIM:(j + 1) * HEAD_DIM, :]

    def keys(c):
        return k_ref[0, pl.ds(pl.multiple_of(c * tk, tk), tk), :]

    def consume(j, s, vt):
        acc_sc[j] += _dot(vt, jnp.exp2(s).astype(BF16))

    s_sc[...] = _dot(keys(0), q(0))

    def body(c, carry):
        kt = keys(c)
        vt = vT_ref[:, pl.ds(pl.multiple_of(c * tk, tk), tk)]
        s = s_sc[...]
        for j in range(Q_PER_KV):
            if j + 1 < Q_PER_KV:
                s_next = _dot(kt, q(j + 1))
            else:
                s_next = _dot(keys(jnp.minimum(c + 1, n_tiles - 1)), q(0))
            consume(j, s, vt)
            s = s_next
        s_sc[...] = s
        return carry

    lax.fori_loop(0, n_tiles, body, 0)
    kc, vc = kc_ref[0], vcT_ref[...]
    s = _dot(kc, q(0))
    for j in range(Q_PER_KV):
        s_next = _dot(kc, q(j + 1)) if j + 1 < Q_PER_KV else None
        consume(j, s, vc)
        s = s_next
    for j in range(Q_PER_KV):
        acc = acc_sc[j]
        o_ref[j * HEAD_DIM:(j + 1) * HEAD_DIM, :] = (acc[:HEAD_DIM] / acc[HEAD_DIM:HEAD_DIM + 1]).astype(BF16)


def _attn_online_kernel(qT_ref, k_ref, vT_ref, kc_ref, vcT_ref, o_ref, m_sc, acc_sc, *, tk):
    m_sc[...] = jnp.full(m_sc.shape, -jnp.inf, F32)
    acc_sc[...] = jnp.zeros(acc_sc.shape, F32)

    def visit(kt, vt):
        for j in range(Q_PER_KV):
            s = _dot(kt, qT_ref[j * HEAD_DIM:(j + 1) * HEAD_DIM, :])
            m_old = m_sc[j]
            m_new = jnp.maximum(m_old, jnp.max(s, axis=0, keepdims=True))
            p = jnp.exp2(s - m_new).astype(BF16)
            acc_sc[j] = jnp.exp2(m_old - m_new) * acc_sc[j] + _dot(vt, p)
            m_sc[j] = m_new

    _visit_all(visit, k_ref, vT_ref, kc_ref, vcT_ref, tk)
    for j in range(Q_PER_KV):
        acc = acc_sc[j]
        o_ref[j * HEAD_DIM:(j + 1) * HEAD_DIM, :] = (acc[:HEAD_DIM] / acc[HEAD_DIM:HEAD_DIM + 1]).astype(BF16)


def _attention(qT, k, vT, kc, vcT, score_bound, *, tq, tk):
    n = qT.shape[1]
    n_ctx = kc.shape[1]
    gw = Q_PER_KV * HEAD_DIM
    common = dict(
        out_shape=jax.ShapeDtypeStruct((ATTN_WIDTH, n), BF16),
        grid=(N_KV_HEADS, n // tq),
        in_specs=[pl.BlockSpec((gw, tq), lambda g, i: (g, i)),
                  pl.BlockSpec((1, n, HEAD_DIM), lambda g, i: (g, 0, 0)),
                  pl.BlockSpec((V_ROWS, n), lambda g, i: (g, 0)),
                  pl.BlockSpec((1, n_ctx, HEAD_DIM), lambda g, i: (g, 0, 0)),
                  pl.BlockSpec((V_ROWS, n_ctx), lambda g, i: (g, 0))],
        out_specs=pl.BlockSpec((gw, tq), lambda g, i: (g, i)),
        compiler_params=_cparams(("parallel", "parallel")),
    )
    acc = pltpu.VMEM((Q_PER_KV, V_ROWS, tq), F32)
    bounded = pl.pallas_call(functools.partial(_attn_bounded_kernel, tk=tk),
                             scratch_shapes=[acc, pltpu.VMEM((tk, tq), F32)], name="attn_bounded", **common)
    online = pl.pallas_call(functools.partial(_attn_online_kernel, tk=tk),
                            scratch_shapes=[pltpu.VMEM((Q_PER_KV, 1, tq), F32), acc],
                            name="attn_online", **common)
    return lax.cond(score_bound <= EXP2_SAFE_BOUND, bounded, online, qT, k, vT, kc, vcT)


def _four_a_kernel(f_ref, cs_ref, base_ref, twr_ref, twi_ref, y_ref, *, nb):
    n1 = f_ref.shape[0]
    cs = cs_ref[...].astype(BF16)
    base = base_ref[...].astype(BF16)
    for u in range(nb):
        xb = f_ref[:, u * FOURIER_WIDTH:(u + 1) * FOURIER_WIDTH]
        ab = _dot(xb, cs)
        stacked = jnp.concatenate([ab[:, :FOURIER_WIDTH], ab[:, FOURIER_WIDTH:]], axis=0)
        p = _dot(base, stacked.astype(BF16))
        pr, pi = p[:n1], p[n1:]
        tr = twr_ref[0, :, u:u + 1]
        ti = twi_ref[0, :, u:u + 1]
        y_ref[u, 0] = tr * pr - ti * pi
        y_ref[u, 1] = tr * pi + ti * pr


def _four_b_kernel(y_ref, wb_ref, o_ref):
    n2, _, kb, w = y_ref.shape
    y = y_ref[...].reshape(n2 * 2 * kb, w).astype(BF16)
    o_ref[...] = _dot(wb_ref[...].astype(BF16), y).reshape(n2, kb, w)


def _fourier(f):
    n = f.shape[0]
    n2 = LANES
    n1 = n // n2
    nb = kb = 8
    cs, base, twr, twi, wb = (jnp.asarray(t) for t in _fourier_tables(n, kb))
    f2d = f.reshape(n1, n2 * FOURIER_WIDTH)
    y = pl.pallas_call(
        functools.partial(_four_a_kernel, nb=nb),
        out_shape=jax.ShapeDtypeStruct((n2, 2, n1, FOURIER_WIDTH), F32),
        grid=(n2 // nb,),
        in_specs=[pl.BlockSpec((n1, nb * FOURIER_WIDTH), lambda s: (0, s)),
                  pl.BlockSpec(cs.shape, lambda s: (0, 0)),
                  pl.BlockSpec(base.shape, lambda s: (0, 0)),
                  pl.BlockSpec((1, n1, nb), lambda s: (s, 0, 0)),
                  pl.BlockSpec((1, n1, nb), lambda s: (s, 0, 0))],
        out_specs=pl.BlockSpec((nb, 2, n1, FOURIER_WIDTH), lambda s: (s, 0, 0, 0)),
        compiler_params=_cparams(("parallel",)),
        name="four_a",
    )(f2d, cs, base, twr, twi)
    out = pl.pallas_call(
        _four_b_kernel,
        out_shape=jax.ShapeDtypeStruct((n2, n1, FOURIER_WIDTH), F32),
        grid=(n1 // kb,),
        in_specs=[pl.BlockSpec((n2, 2, kb, FOURIER_WIDTH), lambda s: (0, 0, s, 0)),
                  pl.BlockSpec(wb.shape, lambda s: (0, 0))],
        out_specs=pl.BlockSpec((n2, kb, FOURIER_WIDTH), lambda s: (0, s, 0)),
        compiler_params=_cparams(("parallel",)),
        name="four_b",
    )(y, wb)
    return out.reshape(n, FOURIER_WIDTH)


def _outproj_kernel(aT_ref, four_ref, x_ref, wa_ref, wf_ref, nw_ref, mod_ref, x1_ref, h_ref):
    _, _, gt1, sh2, sc2, _ = _mod_rows(mod_ref, 0)
    y = lax.dot_general(aT_ref[...], wa_ref[...], (((0,), (0,)), ((), ())), preferred_element_type=F32)
    y = y + _dot(four_ref[...].astype(BF16), wf_ref[...])
    x1 = x_ref[...] + gt1 * _rms(y, nw_ref[0, 1:2, :])
    x1_ref[...] = x1
    h_ref[...] = (_rms(x1, nw_ref[0, 2:3, :]) * (1.0 + sc2) + sh2).astype(BF16)


def _outproj(attnT, four, x2d, wa, wf, nw, mod, *, tm):
    n = x2d.shape[0]
    const = lambda i: (0, 0)
    return pl.pallas_call(
        _outproj_kernel,
        out_shape=(jax.ShapeDtypeStruct((n, D_MODEL), F32), jax.ShapeDtypeStruct((n, D_MODEL), BF16)),
        grid=(n // tm,),
        in_specs=[pl.BlockSpec((ATTN_WIDTH, tm), lambda i: (0, i)),
                  pl.BlockSpec((tm, FOURIER_WIDTH), lambda i: (i, 0)),
                  pl.BlockSpec((tm, D_MODEL), lambda i: (i, 0)),
                  pl.BlockSpec((ATTN_WIDTH, D_MODEL), const),
                  pl.BlockSpec((FOURIER_WIDTH, D_MODEL), const),
                  pl.BlockSpec((1, 4, D_MODEL), lambda i: (0, 0, 0)),
                  pl.BlockSpec((1, 8, 6 * D_MODEL), lambda i: (0, 0, 0))],
        out_specs=(pl.BlockSpec((tm, D_MODEL), lambda i: (i, 0)),
                   pl.BlockSpec((tm, D_MODEL), lambda i: (i, 0))),
        compiler_params=_cparams(("parallel",)),
        name="outproj",
    )(attnT, four, x2d, wa, wf, nw, mod)


def _ffn_kernel(h_ref, x_ref, wg_ref, wu_ref, wd_ref, nw_ref, mod_ref, nw1_ref, mod1_ref,
                x2_ref, h3_ref, acc_ref):
    c = pl.program_id(1)
    h = h_ref[...]
    g = _dot(h, wg_ref[...])
    u = _dot(h, wu_ref[...])
    part = _dot((g * jax.nn.sigmoid(g) * u).astype(BF16), wd_ref[...])

    @pl.when(c == 0)
    def _():
        acc_ref[...] = part

    @pl.when(c > 0)
    def _():
        acc_ref[...] += part

    @pl.when(c == pl.num_programs(1) - 1)
    def _():
        gt2 = _mod_rows(mod_ref, 0)[5]
        sh, sc = _mod_rows(mod1_ref, 0)[:2]
        x2 = x_ref[...] + gt2 * _rms(acc_ref[...], nw_ref[0, 3:4, :])
        x2_ref[...] = x2
        h3_ref[...] = (_rms(x2, nw1_ref[0, 0:1, :]) * (1.0 + sc) + sh).astype(BF16)


def _ffn(h, x1, wg, wu, wd, nw, mod, nw1, mod1, *, tm, tf):
    n = h.shape[0]
    nwspec = pl.BlockSpec((1, 4, D_MODEL), lambda i, c: (0, 0, 0))
    modspec = pl.BlockSpec((1, 8, 6 * D_MODEL), lambda i, c: (0, 0, 0))
    return pl.pallas_call(
        _ffn_kernel,
        out_shape=(jax.ShapeDtypeStruct((n, D_MODEL), F32), jax.ShapeDtypeStruct((n, D_MODEL), BF16)),
        grid=(n // tm, D_FF // tf),
        in_specs=[pl.BlockSpec((tm, D_MODEL), lambda i, c: (i, 0)),
                  pl.BlockSpec((tm, D_MODEL), lambda i, c: (i, 0)),
                  pl.BlockSpec((D_MODEL, tf), lambda i, c: (0, c)),
                  pl.BlockSpec((D_MODEL, tf), lambda i, c: (0, c)),
                  pl.BlockSpec((tf, D_MODEL), lambda i, c: (c, 0)),
                  nwspec, modspec, nwspec, modspec],
        out_specs=(pl.BlockSpec((tm, D_MODEL), lambda i, c: (i, 0)),
                   pl.BlockSpec((tm, D_MODEL), lambda i, c: (i, 0))),
        scratch_shapes=[pltpu.VMEM((tm, D_MODEL), F32)],
        compiler_params=_cparams(("parallel", "arbitrary")),
        name="ffn",
    )(h, x1, wg, wu, wd, nw, mod, nw1, mod1)


EDGE_ROWS = 16


def _convedge_kernel(hf_ref, hl_ref, wc_ref, wu_ref, vf_ref, vl_ref):
    nt = hf_ref.shape[0]
    for h_ref, v_ref in ((hf_ref, vf_ref), (hl_ref, vl_ref)):
        h = h_ref[...].reshape(nt * EDGE_ROWS, D_MODEL)
        v = _dot(h, wc_ref[...]) * _dot(h, wu_ref[...])
        v_ref[...] = v.reshape(nt, EDGE_ROWS, D_MODEL)


def _convedge(h3, w_in_bf, *, tm):
    n = h3.shape[0]
    nt = n // tm
    h3t = h3.reshape(nt, tm, D_MODEL)
    last = tm // EDGE_ROWS - 1
    shape = jax.ShapeDtypeStruct((nt, EDGE_ROWS, D_MODEL), F32)
    return pl.pallas_call(
        _convedge_kernel,
        out_shape=(shape, shape),
        grid=(1,),
        in_specs=[pl.BlockSpec((nt, EDGE_ROWS, D_MODEL), lambda i: (0, 0, 0)),
                  pl.BlockSpec((nt, EDGE_ROWS, D_MODEL), lambda i: (0, last, 0)),
                  pl.BlockSpec((D_MODEL, D_MODEL), lambda i: (0, 1)),
                  pl.BlockSpec((D_MODEL, D_MODEL), lambda i: (0, 2))],
        out_specs=(pl.BlockSpec((nt, EDGE_ROWS, D_MODEL), lambda i: (0, 0, 0)),
                   pl.BlockSpec((nt, EDGE_ROWS, D_MODEL), lambda i: (0, 0, 0))),
        compiler_params=_cparams(("arbitrary",)),
        name="convedge",
    )(h3t, h3t, w_in_bf, w_in_bf)


def _convmix_kernel(h_ref, x_ref, vl_ref, vf_ref, win_ref, cw_ref, wout_ref, nw_ref, mod_ref, r_ref,
                    x3_ref, h4_ref, comb_ref, sel_ref, rank_ref):
    i = pl.program_id(0)
    tm = h_ref.shape[0]
    _, _, gt1, sh2, sc2, _ = _mod_rows(mod_ref, 0)
    z = _dot(h_ref[...], win_ref[...])
    b = z[:, :D_MODEL]
    v = z[:, D_MODEL:2 * D_MODEL] * z[:, 2 * D_MODEL:]
    has_prev = (i > 0).astype(F32)
    has_next = (i < pl.num_programs(0) - 1).astype(F32)
    prev_row = vl_ref[0, EDGE_ROWS - 1:EDGE_ROWS, :] * has_prev
    next_row = vf_ref[0, 0:1, :] * has_next
    rows = lax.broadcasted_iota(jnp.int32, (tm, 1), 0)
    v_dn = jnp.where(rows == 0, prev_row, pltpu.roll(v, 1, axis=0))
    v_up = jnp.where(rows == tm - 1, next_row, pltpu.roll(v, tm - 1, axis=0))
    conv = v_dn * cw_ref[0, 0:1, :] + v * cw_ref[0, 1:2, :] + v_up * cw_ref[0, 2:3, :]
    y = _dot((b * conv).astype(BF16), wout_ref[...])
    x3 = x_ref[...] + gt1 * _rms(y, nw_ref[0, 1:2, :])
    x3_ref[...] = x3
    h4 = _rms(x3, nw_ref[0, 2:3, :]) * (1.0 + sc2) + sh2
    h4_ref[...] = h4.astype(BF16)
    hhi, hlo = _split_bf16(h4)
    rhi, rlo = _split_bf16(r_ref[...])
    logits = _dot(hhi, rhi) + (_dot(hlo, rhi) + _dot(hhi, rlo))
    lane = lax.broadcasted_iota(jnp.int32, logits.shape, 1)
    logits = jnp.where(lane < N_EXPERTS, logits, -jnp.inf)
    e = jnp.exp(logits - jnp.max(logits, axis=-1, keepdims=True))
    probs = e / jnp.sum(e, axis=-1, keepdims=True)
    v1 = jnp.max(probs, axis=-1, keepdims=True)
    i1 = jnp.min(jnp.where(probs == v1, lane, LANES), axis=-1, keepdims=True)
    rest = jnp.where(lane == i1, -1.0, probs)
    v2 = jnp.max(rest, axis=-1, keepdims=True)
    i2 = jnp.min(jnp.where(rest == v2, lane, LANES), axis=-1, keepdims=True)
    tot = v1 + v2
    comb_ref[...] = jnp.where(lane == i1, v1 / tot, 0.0) + jnp.where(lane == i2, v2 / tot, 0.0)
    sel = jnp.where(jnp.logical_or(lane == i1, lane == i2), 1.0, 0.0)
    sel_ref[...] = sel
    earlier = lax.broadcasted_iota(jnp.int32, (tm, tm), 1) < lax.broadcasted_iota(jnp.int32, (tm, tm), 0)
    rank_ref[...] = _dot(jnp.where(earlier, 1.0, 0.0).astype(BF16), sel.astype(BF16))


def _convmix(h3, x2, v_first, v_last, w_in_bf, conv_w, w_out_bf, nw, mod, router_pad, *, tm):
    n = h3.shape[0]
    nt = n // tm
    const = lambda i: (0, 0)
    lanes_shape = jax.ShapeDtypeStruct((n, LANES), F32)
    lanes_spec = pl.BlockSpec((tm, LANES), lambda i: (i, 0))
    return pl.pallas_call(
        _convmix_kernel,
        out_shape=(jax.ShapeDtypeStruct((n, D_MODEL), F32), jax.ShapeDtypeStruct((n, D_MODEL), BF16),
                   lanes_shape, lanes_shape, lanes_shape),
        grid=(nt,),
        in_specs=[pl.BlockSpec((tm, D_MODEL), lambda i: (i, 0)),
                  pl.BlockSpec((tm, D_MODEL), lambda i: (i, 0)),
                  pl.BlockSpec((1, EDGE_ROWS, D_MODEL), lambda i: (jnp.maximum(i - 1, 0), 0, 0)),
                  pl.BlockSpec((1, EDGE_ROWS, D_MODEL), lambda i: (jnp.minimum(i + 1, nt - 1), 0, 0)),
                  pl.BlockSpec((D_MODEL, 3 * D_MODEL), const),
                  pl.BlockSpec((1, 3, D_MODEL), lambda i: (0, 0, 0)),
                  pl.BlockSpec((D_MODEL, D_MODEL), const),
                  pl.BlockSpec((1, 4, D_MODEL), lambda i: (0, 0, 0)),
                  pl.BlockSpec((1, 8, 6 * D_MODEL), lambda i: (0, 0, 0)),
                  pl.BlockSpec((D_MODEL, LANES), const)],
        out_specs=(pl.BlockSpec((tm, D_MODEL), lambda i: (i, 0)),
                   pl.BlockSpec((tm, D_MODEL), lambda i: (i, 0)),
                   lanes_spec, lanes_spec, lanes_spec),
        compiler_params=_cparams(("parallel",)),
        name="convmix",
    )(h3, x2, v_last, v_first, w_in_bf, conv_w, w_out_bf, nw, mod, router_pad)


ROW_TILE = 512
FLAG_ACTIVE, FLAG_FIRST, FLAG_LAST = 1, 2, 4


def _count_le(sorted_vals, x):
    return jnp.sum((sorted_vals[None, :] <= x[:, None]).astype(jnp.int32), axis=1)


def _ragged_items(n_per_group, lo_per_group, n_items):
    off_end = jnp.cumsum(n_per_group)
    off = off_end - n_per_group
    total = off_end[-1]
    k = jnp.minimum(jnp.arange(n_items, dtype=jnp.int32), total - 1)
    grp = jnp.minimum(_count_le(off_end, k), n_per_group.shape[0] - 1)
    member = lo_per_group[grp] + k - off[grp]
    active = jnp.arange(n_items, dtype=jnp.int32) < total
    return grp, member, active


def _flags(active, key):
    prev_differs = jnp.concatenate([jnp.array([True]), key[1:] != key[:-1]])
    next_active = jnp.concatenate([active[1:], jnp.array([False])])
    next_differs = jnp.concatenate([key[1:] != key[:-1], jnp.array([True])])
    last = jnp.logical_or(next_differs, jnp.logical_not(next_active))
    a = active.astype(jnp.int32)
    return a * (FLAG_ACTIVE + FLAG_FIRST * prev_differs.astype(jnp.int32) + FLAG_LAST * last.astype(jnp.int32))


def _route(sel, lrank, n_row_tiles):
    n = sel.shape[0]
    n_tok_tiles = n // ROW_TILE
    i32 = jnp.int32
    sel8 = sel[:, :N_EXPERTS].astype(i32)
    cnt_tile = jnp.sum(sel8.reshape(n_tok_tiles, ROW_TILE, N_EXPERTS), axis=1)
    cum_end = jnp.cumsum(cnt_tile, axis=0)
    cum_beg = cum_end - cnt_tile
    cnt = cum_end[-1]
    cnt_pad = (cnt + ROW_TILE - 1) // ROW_TILE * ROW_TILE
    grp_end = jnp.cumsum(cnt_pad)
    start = grp_end - cnt_pad
    rank = lrank[:, :N_EXPERTS].astype(i32) + jnp.repeat(cum_beg, ROW_TILE, axis=0)
    pos = jnp.where(sel8 > 0, start[None, :] + rank, -1).astype(F32)
    r = jnp.arange(n_row_tiles, dtype=i32)
    base = r * ROW_TILE
    tile_valid = base < grp_end[-1]
    n_valid = grp_end[-1] // ROW_TILE
    tile_exp = jnp.minimum(_count_le(grp_end, base), N_EXPERTS - 1)
    last_exp = tile_exp[n_valid - 1]
    tile_exp = jnp.where(tile_valid, tile_exp, last_exp)
    tile_src = jnp.where(tile_valid, r, n_valid - 1)
    tile_first = jnp.logical_and(tile_valid, base == start[tile_exp])
    rho0 = base - start[tile_exp]
    rho1 = jnp.minimum(rho0 + ROW_TILE, cnt[tile_exp])
    ends = cum_end.T[tile_exp]
    c_lo = jnp.sum((ends <= rho0[:, None]).astype(i32), axis=1)
    c_hi = jnp.sum((ends <= (rho1 - 1)[:, None]).astype(i32), axis=1)
    n_chunks = jnp.where(tile_valid, c_hi - c_lo + 1, 0)
    n_items = n_row_tiles + N_EXPERTS * n_tok_tiles
    g_tile, g_chunk, g_active = _ragged_items(n_chunks, c_lo, n_items)
    gather_items = (g_tile, g_chunk, tile_exp[g_tile], _flags(g_active, g_tile))
    row_lo = (start[None, :] + cum_beg).reshape(-1)
    row_hi = (start[None, :] + cum_end - 1).reshape(-1)
    n_rt = jnp.where(cnt_tile.reshape(-1) > 0, row_hi // ROW_TILE - row_lo // ROW_TILE + 1, 0)
    c_pair, c_row, c_active = _ragged_items(n_rt, row_lo // ROW_TILE, n_items)
    c_tok = c_pair // N_EXPERTS
    combine_items = (c_tok, c_row, c_pair % N_EXPERTS, _flags(c_active, c_tok))
    tiles = (tile_exp, tile_src, tile_valid.astype(i32), tile_first.astype(i32))
    return pos, gather_items, tiles, combine_items


def _gather_kernel(it_tile, it_chunk, it_exp, it_flag, pos_ref, h_ref, w_ref, xs_ref, ws_ref, accx, accw):
    k = pl.program_id(0)
    flag = it_flag[k]

    @pl.when(flag & FLAG_ACTIVE != 0)
    def _():
        rows = it_tile[k] * ROW_TILE + lax.broadcasted_iota(jnp.int32, (ROW_TILE, 1), 0)
        onehot = jnp.where(pos_ref[0] == rows.astype(F32), 1.0, 0.0).astype(BF16)
        px = _dot(onehot, h_ref[...])
        pw = _dot(onehot, w_ref[...])

        @pl.when(flag & FLAG_FIRST != 0)
        def _():
            accx[...] = px
            accw[...] = pw

        @pl.when(flag & FLAG_FIRST == 0)
        def _():
            accx[...] += px
            accw[...] += pw

        @pl.when(flag & FLAG_LAST != 0)
        def _():
            xs_ref[...] = accx[...].astype(BF16)
            ws_ref[...] = accw[...]


def _gather_rows(items, pos_t, h4, w_split, n_rows):
    n_items = items[0].shape[0]
    grid_spec = pltpu.PrefetchScalarGridSpec(
        num_scalar_prefetch=4, grid=(n_items,),
        in_specs=[pl.BlockSpec((1, 1, ROW_TILE), lambda k, t, c, e, f: (e[k], 0, c[k])),
                  pl.BlockSpec((ROW_TILE, D_MODEL), lambda k, t, c, e, f: (c[k], 0)),
                  pl.BlockSpec((ROW_TILE, LANES), lambda k, t, c, e, f: (c[k], 0))],
        out_specs=(pl.BlockSpec((ROW_TILE, D_MODEL), lambda k, t, c, e, f: (t[k], 0)),
                   pl.BlockSpec((ROW_TILE, LANES), lambda k, t, c, e, f: (t[k], 0))),
        scratch_shapes=[pltpu.VMEM((ROW_TILE, D_MODEL), F32), pltpu.VMEM((ROW_TILE, LANES), F32)])
    return pl.pallas_call(
        _gather_kernel, grid_spec=grid_spec,
        out_shape=(jax.ShapeDtypeStruct((n_rows, D_MODEL), BF16), jax.ShapeDtypeStruct((n_rows, LANES), F32)),
        compiler_params=_cparams(("arbitrary",)),
        name="moe_gather",
    )(*items, pos_t, h4, w_split)


def _experts_kernel(t_exp, t_src, t_valid, t_first, xs_ref, ws_ref, wg_ref, wu_ref, wd_ref, ys_ref,
                    wg_c, wu_c, wd_c, acc_ref):
    r = pl.program_id(0)
    c = pl.program_id(1)
    last = c == pl.num_programs(1) - 1

    @pl.when(t_first[r] != 0)
    def _():
        wg_c[c] = wg_ref[0, 0].astype(BF16)
        wu_c[c] = wu_ref[0, 0].astype(BF16)
        wd_c[c] = wd_ref[0, 0].astype(BF16)

    @pl.when(t_valid[r] != 0)
    def _():
        e = t_exp[r]
        ws = ws_ref[...]
        lane = lax.broadcasted_iota(jnp.int32, ws.shape, 1)
        mine = jnp.logical_or(lane == e, lane == e + N_EXPERTS)
        w = jnp.sum(jnp.where(mine, ws, 0.0), axis=-1, keepdims=True)
        x = xs_ref[...]
        g = _dot(x, wg_c[c])
        u = _dot(x, wu_c[c])
        part = _dot((g * jax.nn.sigmoid(g) * u * w).astype(BF16), wd_c[c])

        @pl.when(c == 0)
        def _():
            acc_ref[...] = part

        @pl.when(c > 0)
        def _():
            acc_ref[...] += part

        @pl.when(last)
        def _():
            ys_ref[...] = acc_ref[...].astype(BF16)

    @pl.when(jnp.logical_and(t_valid[r] == 0, last))
    def _():
        ys_ref[...] = jnp.zeros(ys_ref.shape, BF16)


def _experts(tiles, xs, ws, wg, wu, wd, *, tf):
    n_rows = xs.shape[0]
    n_ch = D_EXPERT // tf

    def w_in(r, c, e, s, v, f):
        return (0, e[r], 0, jnp.where(f[r] != 0, c, n_ch - 1))

    def w_dn(r, c, e, s, v, f):
        return (0, e[r], jnp.where(f[r] != 0, c, n_ch - 1), 0)

    grid_spec = pltpu.PrefetchScalarGridSpec(
        num_scalar_prefetch=4, grid=(n_rows // ROW_TILE, n_ch),
        in_specs=[pl.BlockSpec((ROW_TILE, D_MODEL), lambda r, c, e, s, v, f: (s[r], 0)),
                  pl.BlockSpec((ROW_TILE, LANES), lambda r, c, e, s, v, f: (s[r], 0)),
                  pl.BlockSpec((1, 1, D_MODEL, tf), w_in),
                  pl.BlockSpec((1, 1, D_MODEL, tf), w_in),
                  pl.BlockSpec((1, 1, tf, D_MODEL), w_dn)],
        out_specs=pl.BlockSpec((ROW_TILE, D_MODEL), lambda r, c, e, s, v, f: (r, 0)),
        scratch_shapes=[pltpu.VMEM((n_ch, D_MODEL, tf), BF16), pltpu.VMEM((n_ch, D_MODEL, tf), BF16),
                        pltpu.VMEM((n_ch, tf, D_MODEL), BF16), pltpu.VMEM((ROW_TILE, D_MODEL), F32)])
    return pl.pallas_call(
        _experts_kernel, grid_spec=grid_spec,
        out_shape=jax.ShapeDtypeStruct((n_rows, D_MODEL), BF16),
        compiler_params=_cparams(("arbitrary", "arbitrary")),
        name="moe_experts",
    )(*tiles, xs, ws, wg, wu, wd)


def _combine_kernel(it_tok, it_row, it_exp, it_flag, pos_ref, ys_ref, x_ref, nw_ref, mod_ref, o_ref, acc_ref):
    k = pl.program_id(0)
    flag = it_flag[k]

    @pl.when(flag & FLAG_ACTIVE != 0)
    def _():
        pos = pos_ref[...]
        lane = lax.broadcasted_iota(jnp.int32, pos.shape, 1)
        mine = jnp.sum(jnp.where(lane == it_exp[k], pos, 0.0), axis=-1, keepdims=True)
        cols = it_row[k] * ROW_TILE + lax.broadcasted_iota(jnp.int32, (1, ROW_TILE), 1)
        onehot = jnp.where(mine == cols.astype(F32), 1.0, 0.0).astype(BF16)
        part = _dot(onehot, ys_ref[...])

        @pl.when(flag & FLAG_FIRST != 0)
        def _():
            acc_ref[...] = part

        @pl.when(flag & FLAG_FIRST == 0)
        def _():
            acc_ref[...] += part

        @pl.when(flag & FLAG_LAST != 0)
        def _():
            gt2 = _mod_rows(mod_ref, 0)[5]
            o_ref[...] = x_ref[...] + gt2 * _rms(acc_ref[...], nw_ref[0, 3:4, :])


def _combine(items, pos_pad, ys, x3, nw, mod):
    n = x3.shape[0]
    n_items = items[0].shape[0]
    grid_spec = pltpu.PrefetchScalarGridSpec(
        num_scalar_prefetch=4, grid=(n_items,),
        in_specs=[pl.BlockSpec((ROW_TILE, LANES), lambda k, t, r, e, f: (t[k], 0)),
                  pl.BlockSpec((ROW_TILE, D_MODEL), lambda k, t, r, e, f: (r[k], 0)),
                  pl.BlockSpec((ROW_TILE, D_MODEL), lambda k, t, r, e, f: (t[k], 0)),
                  pl.BlockSpec((1, 4, D_MODEL), lambda k, t, r, e, f: (0, 0, 0)),
                  pl.BlockSpec((1, 8, 6 * D_MODEL), lambda k, t, r, e, f: (0, 0, 0))],
        out_specs=pl.BlockSpec((ROW_TILE, D_MODEL), lambda k, t, r, e, f: (t[k], 0)),
        scratch_shapes=[pltpu.VMEM((ROW_TILE, D_MODEL), F32)])
    return pl.pallas_call(
        _combine_kernel, grid_spec=grid_spec,
        out_shape=jax.ShapeDtypeStruct((n, D_MODEL), F32),
        compiler_params=_cparams(("arbitrary",)),
        name="moe_combine",
    )(*items, pos_pad, ys, x3, nw, mod)


def _moe(h4, comb, sel, lrank, x3, wg, wu, wd, nw, mod):
    n = h4.shape[0]
    n_rows = 2 * n + N_EXPERTS * ROW_TILE
    pos, gather_items, tiles, combine_items = _route(sel, lrank, n_rows // ROW_TILE)
    pos_t = pos.T.reshape(N_EXPERTS, 1, n)
    pos_pad = jnp.pad(pos, ((0, 0), (0, LANES - N_EXPERTS)), constant_values=-1.0)
    comb8 = comb[:, :N_EXPERTS]
    hi = comb8.astype(BF16)
    lo = (comb8 - hi.astype(F32)).astype(BF16)
    w_split = jnp.pad(jnp.concatenate([hi, lo], axis=1), ((0, 0), (0, LANES - 2 * N_EXPERTS)))
    xs, ws = _gather_rows(gather_items, pos_t, h4, w_split, n_rows)
    ys = _experts(tiles, xs, ws, wg, wu, wd, tf=512)
    return _combine(combine_items, pos_pad, ys, x3, nw, mod)


def kernel(x, c, ctx, c_ctx, ada_w, ada_b, norm_w, e_w_in, e_q_gain, e_k_gain, e_w_out, e_ffn_gate,
           e_ffn_up, e_ffn_down, o_w_in, o_conv_w, o_w_out, o_router, o_exp_gate, o_exp_up, o_exp_down):
    assert x.shape[0] == 1 and x.shape[2] == D_MODEL and ada_w.shape[0] == 2
    n = x.shape[1]
    x2d = x[0]
    ctx2d = ctx[0]
    mod = _ada(c, c_ctx, ada_w, ada_b)
    mod0, mod1 = mod[0:1], mod[1:2]
    nw0, nw1 = norm_w[0:1], norm_w[1:2]

    w_in = e_w_in[0].astype(BF16)
    scale = HEAD_DIM ** -0.5 * np.log2(np.e)
    gain = jnp.concatenate([jnp.tile(e_q_gain[0], N_Q_HEADS) * scale,
                            jnp.tile(e_k_gain[0], N_KV_HEADS)])[None, :]
    score_bound = 1.02 * HEAD_DIM * scale * jnp.max(jnp.abs(e_q_gain[0])) * jnp.max(jnp.abs(e_k_gain[0]))
    qT, k, vT, f = _evenproj(x2d, nw0, mod0, w_in, gain, latent=True, tm=512)
    kc, vcT = _evenproj(ctx2d, nw0, mod0, w_in, gain, latent=False, tm=ctx2d.shape[0])
    attnT = _attention(qT, k, vT, kc, vcT, score_bound, tq=512, tk=512)
    four = _fourier(f)
    w_out = e_w_out[0].astype(BF16)
    x1, h2 = _outproj(attnT, four, x2d, w_out[:ATTN_WIDTH], w_out[ATTN_WIDTH:], nw0, mod0, tm=512)
    x2, h3 = _ffn(h2, x1, e_ffn_gate[0].astype(BF16), e_ffn_up[0].astype(BF16),
                  e_ffn_down[0].astype(BF16), nw0, mod0, nw1, mod1, tm=1024, tf=1408)

    ow_in = o_w_in[0].astype(BF16)
    v_first, v_last = _convedge(h3, ow_in, tm=ROW_TILE)
    router_pad = jnp.pad(o_router[0], ((0, 0), (0, LANES - N_EXPERTS)))
    x3, h4, comb, sel, lrank = _convmix(h3, x2, v_first, v_last, ow_in, o_conv_w, o_w_out[0].astype(BF16),
                                        nw1, mod1, router_pad, tm=ROW_TILE)
    out = _moe(h4, comb, sel, lrank, x3, o_exp_gate, o_exp_up, o_exp_down, nw1, mod1)
    return out[None]
```

```python
import functools

import numpy as np
import jax
import jax.numpy as jnp
from jax import lax
from jax.experimental import pallas as pl
from jax.experimental.pallas import tpu as pltpu

D_MODEL = 1024
GRID_W = 64
HEAD_DIM = 64
N_Q_HEADS = 12
N_KV_HEADS = 4
Q_PER_KV = N_Q_HEADS // N_KV_HEADS
ATTN_WIDTH = N_Q_HEADS * HEAD_DIM
KV_WIDTH = N_KV_HEADS * HEAD_DIM
QK_WIDTH = ATTN_WIDTH + KV_WIDTH
N_FOURIER_GROUPS = 4
FOURIER_GROUP_DIM = 64
FOURIER_WIDTH = N_FOURIER_GROUPS * FOURIER_GROUP_DIM
EVEN_IN_WIDTH = ATTN_WIDTH + 2 * KV_WIDTH + FOURIER_WIDTH
D_FF = 2816
N_EXPERTS = 8
D_EXPERT = 3584
ONES_ROWS = 16
V_ROWS = HEAD_DIM + ONES_ROWS
EXP2_SAFE_BOUND = 60.0
ROPE_THETA = 10000.0
ROPE_HALF = HEAD_DIM // 4
NORM_EPS = 1e-6

LANES = 128
VMEM_LIMIT = 56 * 1024 * 1024

BF16 = jnp.bfloat16
QK_DTYPE = jnp.float8_e4m3fn
F32 = jnp.float32


def _cparams(semantics, vmem=VMEM_LIMIT):
    return pltpu.CompilerParams(dimension_semantics=semantics, vmem_limit_bytes=vmem)


def _dot(a, b):
    return jnp.dot(a, b, preferred_element_type=F32)


def _split_bf16(x):
    hi = x.astype(BF16)
    lo = (x - hi.astype(F32)).astype(BF16)
    return hi, lo


def _rms(x, g):
    return x * lax.rsqrt(jnp.mean(x * x, axis=-1, keepdims=True) + NORM_EPS) * g


def _mod_rows(mod_ref, row):
    return [mod_ref[0, row:row + 1, i * D_MODEL:(i + 1) * D_MODEL] for i in range(6)]


@functools.lru_cache(maxsize=None)
def _rope_tables(n_tokens):
    t = np.arange(n_tokens)
    row = (t // GRID_W).astype(np.float64)
    col = (t % GRID_W).astype(np.float64)
    inv = ROPE_THETA ** (-np.arange(ROPE_HALF, dtype=np.float64) / ROPE_HALF)
    ar, ac = row[:, None] * inv, col[:, None] * inv
    cos = np.concatenate([np.cos(ar), np.cos(ar), np.cos(ac), np.cos(ac)], axis=-1)
    sin = np.concatenate([-np.sin(ar), np.sin(ar), -np.sin(ac), np.sin(ac)], axis=-1)
    reps = LANES // HEAD_DIM
    return (np.tile(cos, (1, reps)).astype(np.float32), np.tile(sin, (1, reps)).astype(np.float32))


@functools.lru_cache(maxsize=None)
def _head_matrices():
    head = np.arange(QK_WIDTH) // HEAD_DIM
    red = (head[:, None] == np.arange(LANES)[None, :]).astype(np.float32) / HEAD_DIM
    exp = (np.arange(LANES)[:, None] == head[None, :]).astype(np.float32)
    return red, exp


@functools.lru_cache(maxsize=None)
def _fourier_tables(n_tokens, kb):
    n2 = LANES
    n1 = n_tokens // n2
    c = np.arange(FOURIER_GROUP_DIM)
    ang = 2 * np.pi * np.outer(c, c) / FOURIER_GROUP_DIM
    eye = np.eye(N_FOURIER_GROUPS)
    cs = np.concatenate([np.kron(eye, np.cos(ang)), np.kron(eye, np.sin(ang))], axis=1)
    k1 = np.arange(n1)
    th = 2 * np.pi * np.outer(k1, k1) / n1
    cr, ci = np.cos(th), -np.sin(th)
    base = np.block([[cr, ci], [ci, -cr]])
    psi = 2 * np.pi * np.outer(k1, np.arange(n2)) / n_tokens
    twr = np.cos(psi).reshape(n1, n2 // kb, kb).transpose(1, 0, 2)
    twi = (-np.sin(psi)).reshape(n1, n2 // kb, kb).transpose(1, 0, 2)
    k2 = np.arange(n2)
    ph = 2 * np.pi * np.outer(k2, k2) / n2
    fr, fi = np.cos(ph), -np.sin(ph)
    scale = 1.0 / np.sqrt(n_tokens * FOURIER_GROUP_DIM)
    m3 = np.stack([fr, -fi], axis=-1) * scale
    wb = np.einsum('knr,uv->kunrv', m3, np.eye(kb)).reshape(n2 * kb, n2 * 2 * kb)
    f32 = np.float32
    return cs.astype(f32), base.astype(f32), twr.astype(f32), twi.astype(f32), wb.astype(f32)


def _ada_kernel(cb_ref, w_ref, b_ref, o_ref):
    tn = o_ref.shape[-1]
    o_ref[...] = jnp.zeros(o_ref.shape, F32)
    for r in range(2):
        cb = cb_ref[r]
        s = cb * jax.nn.sigmoid(cb)
        for j in range(tn // LANES):
            sl = slice(j * LANES, (j + 1) * LANES)
            col = jnp.sum(s * w_ref[0, :, sl], axis=0, keepdims=True)
            o_ref[0, r:r + 1, sl] = col + b_ref[0, :, sl]


def _ada(c, c_ctx, ada_w, ada_b):
    depth = ada_w.shape[0]
    n = ada_w.shape[-1]
    tn = 1536
    cb = jnp.stack([jnp.broadcast_to(c[0][:, None], (D_MODEL, LANES)),
                    jnp.broadcast_to(c_ctx[:, None], (D_MODEL, LANES))])
    return pl.pallas_call(
        _ada_kernel,
        out_shape=jax.ShapeDtypeStruct((depth, 8, n), F32),
        grid=(depth, n // tn),
        in_specs=[pl.BlockSpec((2, D_MODEL, LANES), lambda i, j: (0, 0, 0)),
                  pl.BlockSpec((1, D_MODEL, tn), lambda i, j: (i, 0, j)),
                  pl.BlockSpec((1, 1, tn), lambda i, j: (i, 0, j))],
        out_specs=pl.BlockSpec((1, 8, tn), lambda i, j: (i, 0, j)),
        compiler_params=_cparams(("parallel", "parallel")),
        name="ada",
    )(cb, ada_w, ada_b[:, None, :])


def _evenproj_kernel(*refs, row, latent):
    if latent:
        (x_ref, nw_ref, mod_ref, w_ref, gain_ref, red_ref, exp_ref, cos_ref, sin_ref,
         qT_ref, k_ref, vT_ref, f_ref) = refs
    else:
        x_ref, nw_ref, mod_ref, w_ref, gain_ref, red_ref, exp_ref, k_ref, vT_ref = refs
    sh, sc = _mod_rows(mod_ref, row)[:2]
    h = (_rms(x_ref[...], nw_ref[0, 0:1, :]) * (1.0 + sc) + sh).astype(BF16)
    z = _dot(h, w_ref[...])
    zqk = z[:, :QK_WIDTH]
    hi, lo = _split_bf16(zqk * zqk)
    red = red_ref[...].astype(BF16)
    ms = _dot(hi, red) + _dot(lo, red)
    rhi, rlo = _split_bf16(lax.rsqrt(ms + NORM_EPS))
    expm = exp_ref[...].astype(BF16)
    yn = zqk * (_dot(rhi, expm) + _dot(rlo, expm)) * gain_ref[...]
    if latent:
        lane = lax.broadcasted_iota(jnp.int32, (1, LANES), 1)
        first_half = (lane // ROPE_HALF) % 2 == 0
        cos, sin = cos_ref[...], sin_ref[...]
        chunks = []
        for c in range(QK_WIDTH // LANES):
            yc = yn[:, c * LANES:(c + 1) * LANES]
            partner = jnp.where(first_half, pltpu.roll(yc, LANES - ROPE_HALF, axis=1),
                                pltpu.roll(yc, ROPE_HALF, axis=1))
            chunks.append(yc * cos + partner * sin)
        yn = jnp.concatenate(chunks, axis=1)
        qT_ref[...] = yn[:, :ATTN_WIDTH].T.astype(QK_DTYPE)
        f_ref[...] = z[:, QK_WIDTH + KV_WIDTH:].astype(BF16)
    for g in range(N_KV_HEADS):
        k_ref[g] = yn[:, ATTN_WIDTH + g * HEAD_DIM:ATTN_WIDTH + (g + 1) * HEAD_DIM].astype(QK_DTYPE)
    vT = z[:, QK_WIDTH:QK_WIDTH + KV_WIDTH].T.astype(BF16)
    ones = jnp.ones((ONES_ROWS, vT.shape[1]), BF16)
    for g in range(N_KV_HEADS):
        vT_ref[g * V_ROWS:g * V_ROWS + HEAD_DIM, :] = vT[g * HEAD_DIM:(g + 1) * HEAD_DIM]
        vT_ref[g * V_ROWS + HEAD_DIM:(g + 1) * V_ROWS, :] = ones


def _evenproj(x2d, nw, mod, w_bf, gain, *, latent, tm):
    n = x2d.shape[0]
    red, expm = _head_matrices()
    const = lambda i: (0, 0)
    in_specs = [pl.BlockSpec((tm, D_MODEL), lambda i: (i, 0)),
                pl.BlockSpec((1, 4, D_MODEL), lambda i: (0, 0, 0)),
                pl.BlockSpec((1, 8, 6 * D_MODEL), lambda i: (0, 0, 0)),
                pl.BlockSpec((D_MODEL, EVEN_IN_WIDTH), const),
                pl.BlockSpec((1, QK_WIDTH), const),
                pl.BlockSpec((QK_WIDTH, LANES), const),
                pl.BlockSpec((LANES, QK_WIDTH), const)]
    args = [x2d, nw, mod, w_bf, gain, jnp.asarray(red), jnp.asarray(expm)]
    k_shape = jax.ShapeDtypeStruct((N_KV_HEADS, n, HEAD_DIM), QK_DTYPE)
    vT_shape = jax.ShapeDtypeStruct((N_KV_HEADS * V_ROWS, n), BF16)
    k_spec = pl.BlockSpec((N_KV_HEADS, tm, HEAD_DIM), lambda i: (0, i, 0))
    vT_spec = pl.BlockSpec((N_KV_HEADS * V_ROWS, tm), lambda i: (0, i))
    if latent:
        cos, sin = _rope_tables(n)
        in_specs += [pl.BlockSpec((tm, LANES), lambda i: (i, 0))] * 2
        args += [jnp.asarray(cos), jnp.asarray(sin)]
        out_shape = (jax.ShapeDtypeStruct((ATTN_WIDTH, n), QK_DTYPE), k_shape, vT_shape,
                     jax.ShapeDtypeStruct((n, FOURIER_WIDTH), BF16))
        out_specs = (pl.BlockSpec((ATTN_WIDTH, tm), lambda i: (0, i)), k_spec, vT_spec,
                     pl.BlockSpec((tm, FOURIER_WIDTH), lambda i: (i, 0)))
    else:
        out_shape = (k_shape, vT_shape)
        out_specs = (k_spec, vT_spec)
    return pl.pallas_call(
        functools.partial(_evenproj_kernel, row=0 if latent else 1, latent=latent),
        out_shape=out_shape, grid=(n // tm,), in_specs=in_specs, out_specs=out_specs,
        compiler_params=_cparams(("parallel",)),
        name="evenproj_lat" if latent else "evenproj_ctx",
    )(*args)


def _visit_all(visit, k_ref, vT_ref, kc_ref, vcT_ref, tk):
    def body(c, carry):
        off = pl.multiple_of(c * tk, tk)
        visit(k_ref[0, pl.ds(off, tk), :], vT_ref[:, pl.ds(off, tk)])
        return carry

    lax.fori_loop(0, k_ref.shape[1] // tk, body, 0)
    visit(kc_ref[0], vcT_ref[...])


def _attn_bounded_kernel(qT_ref, k_ref, vT_ref, kc_ref, vcT_ref, o_ref, acc_sc, s_sc, *, tk):
    n_tiles = k_ref.shape[1] // tk
    acc_sc[...] = jnp.zeros(acc_sc.shape, F32)

    def q(j):
        return qT_ref[j * HEAD_DIM:(j + 1) * HEAD_DIM, :]

    def keys(c):
        return k_ref[0, pl.ds(pl.multiple_of(c * tk, tk), tk), :]

    def consume(j, s, vt):
        acc_sc[j] += _dot(vt, jnp.exp2(s).astype(BF16))

    s_sc[...] = _dot(keys(0), q(0))

    def body(c, carry):
        kt = keys(c)
        vt = vT_ref[:, pl.ds(pl.multiple_of(c * tk, tk), tk)]
        s = s_sc[...]
        for j in range(Q_PER_KV):
            if j + 1 < Q_PER_KV:
                s_next = _dot(kt, q(j + 1))
            else:
                s_next = _dot(keys(jnp.minimum(c + 1, n_tiles - 1)), q(0))
            consume(j, s, vt)
            s = s_next
        s_sc[...] = s
        return carry

    lax.fori_loop(0, n_tiles, body, 0, unroll=8)
    kc, vc = kc_ref[0], vcT_ref[...]
    s = _dot(kc, q(0))
    for j in range(Q_PER_KV):
        s_next = _dot(kc, q(j + 1)) if j + 1 < Q_PER_KV else None
        consume(j, s, vc)
        s = s_next
    for j in range(Q_PER_KV):
        acc = acc_sc[j]
        o_ref[j * HEAD_DIM:(j + 1) * HEAD_DIM, :] = (acc[:HEAD_DIM] / acc[HEAD_DIM:HEAD_DIM + 1]).astype(BF16)


def _attn_online_kernel(qT_ref, k_ref, vT_ref, kc_ref, vcT_ref, o_ref, m_sc, acc_sc, *, tk):
    m_sc[...] = jnp.full(m_sc.shape, -jnp.inf, F32)
    acc_sc[...] = jnp.zeros(acc_sc.shape, F32)

    def visit(kt, vt):
        for j in range(Q_PER_KV):
            s = _dot(kt, qT_ref[j * HEAD_DIM:(j + 1) * HEAD_DIM, :])
            m_old = m_sc[j]
            m_new = jnp.maximum(m_old, jnp.max(s, axis=0, keepdims=True))
            p = jnp.exp2(s - m_new).astype(BF16)
            acc_sc[j] = jnp.exp2(m_old - m_new) * acc_sc[j] + _dot(vt, p)
            m_sc[j] = m_new

    _visit_all(visit, k_ref, vT_ref, kc_ref, vcT_ref, tk)
    for j in range(Q_PER_KV):
        acc = acc_sc[j]
        o_ref[j * HEAD_DIM:(j + 1) * HEAD_DIM, :] = (acc[:HEAD_DIM] / acc[HEAD_DIM:HEAD_DIM + 1]).astype(BF16)


def _attention(qT, k, vT, kc, vcT, score_bound, *, tq, tk):
    n = qT.shape[1]
    n_ctx = kc.shape[1]
    gw = Q_PER_KV * HEAD_DIM
    common = dict(
        out_shape=jax.ShapeDtypeStruct((ATTN_WIDTH, n), BF16),
        grid=(N_KV_HEADS, n // tq),
        in_specs=[pl.BlockSpec((gw, tq), lambda g, i: (g, i)),
                  pl.BlockSpec((1, n, HEAD_DIM), lambda g, i: (g, 0, 0)),
                  pl.BlockSpec((V_ROWS, n), lambda g, i: (g, 0)),
                  pl.BlockSpec((1, n_ctx, HEAD_DIM), lambda g, i: (g, 0, 0)),
                  pl.BlockSpec((V_ROWS, n_ctx), lambda g, i: (g, 0))],
        out_specs=pl.BlockSpec((gw, tq), lambda g, i: (g, i)),
        compiler_params=_cparams(("parallel", "parallel")),
    )
    acc = pltpu.VMEM((Q_PER_KV, V_ROWS, tq), F32)
    bounded = pl.pallas_call(functools.partial(_attn_bounded_kernel, tk=tk),
                             scratch_shapes=[acc, pltpu.VMEM((tk, tq), F32)], name="attn_bounded", **common)
    online = pl.pallas_call(functools.partial(_attn_online_kernel, tk=tk),
                            scratch_shapes=[pltpu.VMEM((Q_PER_KV, 1, tq), F32), acc],
                            name="attn_online", **common)
    return lax.cond(score_bound <= EXP2_SAFE_BOUND, bounded, online, qT, k, vT, kc, vcT)


def _four_a_kernel(f_ref, cs_ref, base_ref, twr_ref, twi_ref, y_ref, *, nb):
    n1 = f_ref.shape[0]
    cs = cs_ref[...].astype(BF16)
    base = base_ref[...].astype(BF16)
    for u in range(nb):
        xb = f_ref[:, u * FOURIER_WIDTH:(u + 1) * FOURIER_WIDTH]
        ab = _dot(xb, cs)
        stacked = jnp.concatenate([ab[:, :FOURIER_WIDTH], ab[:, FOURIER_WIDTH:]], axis=0)
        p = _dot(base, stacked.astype(BF16))
        pr, pi = p[:n1], p[n1:]
        tr = twr_ref[0, :, u:u + 1]
        ti = twi_ref[0, :, u:u + 1]
        y_ref[u, 0] = tr * pr - ti * pi
        y_ref[u, 1] = tr * pi + ti * pr


def _four_b_kernel(y_ref, wb_ref, o_ref):
    n2, _, kb, w = y_ref.shape
    y = y_ref[...].reshape(n2 * 2 * kb, w).astype(BF16)
    o_ref[...] = _dot(wb_ref[...].astype(BF16), y).reshape(n2, kb, w)


def _fourier(f):
    n = f.shape[0]
    n2 = LANES
    n1 = n // n2
    nb = kb = 8
    cs, base, twr, twi, wb = (jnp.asarray(t) for t in _fourier_tables(n, kb))
    f2d = f.reshape(n1, n2 * FOURIER_WIDTH)
    y = pl.pallas_call(
        functools.partial(_four_a_kernel, nb=nb),
        out_shape=jax.ShapeDtypeStruct((n2, 2, n1, FOURIER_WIDTH), F32),
        grid=(n2 // nb,),
        in_specs=[pl.BlockSpec((n1, nb * FOURIER_WIDTH), lambda s: (0, s)),
                  pl.BlockSpec(cs.shape, lambda s: (0, 0)),
                  pl.BlockSpec(base.shape, lambda s: (0, 0)),
                  pl.BlockSpec((1, n1, nb), lambda s: (s, 0, 0)),
                  pl.BlockSpec((1, n1, nb), lambda s: (s, 0, 0))],
        out_specs=pl.BlockSpec((nb, 2, n1, FOURIER_WIDTH), lambda s: (s, 0, 0, 0)),
        compiler_params=_cparams(("parallel",)),
        name="four_a",
    )(f2d, cs, base, twr, twi)
    out = pl.pallas_call(
        _four_b_kernel,
        out_shape=jax.ShapeDtypeStruct((n2, n1, FOURIER_WIDTH), F32),
        grid=(n1 // kb,),
        in_specs=[pl.BlockSpec((n2, 2, kb, FOURIER_WIDTH), lambda s: (0, 0, s, 0)),
                  pl.BlockSpec(wb.shape, lambda s: (0, 0))],
        out_specs=pl.BlockSpec((n2, kb, FOURIER_WIDTH), lambda s: (0, s, 0)),
        compiler_params=_cparams(("parallel",)),
        name="four_b",
    )(y, wb)
    return out.reshape(n, FOURIER_WIDTH)


def _outproj_kernel(aT_ref, four_ref, x_ref, wa_ref, wf_ref, nw_ref, mod_ref, x1_ref, h_ref):
    _, _, gt1, sh2, sc2, _ = _mod_rows(mod_ref, 0)
    y = lax.dot_general(aT_ref[...], wa_ref[...], (((0,), (0,)), ((), ())), preferred_element_type=F32)
    y = y + _dot(four_ref[...].astype(BF16), wf_ref[...])
    x1 = x_ref[...] + gt1 * _rms(y, nw_ref[0, 1:2, :])
    x1_ref[...] = x1
    h_ref[...] = (_rms(x1, nw_ref[0, 2:3, :]) * (1.0 + sc2) + sh2).astype(BF16)


def _outproj(attnT, four, x2d, wa, wf, nw, mod, *, tm):
    n = x2d.shape[0]
    const = lambda i: (0, 0)
    return pl.pallas_call(
        _outproj_kernel,
        out_shape=(jax.ShapeDtypeStruct((n, D_MODEL), F32), jax.ShapeDtypeStruct((n, D_MODEL), BF16)),
        grid=(n // tm,),
        in_specs=[pl.BlockSpec((ATTN_WIDTH, tm), lambda i: (0, i)),
                  pl.BlockSpec((tm, FOURIER_WIDTH), lambda i: (i, 0)),
                  pl.BlockSpec((tm, D_MODEL), lambda i: (i, 0)),
                  pl.BlockSpec((ATTN_WIDTH, D_MODEL), const),
                  pl.BlockSpec((FOURIER_WIDTH, D_MODEL), const),
                  pl.BlockSpec((1, 4, D_MODEL), lambda i: (0, 0, 0)),
                  pl.BlockSpec((1, 8, 6 * D_MODEL), lambda i: (0, 0, 0))],
        out_specs=(pl.BlockSpec((tm, D_MODEL), lambda i: (i, 0)),
                   pl.BlockSpec((tm, D_MODEL), lambda i: (i, 0))),
        compiler_params=_cparams(("parallel",)),
        name="outproj",
    )(attnT, four, x2d, wa, wf, nw, mod)


def _ffn_kernel(h_ref, x_ref, wg_ref, wu_ref, wd_ref, nw_ref, mod_ref, nw1_ref, mod1_ref,
                x2_ref, h3_ref, acc_ref):
    c = pl.program_id(1)
    h = h_ref[...]
    g = _dot(h, wg_ref[...])
    u = _dot(h, wu_ref[...])
    part = _dot((g * jax.nn.sigmoid(g) * u).astype(BF16), wd_ref[...])

    @pl.when(c == 0)
    def _():
        acc_ref[...] = part

    @pl.when(c > 0)
    def _():
        acc_ref[...] += part

    @pl.when(c == pl.num_programs(1) - 1)
    def _():
        gt2 = _mod_rows(mod_ref, 0)[5]
        sh, sc = _mod_rows(mod1_ref, 0)[:2]
        x2 = x_ref[...] + gt2 * _rms(acc_ref[...], nw_ref[0, 3:4, :])
        x2_ref[...] = x2
        h3_ref[...] = (_rms(x2, nw1_ref[0, 0:1, :]) * (1.0 + sc) + sh).astype(BF16)


def _ffn(h, x1, wg, wu, wd, nw, mod, nw1, mod1, *, tm, tf):
    n = h.shape[0]
    nwspec = pl.BlockSpec((1, 4, D_MODEL), lambda i, c: (0, 0, 0))
    modspec = pl.BlockSpec((1, 8, 6 * D_MODEL), lambda i, c: (0, 0, 0))
    return pl.pallas_call(
        _ffn_kernel,
        out_shape=(jax.ShapeDtypeStruct((n, D_MODEL), F32), jax.ShapeDtypeStruct((n, D_MODEL), BF16)),
        grid=(n // tm, D_FF // tf),
        in_specs=[pl.BlockSpec((tm, D_MODEL), lambda i, c: (i, 0)),
                  pl.BlockSpec((tm, D_MODEL), lambda i, c: (i, 0)),
                  pl.BlockSpec((D_MODEL, tf), lambda i, c: (0, c)),
                  pl.BlockSpec((D_MODEL, tf), lambda i, c: (0, c)),
                  pl.BlockSpec((tf, D_MODEL), lambda i, c: (c, 0)),
                  nwspec, modspec, nwspec, modspec],
        out_specs=(pl.BlockSpec((tm, D_MODEL), lambda i, c: (i, 0)),
                   pl.BlockSpec((tm, D_MODEL), lambda i, c: (i, 0))),
        scratch_shapes=[pltpu.VMEM((tm, D_MODEL), F32)],
        compiler_params=_cparams(("parallel", "arbitrary")),
        name="ffn",
    )(h, x1, wg, wu, wd, nw, mod, nw1, mod1)


EDGE_ROWS = 16


def _convedge_kernel(hf_ref, hl_ref, wc_ref, wu_ref, vf_ref, vl_ref):
    nt = hf_ref.shape[0]
    for h_ref, v_ref in ((hf_ref, vf_ref), (hl_ref, vl_ref)):
        h = h_ref[...].reshape(nt * EDGE_ROWS, D_MODEL)
        v = _dot(h, wc_ref[...]) * _dot(h, wu_ref[...])
        v_ref[...] = v.reshape(nt, EDGE_ROWS, D_MODEL)


def _convedge(h3, w_in_bf, *, tm):
    n = h3.shape[0]
    nt = n // tm
    h3t = h3.reshape(nt, tm, D_MODEL)
    last = tm // EDGE_ROWS - 1
    shape = jax.ShapeDtypeStruct((nt, EDGE_ROWS, D_MODEL), F32)
    return pl.pallas_call(
        _convedge_kernel,
        out_shape=(shape, shape),
        grid=(1,),
        in_specs=[pl.BlockSpec((nt, EDGE_ROWS, D_MODEL), lambda i: (0, 0, 0)),
                  pl.BlockSpec((nt, EDGE_ROWS, D_MODEL), lambda i: (0, last, 0)),
                  pl.BlockSpec((D_MODEL, D_MODEL), lambda i: (0, 1)),
                  pl.BlockSpec((D_MODEL, D_MODEL), lambda i: (0, 2))],
        out_specs=(pl.BlockSpec((nt, EDGE_ROWS, D_MODEL), lambda i: (0, 0, 0)),
                   pl.BlockSpec((nt, EDGE_ROWS, D_MODEL), lambda i: (0, 0, 0))),
        compiler_params=_cparams(("arbitrary",)),
        name="convedge",
    )(h3t, h3t, w_in_bf, w_in_bf)


def _convmix_kernel(h_ref, x_ref, vl_ref, vf_ref, win_ref, cw_ref, wout_ref, nw_ref, mod_ref, r_ref,
                    x3_ref, h4_ref, comb_ref, sel_ref, rank_ref):
    i = pl.program_id(0)
    tm = h_ref.shape[0]
    _, _, gt1, sh2, sc2, _ = _mod_rows(mod_ref, 0)
    z = _dot(h_ref[...], win_ref[...])
    b = z[:, :D_MODEL]
    v = z[:, D_MODEL:2 * D_MODEL] * z[:, 2 * D_MODEL:]
    has_prev = (i > 0).astype(F32)
    has_next = (i < pl.num_programs(0) - 1).astype(F32)
    prev_row = vl_ref[0, EDGE_ROWS - 1:EDGE_ROWS, :] * has_prev
    next_row = vf_ref[0, 0:1, :] * has_next
    rows = lax.broadcasted_iota(jnp.int32, (tm, 1), 0)
    v_dn = jnp.where(rows == 0, prev_row, pltpu.roll(v, 1, axis=0))
    v_up = jnp.where(rows == tm - 1, next_row, pltpu.roll(v, tm - 1, axis=0))
    conv = v_dn * cw_ref[0, 0:1, :] + v * cw_ref[0, 1:2, :] + v_up * cw_ref[0, 2:3, :]
    y = _dot((b * conv).astype(BF16), wout_ref[...])
    x3 = x_ref[...] + gt1 * _rms(y, nw_ref[0, 1:2, :])
    x3_ref[...] = x3
    h4 = _rms(x3, nw_ref[0, 2:3, :]) * (1.0 + sc2) + sh2
    h4_ref[...] = h4.astype(BF16)
    hhi, hlo = _split_bf16(h4)
    rhi, rlo = _split_bf16(r_ref[...])
    logits = _dot(hhi, rhi) + (_dot(hlo, rhi) + _dot(hhi, rlo))
    lane = lax.broadcasted_iota(jnp.int32, logits.shape, 1)
    logits = jnp.where(lane < N_EXPERTS, logits, -jnp.inf)
    e = jnp.exp(logits - jnp.max(logits, axis=-1, keepdims=True))
    probs = e / jnp.sum(e, axis=-1, keepdims=True)
    v1 = jnp.max(probs, axis=-1, keepdims=True)
    i1 = jnp.min(jnp.where(probs == v1, lane, LANES), axis=-1, keepdims=True)
    rest = jnp.where(lane == i1, -1.0, probs)
    v2 = jnp.max(rest, axis=-1, keepdims=True)
    i2 = jnp.min(jnp.where(rest == v2, lane, LANES), axis=-1, keepdims=True)
    tot = v1 + v2
    comb_ref[...] = jnp.where(lane == i1, v1 / tot, 0.0) + jnp.where(lane == i2, v2 / tot, 0.0)
    sel = jnp.where(jnp.logical_or(lane == i1, lane == i2), 1.0, 0.0)
    sel_ref[...] = sel
    earlier = lax.broadcasted_iota(jnp.int32, (tm, tm), 1) < lax.broadcasted_iota(jnp.int32, (tm, tm), 0)
    rank_ref[...] = _dot(jnp.where(earlier, 1.0, 0.0).astype(BF16), sel.astype(BF16))


def _convmix(h3, x2, v_first, v_last, w_in_bf, conv_w, w_out_bf, nw, mod, router_pad, *, tm):
    n = h3.shape[0]
    nt = n // tm
    const = lambda i: (0, 0)
    lanes_shape = jax.ShapeDtypeStruct((n, LANES), F32)
    lanes_spec = pl.BlockSpec((tm, LANES), lambda i: (i, 0))
    return pl.pallas_call(
        _convmix_kernel,
        out_shape=(jax.ShapeDtypeStruct((n, D_MODEL), F32), jax.ShapeDtypeStruct((n, D_MODEL), BF16),
                   lanes_shape, lanes_shape, lanes_shape),
        grid=(nt,),
        in_specs=[pl.BlockSpec((tm, D_MODEL), lambda i: (i, 0)),
                  pl.BlockSpec((tm, D_MODEL), lambda i: (i, 0)),
                  pl.BlockSpec((1, EDGE_ROWS, D_MODEL), lambda i: (jnp.maximum(i - 1, 0), 0, 0)),
                  pl.BlockSpec((1, EDGE_ROWS, D_MODEL), lambda i: (jnp.minimum(i + 1, nt - 1), 0, 0)),
                  pl.BlockSpec((D_MODEL, 3 * D_MODEL), const),
                  pl.BlockSpec((1, 3, D_MODEL), lambda i: (0, 0, 0)),
                  pl.BlockSpec((D_MODEL, D_MODEL), const),
                  pl.BlockSpec((1, 4, D_MODEL), lambda i: (0, 0, 0)),
                  pl.BlockSpec((1, 8, 6 * D_MODEL), lambda i: (0, 0, 0)),
                  pl.BlockSpec((D_MODEL, LANES), const)],
        out_specs=(pl.BlockSpec((tm, D_MODEL), lambda i: (i, 0)),
                   pl.BlockSpec((tm, D_MODEL), lambda i: (i, 0)),
                   lanes_spec, lanes_spec, lanes_spec),
        compiler_params=_cparams(("parallel",)),
        name="convmix",
    )(h3, x2, v_last, v_first, w_in_bf, conv_w, w_out_bf, nw, mod, router_pad)


ROW_TILE = 512
FLAG_ACTIVE, FLAG_FIRST, FLAG_LAST = 1, 2, 4


def _count_le(sorted_vals, x):
    return jnp.sum((sorted_vals[None, :] <= x[:, None]).astype(jnp.int32), axis=1)


def _ragged_items(n_per_group, lo_per_group, n_items):
    off_end = jnp.cumsum(n_per_group)
    off = off_end - n_per_group
    total = off_end[-1]
    k = jnp.minimum(jnp.arange(n_items, dtype=jnp.int32), total - 1)
    grp = jnp.minimum(_count_le(off_end, k), n_per_group.shape[0] - 1)
    member = lo_per_group[grp] + k - off[grp]
    active = jnp.arange(n_items, dtype=jnp.int32) < total
    return grp, member, active


def _flags(active, key):
    prev_differs = jnp.concatenate([jnp.array([True]), key[1:] != key[:-1]])
    next_active = jnp.concatenate([active[1:], jnp.array([False])])
    next_differs = jnp.concatenate([key[1:] != key[:-1], jnp.array([True])])
    last = jnp.logical_or(next_differs, jnp.logical_not(next_active))
    a = active.astype(jnp.int32)
    return a * (FLAG_ACTIVE + FLAG_FIRST * prev_differs.astype(jnp.int32) + FLAG_LAST * last.astype(jnp.int32))


def _route(sel, lrank, n_row_tiles):
    n = sel.shape[0]
    n_tok_tiles = n // ROW_TILE
    i32 = jnp.int32
    sel8 = sel[:, :N_EXPERTS].astype(i32)
    cnt_tile = jnp.sum(sel8.reshape(n_tok_tiles, ROW_TILE, N_EXPERTS), axis=1)
    cum_end = jnp.cumsum(cnt_tile, axis=0)
    cum_beg = cum_end - cnt_tile
    cnt = cum_end[-1]
    cnt_pad = (cnt + ROW_TILE - 1) // ROW_TILE * ROW_TILE
    grp_end = jnp.cumsum(cnt_pad)
    start = grp_end - cnt_pad
    rank = lrank[:, :N_EXPERTS].astype(i32) + jnp.repeat(cum_beg, ROW_TILE, axis=0)
    pos = jnp.where(sel8 > 0, start[None, :] + rank, -1).astype(F32)
    r = jnp.arange(n_row_tiles, dtype=i32)
    base = r * ROW_TILE
    tile_valid = base < grp_end[-1]
    n_valid = grp_end[-1] // ROW_TILE
    tile_exp = jnp.minimum(_count_le(grp_end, base), N_EXPERTS - 1)
    last_exp = tile_exp[jnp.clip(n_valid - 1, 0, n_row_tiles - 1)]
    tile_exp = jnp.where(tile_valid, tile_exp, last_exp)
    tile_src = jnp.where(tile_valid, r, n_valid - 1)
    tile_first = jnp.logical_and(tile_valid, base == start[tile_exp])
    rho0 = base - start[tile_exp]
    rho1 = jnp.minimum(rho0 + ROW_TILE, cnt[tile_exp])
    ends = cum_end.T[tile_exp]
    c_lo = jnp.sum((ends <= rho0[:, None]).astype(i32), axis=1)
    c_hi = jnp.sum((ends <= (rho1 - 1)[:, None]).astype(i32), axis=1)
    n_chunks = jnp.where(tile_valid, c_hi - c_lo + 1, 0)
    n_items = n_row_tiles + N_EXPERTS * n_tok_tiles
    g_tile, g_chunk, g_active = _ragged_items(n_chunks, c_lo, n_items)
    g_tile = jnp.clip(g_tile, 0, n_row_tiles - 1)
    g_chunk = jnp.clip(g_chunk, 0, n_tok_tiles - 1)
    gather_items = (g_tile, g_chunk, tile_exp[g_tile], _flags(g_active, g_tile))
    row_lo = (start[None, :] + cum_beg).reshape(-1)
    row_hi = (start[None, :] + cum_end - 1).reshape(-1)
    n_rt = jnp.where(cnt_tile.reshape(-1) > 0, row_hi // ROW_TILE - row_lo // ROW_TILE + 1, 0)
    c_pair, c_row, c_active = _ragged_items(n_rt, row_lo // ROW_TILE, n_items)
    c_tok = jnp.clip(c_pair // N_EXPERTS, 0, n_tok_tiles - 1)
    c_row = jnp.clip(c_row, 0, n_row_tiles - 1)
    combine_items = (c_tok, c_row, c_pair % N_EXPERTS, _flags(c_active, c_tok))
    tile_src = jnp.clip(tile_src, 0, n_row_tiles - 1)
    tiles = (tile_exp, tile_src, tile_valid.astype(i32), tile_first.astype(i32))
    return pos, gather_items, tiles, combine_items


def _gather_kernel(it_tile, it_chunk, it_exp, it_flag, pos_ref, h_ref, xs_ref, acc_ref):
    k = pl.program_id(0)
    flag = it_flag[k]

    @pl.when(flag & FLAG_ACTIVE != 0)
    def _():
        rows = it_tile[k] * ROW_TILE + lax.broadcasted_iota(jnp.int32, (ROW_TILE, 1), 0)
        onehot = jnp.where(pos_ref[0] == rows.astype(F32), 1.0, 0.0).astype(BF16)
        part = _dot(onehot, h_ref[...])

        @pl.when(flag & FLAG_FIRST != 0)
        def _():
            acc_ref[...] = part

        @pl.when(flag & FLAG_FIRST == 0)
        def _():
            acc_ref[...] += part

        @pl.when(flag & FLAG_LAST != 0)
        def _():
            xs_ref[...] = acc_ref[...].astype(BF16)


def _gather_rows(items, pos_t, h4, n_rows):
    n_items = items[0].shape[0]
    grid_spec = pltpu.PrefetchScalarGridSpec(
        num_scalar_prefetch=4, grid=(n_items,),
        in_specs=[pl.BlockSpec((1, 1, ROW_TILE), lambda k, t, c, e, f: (e[k], 0, c[k])),
                  pl.BlockSpec((ROW_TILE, D_MODEL), lambda k, t, c, e, f: (c[k], 0))],
        out_specs=pl.BlockSpec((ROW_TILE, D_MODEL), lambda k, t, c, e, f: (t[k], 0)),
        scratch_shapes=[pltpu.VMEM((ROW_TILE, D_MODEL), F32)])
    return pl.pallas_call(
        _gather_kernel, grid_spec=grid_spec,
        out_shape=jax.ShapeDtypeStruct((n_rows, D_MODEL), BF16),
        compiler_params=_cparams(("arbitrary",)),
        name="moe_gather",
    )(*items, pos_t, h4)


def _experts_kernel(t_exp, t_src, t_valid, t_first, xs_ref, wg_ref, wu_ref, wd_ref, ys_ref,
                    wg_c, wu_c, wd_c, acc_ref):
    r = pl.program_id(0)
    c = pl.program_id(1)
    last = c == pl.num_programs(1) - 1

    @pl.when(t_first[r] != 0)
    def _():
        wg_c[c] = wg_ref[0, 0].astype(BF16)
        wu_c[c] = wu_ref[0, 0].astype(BF16)
        wd_c[c] = wd_ref[0, 0].astype(BF16)

    @pl.when(t_valid[r] != 0)
    def _():
        x = xs_ref[...]
        g = _dot(x, wg_c[c])
        u = _dot(x, wu_c[c])
        part = _dot((g * jax.nn.sigmoid(g) * u).astype(BF16), wd_c[c])

        @pl.when(c == 0)
        def _():
            acc_ref[...] = part

        @pl.when(c > 0)
        def _():
            acc_ref[...] += part

        @pl.when(last)
        def _():
            ys_ref[...] = acc_ref[...].astype(BF16)

    @pl.when(jnp.logical_and(t_valid[r] == 0, last))
    def _():
        ys_ref[...] = jnp.zeros(ys_ref.shape, BF16)


def _experts(tiles, xs, wg, wu, wd, *, tf):
    n_rows = xs.shape[0]
    n_ch = D_EXPERT // tf

    def w_in(r, c, e, s, v, f):
        return (0, e[r], 0, jnp.where(f[r] != 0, c, n_ch - 1))

    def w_dn(r, c, e, s, v, f):
        return (0, e[r], jnp.where(f[r] != 0, c, n_ch - 1), 0)

    grid_spec = pltpu.PrefetchScalarGridSpec(
        num_scalar_prefetch=4, grid=(n_rows // ROW_TILE, n_ch),
        in_specs=[pl.BlockSpec((ROW_TILE, D_MODEL), lambda r, c, e, s, v, f: (s[r], 0)),
                  pl.BlockSpec((1, 1, D_MODEL, tf), w_in),
                  pl.BlockSpec((1, 1, D_MODEL, tf), w_in),
                  pl.BlockSpec((1, 1, tf, D_MODEL), w_dn)],
        out_specs=pl.BlockSpec((ROW_TILE, D_MODEL), lambda r, c, e, s, v, f: (r, 0)),
        scratch_shapes=[pltpu.VMEM((n_ch, D_MODEL, tf), BF16), pltpu.VMEM((n_ch, D_MODEL, tf), BF16),
                        pltpu.VMEM((n_ch, tf, D_MODEL), BF16), pltpu.VMEM((ROW_TILE, D_MODEL), F32)])
    return pl.pallas_call(
        _experts_kernel, grid_spec=grid_spec,
        out_shape=jax.ShapeDtypeStruct((n_rows, D_MODEL), BF16),
        compiler_params=_cparams(("arbitrary", "arbitrary")),
        name="moe_experts",
    )(*tiles, xs, wg, wu, wd)


def _combine_kernel(it_tok, it_row, it_exp, it_flag, pos_ref, comb_ref, ys_ref, x_ref, nw_ref, mod_ref, o_ref,
                    acc_ref):
    k = pl.program_id(0)
    flag = it_flag[k]

    @pl.when(flag & FLAG_ACTIVE != 0)
    def _():
        pos = pos_ref[...]
        lane = lax.broadcasted_iota(jnp.int32, pos.shape, 1)
        mine = jnp.sum(jnp.where(lane == it_exp[k], pos, 0.0), axis=-1, keepdims=True)
        weight = jnp.sum(jnp.where(lane == it_exp[k], comb_ref[...], 0.0), axis=-1, keepdims=True)
        cols = it_row[k] * ROW_TILE + lax.broadcasted_iota(jnp.int32, (1, ROW_TILE), 1)
        onehot = jnp.where(mine == cols.astype(F32), 1.0, 0.0).astype(BF16)
        part = weight * _dot(onehot, ys_ref[...])

        @pl.when(flag & FLAG_FIRST != 0)
        def _():
            acc_ref[...] = part

        @pl.when(flag & FLAG_FIRST == 0)
        def _():
            acc_ref[...] += part

        @pl.when(flag & FLAG_LAST != 0)
        def _():
            gt2 = _mod_rows(mod_ref, 0)[5]
            o_ref[...] = x_ref[...] + gt2 * _rms(acc_ref[...], nw_ref[0, 3:4, :])


def _combine(items, pos_pad, comb, ys, x3, nw, mod):
    n = x3.shape[0]
    n_items = items[0].shape[0]
    grid_spec = pltpu.PrefetchScalarGridSpec(
        num_scalar_prefetch=4, grid=(n_items,),
        in_specs=[pl.BlockSpec((ROW_TILE, LANES), lambda k, t, r, e, f: (t[k], 0)),
                  pl.BlockSpec((ROW_TILE, LANES), lambda k, t, r, e, f: (t[k], 0)),
                  pl.BlockSpec((ROW_TILE, D_MODEL), lambda k, t, r, e, f: (r[k], 0)),
                  pl.BlockSpec((ROW_TILE, D_MODEL), lambda k, t, r, e, f: (t[k], 0)),
                  pl.BlockSpec((1, 4, D_MODEL), lambda k, t, r, e, f: (0, 0, 0)),
                  pl.BlockSpec((1, 8, 6 * D_MODEL), lambda k, t, r, e, f: (0, 0, 0))],
        out_specs=pl.BlockSpec((ROW_TILE, D_MODEL), lambda k, t, r, e, f: (t[k], 0)),
        scratch_shapes=[pltpu.VMEM((ROW_TILE, D_MODEL), F32)])
    return pl.pallas_call(
        _combine_kernel, grid_spec=grid_spec,
        out_shape=jax.ShapeDtypeStruct((n, D_MODEL), F32),
        compiler_params=_cparams(("arbitrary",)),
        name="moe_combine",
    )(*items, pos_pad, comb, ys, x3, nw, mod)


def _moe(h4, comb, sel, lrank, x3, wg, wu, wd, nw, mod):
    n = h4.shape[0]
    n_rows = 2 * n + N_EXPERTS * ROW_TILE
    pos, gather_items, tiles, combine_items = _route(sel, lrank, n_rows // ROW_TILE)
    pos_t = pos.T.reshape(N_EXPERTS, 1, n)
    pos_pad = jnp.pad(pos, ((0, 0), (0, LANES - N_EXPERTS)), constant_values=-1.0)
    xs = _gather_rows(gather_items, pos_t, h4, n_rows)
    ys = _experts(tiles, xs, wg, wu, wd, tf=512)
    return _combine(combine_items, pos_pad, comb, ys, x3, nw, mod)


def kernel(x, c, ctx, c_ctx, ada_w, ada_b, norm_w, e_w_in, e_q_gain, e_k_gain, e_w_out, e_ffn_gate,
           e_ffn_up, e_ffn_down, o_w_in, o_conv_w, o_w_out, o_router, o_exp_gate, o_exp_up, o_exp_down):
    assert x.shape[0] == 1 and x.shape[2] == D_MODEL and ada_w.shape[0] == 2
    n = x.shape[1]
    x2d = x[0]
    ctx2d = ctx[0]
    mod = _ada(c, c_ctx, ada_w, ada_b)
    mod0, mod1 = mod[0:1], mod[1:2]
    nw0, nw1 = norm_w[0:1], norm_w[1:2]

    w_in = e_w_in[0].astype(BF16)
    scale = HEAD_DIM ** -0.5 * np.log2(np.e)
    gain = jnp.concatenate([jnp.tile(e_q_gain[0], N_Q_HEADS) * scale,
                            jnp.tile(e_k_gain[0], N_KV_HEADS)])[None, :]
    score_bound = 1.02 * HEAD_DIM * scale * jnp.max(jnp.abs(e_q_gain[0])) * jnp.max(jnp.abs(e_k_gain[0]))
    qT, k, vT, f = _evenproj(x2d, nw0, mod0, w_in, gain, latent=True, tm=512)
    kc, vcT = _evenproj(ctx2d, nw0, mod0, w_in, gain, latent=False, tm=ctx2d.shape[0])
    attnT = _attention(qT, k, vT, kc, vcT, score_bound, tq=512, tk=512)
    four = _fourier(f)
    w_out = e_w_out[0].astype(BF16)
    x1, h2 = _outproj(attnT, four, x2d, w_out[:ATTN_WIDTH], w_out[ATTN_WIDTH:], nw0, mod0, tm=512)
    x2, h3 = _ffn(h2, x1, e_ffn_gate[0].astype(BF16), e_ffn_up[0].astype(BF16),
                  e_ffn_down[0].astype(BF16), nw0, mod0, nw1, mod1, tm=1024, tf=1408)

    ow_in = o_w_in[0].astype(BF16)
    v_first, v_last = _convedge(h3, ow_in, tm=ROW_TILE)
    router_pad = jnp.pad(o_router[0], ((0, 0), (0, LANES - N_EXPERTS)))
    x3, h4, comb, sel, lrank = _convmix(h3, x2, v_first, v_last, ow_in, o_conv_w, o_w_out[0].astype(BF16),
                                        nw1, mod1, router_pad, tm=ROW_TILE)
    out = _moe(h4, comb, sel, lrank, x3, o_exp_gate, o_exp_up, o_exp_down, nw1, mod1)
    return out[None]
```

```python
import functools

import numpy as np
import jax
import jax.numpy as jnp
from jax import lax
from jax.experimental import pallas as pl
from jax.experimental.pallas import tpu as pltpu
from jax.experimental.pallas import tpu_sc as plsc

D_MODEL = 1024
GRID_W = 64
HEAD_DIM = 64
N_Q_HEADS = 12
N_KV_HEADS = 4
Q_PER_KV = N_Q_HEADS // N_KV_HEADS
ATTN_WIDTH = N_Q_HEADS * HEAD_DIM
KV_WIDTH = N_KV_HEADS * HEAD_DIM
QK_WIDTH = ATTN_WIDTH + KV_WIDTH
N_FOURIER_GROUPS = 4
FOURIER_GROUP_DIM = 64
FOURIER_WIDTH = N_FOURIER_GROUPS * FOURIER_GROUP_DIM
EVEN_IN_WIDTH = ATTN_WIDTH + 2 * KV_WIDTH + FOURIER_WIDTH
D_FF = 2816
N_EXPERTS = 8
D_EXPERT = 3584
ONES_ROWS = 16
V_ROWS = HEAD_DIM + ONES_ROWS
EXP2_SAFE_BOUND = 60.0
ROPE_THETA = 10000.0
ROPE_HALF = HEAD_DIM // 4
NORM_EPS = 1e-6

LANES = 128
VMEM_LIMIT = 56 * 1024 * 1024

BF16 = jnp.bfloat16
QK_DTYPE = jnp.float8_e4m3fn
F32 = jnp.float32


def _cparams(semantics, vmem=VMEM_LIMIT):
    return pltpu.CompilerParams(dimension_semantics=semantics, vmem_limit_bytes=vmem)


def _dot(a, b):
    return jnp.dot(a, b, preferred_element_type=F32)


def _split_bf16(x):
    hi = x.astype(BF16)
    lo = (x - hi.astype(F32)).astype(BF16)
    return hi, lo


def _rms(x, g):
    return x * lax.rsqrt(jnp.mean(x * x, axis=-1, keepdims=True) + NORM_EPS) * g


def _mod_rows(mod_ref, row):
    return [mod_ref[0, row:row + 1, i * D_MODEL:(i + 1) * D_MODEL] for i in range(6)]


def _pack_pairs(x):
    k = x.shape[1] // 2
    bits = lax.bitcast_convert_type(x.astype(BF16).astype(F32), jnp.uint32)
    return (bits[:, :k] >> 16) | (bits[:, k:] & jnp.uint32(0xFFFF0000))


def _unpack_pairs(w):
    lo = lax.bitcast_convert_type(w << 16, F32).astype(BF16)
    hi = lax.bitcast_convert_type(w & jnp.uint32(0xFFFF0000), F32).astype(BF16)
    return jnp.concatenate([lo, hi], axis=1)


@functools.lru_cache(maxsize=None)
def _rope_tables(n_tokens):
    t = np.arange(n_tokens)
    row = (t // GRID_W).astype(np.float64)
    col = (t % GRID_W).astype(np.float64)
    inv = ROPE_THETA ** (-np.arange(ROPE_HALF, dtype=np.float64) / ROPE_HALF)
    ar, ac = row[:, None] * inv, col[:, None] * inv
    cos = np.concatenate([np.cos(ar), np.cos(ar), np.cos(ac), np.cos(ac)], axis=-1)
    sin = np.concatenate([-np.sin(ar), np.sin(ar), -np.sin(ac), np.sin(ac)], axis=-1)
    reps = LANES // HEAD_DIM
    return (np.tile(cos, (1, reps)).astype(np.float32), np.tile(sin, (1, reps)).astype(np.float32))


@functools.lru_cache(maxsize=None)
def _head_matrices():
    head = np.arange(QK_WIDTH) // HEAD_DIM
    red = (head[:, None] == np.arange(LANES)[None, :]).astype(np.float32) / HEAD_DIM
    exp = (np.arange(LANES)[:, None] == head[None, :]).astype(np.float32)
    return red, exp


@functools.lru_cache(maxsize=None)
def _fourier_tables(n_tokens, kb):
    n2 = LANES
    n1 = n_tokens // n2
    c = np.arange(FOURIER_GROUP_DIM)
    ang = 2 * np.pi * np.outer(c, c) / FOURIER_GROUP_DIM
    eye = np.eye(N_FOURIER_GROUPS)
    cs = np.concatenate([np.kron(eye, np.cos(ang)), np.kron(eye, np.sin(ang))], axis=1)
    k1 = np.arange(n1)
    th = 2 * np.pi * np.outer(k1, k1) / n1
    cr, ci = np.cos(th), -np.sin(th)
    base = np.block([[cr, ci], [ci, -cr]])
    psi = 2 * np.pi * np.outer(k1, np.arange(n2)) / n_tokens
    twr = np.cos(psi).reshape(n1, n2 // kb, kb).transpose(1, 0, 2)
    twi = (-np.sin(psi)).reshape(n1, n2 // kb, kb).transpose(1, 0, 2)
    k2 = np.arange(n2)
    ph = 2 * np.pi * np.outer(k2, k2) / n2
    fr, fi = np.cos(ph), -np.sin(ph)
    scale = 1.0 / np.sqrt(n_tokens * FOURIER_GROUP_DIM)
    m3 = np.stack([fr, -fi], axis=-1) * scale
    wb = np.einsum('knr,uv->kunrv', m3, np.eye(kb)).reshape(n2 * kb, n2 * 2 * kb)
    f32 = np.float32
    return cs.astype(f32), base.astype(f32), twr.astype(f32), twi.astype(f32), wb.astype(f32)


def _ada_kernel(cb_ref, w_ref, b_ref, o_ref):
    tn = o_ref.shape[-1]
    o_ref[...] = jnp.zeros(o_ref.shape, F32)
    for r in range(2):
        cb = cb_ref[r]
        s = cb * jax.nn.sigmoid(cb)
        for j in range(tn // LANES):
            sl = slice(j * LANES, (j + 1) * LANES)
            col = jnp.sum(s * w_ref[0, :, sl], axis=0, keepdims=True)
            o_ref[0, r:r + 1, sl] = col + b_ref[0, :, sl]


def _ada(c, c_ctx, ada_w, ada_b):
    depth = ada_w.shape[0]
    n = ada_w.shape[-1]
    tn = 1536
    cb = jnp.stack([jnp.broadcast_to(c[0][:, None], (D_MODEL, LANES)),
                    jnp.broadcast_to(c_ctx[:, None], (D_MODEL, LANES))])
    return pl.pallas_call(
        _ada_kernel,
        out_shape=jax.ShapeDtypeStruct((depth, 8, n), F32),
        grid=(depth, n // tn),
        in_specs=[pl.BlockSpec((2, D_MODEL, LANES), lambda i, j: (0, 0, 0)),
                  pl.BlockSpec((1, D_MODEL, tn), lambda i, j: (i, 0, j)),
                  pl.BlockSpec((1, 1, tn), lambda i, j: (i, 0, j))],
        out_specs=pl.BlockSpec((1, 8, tn), lambda i, j: (i, 0, j)),
        compiler_params=_cparams(("parallel", "parallel")),
        name="ada",
    )(cb, ada_w, ada_b[:, None, :])


def _evenproj_kernel(*refs, row, latent):
    if latent:
        (x_ref, nw_ref, mod_ref, w_ref, gain_ref, red_ref, exp_ref, cos_ref, sin_ref,
         qT_ref, k_ref, vT_ref, f_ref) = refs
    else:
        x_ref, nw_ref, mod_ref, w_ref, gain_ref, red_ref, exp_ref, k_ref, vT_ref = refs
    sh, sc = _mod_rows(mod_ref, row)[:2]
    h = (_rms(x_ref[...], nw_ref[0, 0:1, :]) * (1.0 + sc) + sh).astype(BF16)
    z = _dot(h, w_ref[...])
    zqk = z[:, :QK_WIDTH]
    hi, lo = _split_bf16(zqk * zqk)
    red = red_ref[...].astype(BF16)
    ms = _dot(hi, red) + _dot(lo, red)
    rhi, rlo = _split_bf16(lax.rsqrt(ms + NORM_EPS))
    expm = exp_ref[...].astype(BF16)
    yn = zqk * (_dot(rhi, expm) + _dot(rlo, expm)) * gain_ref[...]
    if latent:
        lane = lax.broadcasted_iota(jnp.int32, (1, LANES), 1)
        first_half = (lane // ROPE_HALF) % 2 == 0
        cos, sin = cos_ref[...], sin_ref[...]
        chunks = []
        for c in range(QK_WIDTH // LANES):
            yc = yn[:, c * LANES:(c + 1) * LANES]
            partner = jnp.where(first_half, pltpu.roll(yc, LANES - ROPE_HALF, axis=1),
                                pltpu.roll(yc, ROPE_HALF, axis=1))
            chunks.append(yc * cos + partner * sin)
        yn = jnp.concatenate(chunks, axis=1)
        qT_ref[...] = yn[:, :ATTN_WIDTH].T.astype(QK_DTYPE)
        f_ref[...] = z[:, QK_WIDTH + KV_WIDTH:].astype(BF16)
    for g in range(N_KV_HEADS):
        k_ref[g] = yn[:, ATTN_WIDTH + g * HEAD_DIM:ATTN_WIDTH + (g + 1) * HEAD_DIM].astype(QK_DTYPE)
    vT = z[:, QK_WIDTH:QK_WIDTH + KV_WIDTH].T.astype(BF16)
    ones = jnp.ones((ONES_ROWS, vT.shape[1]), BF16)
    for g in range(N_KV_HEADS):
        vT_ref[g * V_ROWS:g * V_ROWS + HEAD_DIM, :] = vT[g * HEAD_DIM:(g + 1) * HEAD_DIM]
        vT_ref[g * V_ROWS + HEAD_DIM:(g + 1) * V_ROWS, :] = ones


def _evenproj(x2d, nw, mod, w_bf, gain, *, latent, tm):
    n = x2d.shape[0]
    red, expm = _head_matrices()
    const = lambda i: (0, 0)
    in_specs = [pl.BlockSpec((tm, D_MODEL), lambda i: (i, 0)),
                pl.BlockSpec((1, 4, D_MODEL), lambda i: (0, 0, 0)),
                pl.BlockSpec((1, 8, 6 * D_MODEL), lambda i: (0, 0, 0)),
                pl.BlockSpec((D_MODEL, EVEN_IN_WIDTH), const),
                pl.BlockSpec((1, QK_WIDTH), const),
                pl.BlockSpec((QK_WIDTH, LANES), const),
                pl.BlockSpec((LANES, QK_WIDTH), const)]
    args = [x2d, nw, mod, w_bf, gain, jnp.asarray(red), jnp.asarray(expm)]
    k_shape = jax.ShapeDtypeStruct((N_KV_HEADS, n, HEAD_DIM), QK_DTYPE)
    vT_shape = jax.ShapeDtypeStruct((N_KV_HEADS * V_ROWS, n), BF16)
    k_spec = pl.BlockSpec((N_KV_HEADS, tm, HEAD_DIM), lambda i: (0, i, 0))
    vT_spec = pl.BlockSpec((N_KV_HEADS * V_ROWS, tm), lambda i: (0, i))
    if latent:
        cos, sin = _rope_tables(n)
        in_specs += [pl.BlockSpec((tm, LANES), lambda i: (i, 0))] * 2
        args += [jnp.asarray(cos), jnp.asarray(sin)]
        out_shape = (jax.ShapeDtypeStruct((ATTN_WIDTH, n), QK_DTYPE), k_shape, vT_shape,
                     jax.ShapeDtypeStruct((n, FOURIER_WIDTH), BF16))
        out_specs = (pl.BlockSpec((ATTN_WIDTH, tm), lambda i: (0, i)), k_spec, vT_spec,
                     pl.BlockSpec((tm, FOURIER_WIDTH), lambda i: (i, 0)))
    else:
        out_shape = (k_shape, vT_shape)
        out_specs = (k_spec, vT_spec)
    return pl.pallas_call(
        functools.partial(_evenproj_kernel, row=0 if latent else 1, latent=latent),
        out_shape=out_shape, grid=(n // tm,), in_specs=in_specs, out_specs=out_specs,
        compiler_params=_cparams(("parallel",)),
        name="evenproj_lat" if latent else "evenproj_ctx",
    )(*args)


def _visit_all(visit, k_ref, vT_ref, kc_ref, vcT_ref, tk):
    def body(c, carry):
        off = pl.multiple_of(c * tk, tk)
        visit(k_ref[0, pl.ds(off, tk), :], vT_ref[:, pl.ds(off, tk)])
        return carry

    lax.fori_loop(0, k_ref.shape[1] // tk, body, 0)
    visit(kc_ref[0], vcT_ref[...])


def _attn_bounded_kernel(qT_ref, k_ref, vT_ref, kc_ref, vcT_ref, o_ref, acc_sc, s_sc, *, tk):
    n_tiles = k_ref.shape[1] // tk
    acc_sc[...] = jnp.zeros(acc_sc.shape, F32)

    def q(j):
        return qT_ref[j * HEAD_DIM:(j + 1) * HEAD_DIM, :]

    def keys(c):
        return k_ref[0, pl.ds(pl.multiple_of(c * tk, tk), tk), :]

    def consume(j, s, vt):
        acc_sc[j] += _dot(vt, jnp.exp2(s).astype(BF16))

    s_sc[...] = _dot(keys(0), q(0))

    def body(c, carry):
        kt = keys(c)
        vt = vT_ref[:, pl.ds(pl.multiple_of(c * tk, tk), tk)]
        s = s_sc[...]
        for j in range(Q_PER_KV):
            if j + 1 < Q_PER_KV:
                s_next = _dot(kt, q(j + 1))
            else:
                s_next = _dot(keys(jnp.minimum(c + 1, n_tiles - 1)), q(0))
            consume(j, s, vt)
            s = s_next
        s_sc[...] = s
        return carry

    lax.fori_loop(0, n_tiles, body, 0, unroll=8)
    kc, vc = kc_ref[0], vcT_ref[...]
    s = _dot(kc, q(0))
    for j in range(Q_PER_KV):
        s_next = _dot(kc, q(j + 1)) if j + 1 < Q_PER_KV else None
        consume(j, s, vc)
        s = s_next
    for j in range(Q_PER_KV):
        acc = acc_sc[j]
        o_ref[j * HEAD_DIM:(j + 1) * HEAD_DIM, :] = (acc[:HEAD_DIM] / acc[HEAD_DIM:HEAD_DIM + 1]).astype(BF16)


def _attn_online_kernel(qT_ref, k_ref, vT_ref, kc_ref, vcT_ref, o_ref, m_sc, acc_sc, *, tk):
    m_sc[...] = jnp.full(m_sc.shape, -jnp.inf, F32)
    acc_sc[...] = jnp.zeros(acc_sc.shape, F32)

    def visit(kt, vt):
        for j in range(Q_PER_KV):
            s = _dot(kt, qT_ref[j * HEAD_DIM:(j + 1) * HEAD_DIM, :])
            m_old = m_sc[j]
            m_new = jnp.maximum(m_old, jnp.max(s, axis=0, keepdims=True))
            p = jnp.exp2(s - m_new).astype(BF16)
            acc_sc[j] = jnp.exp2(m_old - m_new) * acc_sc[j] + _dot(vt, p)
            m_sc[j] = m_new

    _visit_all(visit, k_ref, vT_ref, kc_ref, vcT_ref, tk)
    for j in range(Q_PER_KV):
        acc = acc_sc[j]
        o_ref[j * HEAD_DIM:(j + 1) * HEAD_DIM, :] = (acc[:HEAD_DIM] / acc[HEAD_DIM:HEAD_DIM + 1]).astype(BF16)


def _attention(qT, k, vT, kc, vcT, score_bound, *, tq, tk):
    n = qT.shape[1]
    n_ctx = kc.shape[1]
    gw = Q_PER_KV * HEAD_DIM
    common = dict(
        out_shape=jax.ShapeDtypeStruct((ATTN_WIDTH, n), BF16),
        grid=(N_KV_HEADS, n // tq),
        in_specs=[pl.BlockSpec((gw, tq), lambda g, i: (g, i)),
                  pl.BlockSpec((1, n, HEAD_DIM), lambda g, i: (g, 0, 0)),
                  pl.BlockSpec((V_ROWS, n), lambda g, i: (g, 0)),
                  pl.BlockSpec((1, n_ctx, HEAD_DIM), lambda g, i: (g, 0, 0)),
                  pl.BlockSpec((V_ROWS, n_ctx), lambda g, i: (g, 0))],
        out_specs=pl.BlockSpec((gw, tq), lambda g, i: (g, i)),
        compiler_params=_cparams(("parallel", "parallel")),
    )
    acc = pltpu.VMEM((Q_PER_KV, V_ROWS, tq), F32)
    bounded = pl.pallas_call(functools.partial(_attn_bounded_kernel, tk=tk),
                             scratch_shapes=[acc, pltpu.VMEM((tk, tq), F32)], name="attn_bounded", **common)
    online = pl.pallas_call(functools.partial(_attn_online_kernel, tk=tk),
                            scratch_shapes=[pltpu.VMEM((Q_PER_KV, 1, tq), F32), acc],
                            name="attn_online", **common)
    return lax.cond(score_bound <= EXP2_SAFE_BOUND, bounded, online, qT, k, vT, kc, vcT)


def _four_a_kernel(f_ref, cs_ref, base_ref, twr_ref, twi_ref, y_ref, *, nb):
    n1 = f_ref.shape[0]
    cs = cs_ref[...].astype(BF16)
    base = base_ref[...].astype(BF16)
    for u in range(nb):
        xb = f_ref[:, u * FOURIER_WIDTH:(u + 1) * FOURIER_WIDTH]
        ab = _dot(xb, cs)
        stacked = jnp.concatenate([ab[:, :FOURIER_WIDTH], ab[:, FOURIER_WIDTH:]], axis=0)
        p = _dot(base, stacked.astype(BF16))
        pr, pi = p[:n1], p[n1:]
        tr = twr_ref[0, :, u:u + 1]
        ti = twi_ref[0, :, u:u + 1]
        y_ref[u, 0] = tr * pr - ti * pi
        y_ref[u, 1] = tr * pi + ti * pr


def _four_b_kernel(y_ref, wb_ref, o_ref):
    n2, _, kb, w = y_ref.shape
    y = y_ref[...].reshape(n2 * 2 * kb, w).astype(BF16)
    o_ref[...] = _dot(wb_ref[...].astype(BF16), y).reshape(n2, kb, w)


def _fourier(f):
    n = f.shape[0]
    n2 = LANES
    n1 = n // n2
    nb = kb = 8
    cs, base, twr, twi, wb = (jnp.asarray(t) for t in _fourier_tables(n, kb))
    f2d = f.reshape(n1, n2 * FOURIER_WIDTH)
    y = pl.pallas_call(
        functools.partial(_four_a_kernel, nb=nb),
        out_shape=jax.ShapeDtypeStruct((n2, 2, n1, FOURIER_WIDTH), F32),
        grid=(n2 // nb,),
        in_specs=[pl.BlockSpec((n1, nb * FOURIER_WIDTH), lambda s: (0, s)),
                  pl.BlockSpec(cs.shape, lambda s: (0, 0)),
                  pl.BlockSpec(base.shape, lambda s: (0, 0)),
                  pl.BlockSpec((1, n1, nb), lambda s: (s, 0, 0)),
                  pl.BlockSpec((1, n1, nb), lambda s: (s, 0, 0))],
        out_specs=pl.BlockSpec((nb, 2, n1, FOURIER_WIDTH), lambda s: (s, 0, 0, 0)),
        compiler_params=_cparams(("parallel",)),
        name="four_a",
    )(f2d, cs, base, twr, twi)
    out = pl.pallas_call(
        _four_b_kernel,
        out_shape=jax.ShapeDtypeStruct((n2, n1, FOURIER_WIDTH), F32),
        grid=(n1 // kb,),
        in_specs=[pl.BlockSpec((n2, 2, kb, FOURIER_WIDTH), lambda s: (0, 0, s, 0)),
                  pl.BlockSpec(wb.shape, lambda s: (0, 0))],
        out_specs=pl.BlockSpec((n2, kb, FOURIER_WIDTH), lambda s: (0, s, 0)),
        compiler_params=_cparams(("parallel",)),
        name="four_b",
    )(y, wb)
    return out.reshape(n, FOURIER_WIDTH)


def _outproj_kernel(aT_ref, four_ref, x_ref, wa_ref, wf_ref, nw_ref, mod_ref, x1_ref, h_ref):
    _, _, gt1, sh2, sc2, _ = _mod_rows(mod_ref, 0)
    y = lax.dot_general(aT_ref[...], wa_ref[...], (((0,), (0,)), ((), ())), preferred_element_type=F32)
    y = y + _dot(four_ref[...].astype(BF16), wf_ref[...])
    x1 = x_ref[...] + gt1 * _rms(y, nw_ref[0, 1:2, :])
    x1_ref[...] = x1
    h_ref[...] = (_rms(x1, nw_ref[0, 2:3, :]) * (1.0 + sc2) + sh2).astype(BF16)


def _outproj(attnT, four, x2d, wa, wf, nw, mod, *, tm):
    n = x2d.shape[0]
    const = lambda i: (0, 0)
    return pl.pallas_call(
        _outproj_kernel,
        out_shape=(jax.ShapeDtypeStruct((n, D_MODEL), F32), jax.ShapeDtypeStruct((n, D_MODEL), BF16)),
        grid=(n // tm,),
        in_specs=[pl.BlockSpec((ATTN_WIDTH, tm), lambda i: (0, i)),
                  pl.BlockSpec((tm, FOURIER_WIDTH), lambda i: (i, 0)),
                  pl.BlockSpec((tm, D_MODEL), lambda i: (i, 0)),
                  pl.BlockSpec((ATTN_WIDTH, D_MODEL), const),
                  pl.BlockSpec((FOURIER_WIDTH, D_MODEL), const),
                  pl.BlockSpec((1, 4, D_MODEL), lambda i: (0, 0, 0)),
                  pl.BlockSpec((1, 8, 6 * D_MODEL), lambda i: (0, 0, 0))],
        out_specs=(pl.BlockSpec((tm, D_MODEL), lambda i: (i, 0)),
                   pl.BlockSpec((tm, D_MODEL), lambda i: (i, 0))),
        compiler_params=_cparams(("parallel",)),
        name="outproj",
    )(attnT, four, x2d, wa, wf, nw, mod)


def _ffn_kernel(h_ref, x_ref, wg_ref, wu_ref, wd_ref, nw_ref, mod_ref, nw1_ref, mod1_ref,
                x2_ref, h3_ref, acc_ref):
    c = pl.program_id(1)
    h = h_ref[...]
    g = _dot(h, wg_ref[...])
    u = _dot(h, wu_ref[...])
    part = _dot((g * jax.nn.sigmoid(g) * u).astype(BF16), wd_ref[...])

    @pl.when(c == 0)
    def _():
        acc_ref[...] = part

    @pl.when(c > 0)
    def _():
        acc_ref[...] += part

    @pl.when(c == pl.num_programs(1) - 1)
    def _():
        gt2 = _mod_rows(mod_ref, 0)[5]
        sh, sc = _mod_rows(mod1_ref, 0)[:2]
        x2 = x_ref[...] + gt2 * _rms(acc_ref[...], nw_ref[0, 3:4, :])
        x2_ref[...] = x2
        h3_ref[...] = (_rms(x2, nw1_ref[0, 0:1, :]) * (1.0 + sc) + sh).astype(BF16)


def _ffn(h, x1, wg, wu, wd, nw, mod, nw1, mod1, *, tm, tf):
    n = h.shape[0]
    nwspec = pl.BlockSpec((1, 4, D_MODEL), lambda i, c: (0, 0, 0))
    modspec = pl.BlockSpec((1, 8, 6 * D_MODEL), lambda i, c: (0, 0, 0))
    return pl.pallas_call(
        _ffn_kernel,
        out_shape=(jax.ShapeDtypeStruct((n, D_MODEL), F32), jax.ShapeDtypeStruct((n, D_MODEL), BF16)),
        grid=(n // tm, D_FF // tf),
        in_specs=[pl.BlockSpec((tm, D_MODEL), lambda i, c: (i, 0)),
                  pl.BlockSpec((tm, D_MODEL), lambda i, c: (i, 0)),
                  pl.BlockSpec((D_MODEL, tf), lambda i, c: (0, c)),
                  pl.BlockSpec((D_MODEL, tf), lambda i, c: (0, c)),
                  pl.BlockSpec((tf, D_MODEL), lambda i, c: (c, 0)),
                  nwspec, modspec, nwspec, modspec],
        out_specs=(pl.BlockSpec((tm, D_MODEL), lambda i, c: (i, 0)),
                   pl.BlockSpec((tm, D_MODEL), lambda i, c: (i, 0))),
        scratch_shapes=[pltpu.VMEM((tm, D_MODEL), F32)],
        compiler_params=_cparams(("parallel", "arbitrary")),
        name="ffn",
    )(h, x1, wg, wu, wd, nw, mod, nw1, mod1)


EDGE_ROWS = 16


def _convedge_kernel(hf_ref, hl_ref, wc_ref, wu_ref, vf_ref, vl_ref):
    nt = hf_ref.shape[0]
    for h_ref, v_ref in ((hf_ref, vf_ref), (hl_ref, vl_ref)):
        h = h_ref[...].reshape(nt * EDGE_ROWS, D_MODEL)
        v = _dot(h, wc_ref[...]) * _dot(h, wu_ref[...])
        v_ref[...] = v.reshape(nt, EDGE_ROWS, D_MODEL)


def _convedge(h3, w_in_bf, *, tm):
    n = h3.shape[0]
    nt = n // tm
    h3t = h3.reshape(nt, tm, D_MODEL)
    last = tm // EDGE_ROWS - 1
    shape = jax.ShapeDtypeStruct((nt, EDGE_ROWS, D_MODEL), F32)
    return pl.pallas_call(
        _convedge_kernel,
        out_shape=(shape, shape),
        grid=(1,),
        in_specs=[pl.BlockSpec((nt, EDGE_ROWS, D_MODEL), lambda i: (0, 0, 0)),
                  pl.BlockSpec((nt, EDGE_ROWS, D_MODEL), lambda i: (0, last, 0)),
                  pl.BlockSpec((D_MODEL, D_MODEL), lambda i: (0, 1)),
                  pl.BlockSpec((D_MODEL, D_MODEL), lambda i: (0, 2))],
        out_specs=(pl.BlockSpec((nt, EDGE_ROWS, D_MODEL), lambda i: (0, 0, 0)),
                   pl.BlockSpec((nt, EDGE_ROWS, D_MODEL), lambda i: (0, 0, 0))),
        compiler_params=_cparams(("arbitrary",)),
        name="convedge",
    )(h3t, h3t, w_in_bf, w_in_bf)


def _convmix_kernel(h_ref, x_ref, vl_ref, vf_ref, win_ref, cw_ref, wout_ref, nw_ref, mod_ref, r_ref,
                    x3_ref, h4_ref, route_ref, sel_ref, rank_ref):
    i = pl.program_id(0)
    tm = h_ref.shape[0]
    _, _, gt1, sh2, sc2, _ = _mod_rows(mod_ref, 0)
    z = _dot(h_ref[...], win_ref[...])
    b = z[:, :D_MODEL]
    v = z[:, D_MODEL:2 * D_MODEL] * z[:, 2 * D_MODEL:]
    has_prev = (i > 0).astype(F32)
    has_next = (i < pl.num_programs(0) - 1).astype(F32)
    prev_row = vl_ref[0, EDGE_ROWS - 1:EDGE_ROWS, :] * has_prev
    next_row = vf_ref[0, 0:1, :] * has_next
    rows = lax.broadcasted_iota(jnp.int32, (tm, 1), 0)
    v_dn = jnp.where(rows == 0, prev_row, pltpu.roll(v, 1, axis=0))
    v_up = jnp.where(rows == tm - 1, next_row, pltpu.roll(v, tm - 1, axis=0))
    conv = v_dn * cw_ref[0, 0:1, :] + v * cw_ref[0, 1:2, :] + v_up * cw_ref[0, 2:3, :]
    y = _dot((b * conv).astype(BF16), wout_ref[...])
    x3 = x_ref[...] + gt1 * _rms(y, nw_ref[0, 1:2, :])
    x3_ref[...] = x3
    h4 = _rms(x3, nw_ref[0, 2:3, :]) * (1.0 + sc2) + sh2
    h4_ref[...] = _pack_pairs(h4)
    hhi, hlo = _split_bf16(h4)
    rhi, rlo = _split_bf16(r_ref[...])
    logits = _dot(hhi, rhi) + (_dot(hlo, rhi) + _dot(hhi, rlo))
    lane = lax.broadcasted_iota(jnp.int32, logits.shape, 1)
    logits = jnp.where(lane < N_EXPERTS, logits, -jnp.inf)
    e = jnp.exp(logits - jnp.max(logits, axis=-1, keepdims=True))
    probs = e / jnp.sum(e, axis=-1, keepdims=True)
    v1 = jnp.max(probs, axis=-1, keepdims=True)
    i1 = jnp.min(jnp.where(probs == v1, lane, LANES), axis=-1, keepdims=True)
    rest = jnp.where(lane == i1, -1.0, probs)
    v2 = jnp.max(rest, axis=-1, keepdims=True)
    i2 = jnp.min(jnp.where(rest == v2, lane, LANES), axis=-1, keepdims=True)
    tot = v1 + v2
    route_ref[...] = (jnp.where(lane == 0, i1.astype(F32), 0.0) + jnp.where(lane == 1, i2.astype(F32), 0.0)
                      + jnp.where(lane == 2, v1 / tot, 0.0) + jnp.where(lane == 3, v2 / tot, 0.0))
    sel = jnp.where(jnp.logical_or(lane == i1, lane == i2), 1.0, 0.0)
    sel_ref[...] = sel
    earlier = lax.broadcasted_iota(jnp.int32, (tm, tm), 1) < lax.broadcasted_iota(jnp.int32, (tm, tm), 0)
    rank_ref[...] = _dot(jnp.where(earlier, 1.0, 0.0).astype(BF16), sel.astype(BF16))


def _convmix(h3, x2, v_first, v_last, w_in_bf, conv_w, w_out_bf, nw, mod, router_pad, *, tm):
    n = h3.shape[0]
    nt = n // tm
    const = lambda i: (0, 0)
    lanes_shape = jax.ShapeDtypeStruct((n, LANES), F32)
    lanes_spec = pl.BlockSpec((tm, LANES), lambda i: (i, 0))
    return pl.pallas_call(
        _convmix_kernel,
        out_shape=(jax.ShapeDtypeStruct((n, D_MODEL), F32), jax.ShapeDtypeStruct((n, D_MODEL // 2), jnp.uint32),
                   lanes_shape, lanes_shape, lanes_shape),
        grid=(nt,),
        in_specs=[pl.BlockSpec((tm, D_MODEL), lambda i: (i, 0)),
                  pl.BlockSpec((tm, D_MODEL), lambda i: (i, 0)),
                  pl.BlockSpec((1, EDGE_ROWS, D_MODEL), lambda i: (jnp.maximum(i - 1, 0), 0, 0)),
                  pl.BlockSpec((1, EDGE_ROWS, D_MODEL), lambda i: (jnp.minimum(i + 1, nt - 1), 0, 0)),
                  pl.BlockSpec((D_MODEL, 3 * D_MODEL), const),
                  pl.BlockSpec((1, 3, D_MODEL), lambda i: (0, 0, 0)),
                  pl.BlockSpec((D_MODEL, D_MODEL), const),
                  pl.BlockSpec((1, 4, D_MODEL), lambda i: (0, 0, 0)),
                  pl.BlockSpec((1, 8, 6 * D_MODEL), lambda i: (0, 0, 0)),
                  pl.BlockSpec((D_MODEL, LANES), const)],
        out_specs=(pl.BlockSpec((tm, D_MODEL), lambda i: (i, 0)),
                   pl.BlockSpec((tm, D_MODEL // 2), lambda i: (i, 0)),
                   lanes_spec, lanes_spec, lanes_spec),
        compiler_params=_cparams(("parallel",)),
        name="convmix",
    )(h3, x2, v_last, v_first, w_in_bf, conv_w, w_out_bf, nw, mod, router_pad)


ROW_TILE = 512


def _count_le(sorted_vals, x):
    return jnp.sum((sorted_vals[None, :] <= x[:, None]).astype(jnp.int32), axis=1)


def _route(route, sel, lrank, n_row_tiles):
    n = sel.shape[0]
    n_tok_tiles = n // ROW_TILE
    i32 = jnp.int32
    sel8 = sel[:, :N_EXPERTS].astype(i32)
    cnt_tile = jnp.sum(sel8.reshape(n_tok_tiles, ROW_TILE, N_EXPERTS), axis=1)
    cum_end = jnp.cumsum(cnt_tile, axis=0)
    cum_beg = cum_end - cnt_tile
    cnt = cum_end[-1]
    cnt_pad = (cnt + ROW_TILE - 1) // ROW_TILE * ROW_TILE
    grp_end = jnp.cumsum(cnt_pad)
    start = grp_end - cnt_pad
    pos = start[None, :] + lrank[:, :N_EXPERTS].astype(i32) + jnp.repeat(cum_beg, ROW_TILE, axis=0)
    experts = jnp.clip(route[:, :2].astype(i32), 0, N_EXPERTS - 1)
    pos2 = jnp.take_along_axis(pos, experts, axis=1)
    pos2 = jnp.clip(pos2, 0, n_row_tiles * ROW_TILE - 1)
    r = jnp.arange(n_row_tiles, dtype=i32)
    base = r * ROW_TILE
    tile_valid = base < grp_end[-1]
    n_valid = jnp.clip(grp_end[-1] // ROW_TILE, 1, n_row_tiles)
    tile_exp = jnp.minimum(_count_le(grp_end, base), N_EXPERTS - 1)
    tile_exp = jnp.where(tile_valid, tile_exp, tile_exp[n_valid - 1])
    tile_src = jnp.where(tile_valid, r, n_valid - 1)
    tile_first = jnp.logical_and(tile_valid, base == start[tile_exp])
    tiles = (tile_exp, tile_src, tile_valid.astype(i32), tile_first.astype(i32))
    return pos2.T.reshape(-1), tiles


SC_CHUNK = 128


def _sc_workers():
    info = pltpu.get_tpu_info().sparse_core
    return info.num_cores, info.num_cores * info.num_subcores


def _sc_scatter_rows(rows, idx, n_out):
    n_src, d = rows.shape
    n_cores, n_workers = _sc_workers()
    per_worker = idx.shape[0] // n_workers
    assert idx.shape[0] % (n_workers * SC_CHUNK) == 0 and n_src % per_worker == 0
    idx2d = idx.reshape(-1, SC_CHUNK)
    mesh = plsc.VectorSubcoreMesh(core_axis_name="c", subcore_axis_name="s")

    @functools.partial(pl.kernel, mesh=mesh, out_type=jax.ShapeDtypeStruct((n_out, d), rows.dtype),
                       scratch_types=[pltpu.VMEM((1, SC_CHUNK), jnp.int32), pltpu.VMEM((SC_CHUNK, d), rows.dtype)])
    def scatter(rows_hbm, idx_hbm, out_hbm, idx_v, rows_v):
        wid = lax.axis_index("s") * n_cores + lax.axis_index("c")

        @pl.loop(0, per_worker // SC_CHUNK)
        def _(j):
            a = wid * per_worker + j * SC_CHUNK
            pltpu.sync_copy(idx_hbm.at[pl.ds(a // SC_CHUNK, 1)], idx_v)
            pltpu.sync_copy(rows_hbm.at[pl.ds(lax.rem(a, n_src), SC_CHUNK)], rows_v)
            pltpu.sync_copy(rows_v, out_hbm.at[idx_v.at[0]])

    return scatter(rows, idx2d)


def _sc_gather_rows(table, idx):
    d = table.shape[1]
    n_cores, n_workers = _sc_workers()
    per_worker = idx.shape[0] // n_workers
    assert idx.shape[0] % (n_workers * SC_CHUNK) == 0
    idx2d = idx.reshape(-1, SC_CHUNK)
    mesh = plsc.VectorSubcoreMesh(core_axis_name="c", subcore_axis_name="s")

    @functools.partial(pl.kernel, mesh=mesh, out_type=jax.ShapeDtypeStruct((idx.shape[0], d), table.dtype),
                       scratch_types=[pltpu.VMEM((1, SC_CHUNK), jnp.int32), pltpu.VMEM((SC_CHUNK, d), table.dtype)])
    def gather(table_hbm, idx_hbm, out_hbm, idx_v, rows_v):
        wid = lax.axis_index("s") * n_cores + lax.axis_index("c")

        @pl.loop(0, per_worker // SC_CHUNK)
        def _(j):
            a = wid * per_worker + j * SC_CHUNK
            pltpu.sync_copy(idx_hbm.at[pl.ds(a // SC_CHUNK, 1)], idx_v)
            pltpu.sync_copy(table_hbm.at[idx_v.at[0]], rows_v)
            pltpu.sync_copy(rows_v, out_hbm.at[pl.ds(a, SC_CHUNK)])

    return gather(table, idx2d)


def _experts_kernel(t_exp, t_src, t_valid, t_first, xs_ref, wg_ref, wu_ref, wd_ref, ys_ref,
                    wg_c, wu_c, wd_c, x_sc, acc_ref):
    r = pl.program_id(0)
    c = pl.program_id(1)
    last = c == pl.num_programs(1) - 1

    @pl.when(t_first[r] != 0)
    def _():
        wg_c[c] = wg_ref[0, 0].astype(BF16)
        wu_c[c] = wu_ref[0, 0].astype(BF16)
        wd_c[c] = wd_ref[0, 0].astype(BF16)

    @pl.when(t_valid[r] != 0)
    def _():
        @pl.when(c == 0)
        def _():
            x_sc[...] = _unpack_pairs(xs_ref[...])

        x = x_sc[...]
        g = _dot(x, wg_c[c])
        u = _dot(x, wu_c[c])
        part = _dot((g * jax.nn.sigmoid(g) * u).astype(BF16), wd_c[c])

        @pl.when(c == 0)
        def _():
            acc_ref[...] = part

        @pl.when(c > 0)
        def _():
            acc_ref[...] += part

        @pl.when(last)
        def _():
            ys_ref[...] = _pack_pairs(acc_ref[...])

    @pl.when(jnp.logical_and(t_valid[r] == 0, last))
    def _():
        ys_ref[...] = jnp.zeros(ys_ref.shape, ys_ref.dtype)


def _experts(tiles, xs, wg, wu, wd, *, tf):
    n_rows, half = xs.shape
    n_ch = D_EXPERT // tf

    def w_in(r, c, e, s, v, f):
        return (0, e[r], 0, jnp.where(f[r] != 0, c, n_ch - 1))

    def w_dn(r, c, e, s, v, f):
        return (0, e[r], jnp.where(f[r] != 0, c, n_ch - 1), 0)

    grid_spec = pltpu.PrefetchScalarGridSpec(
        num_scalar_prefetch=4, grid=(n_rows // ROW_TILE, n_ch),
        in_specs=[pl.BlockSpec((ROW_TILE, half), lambda r, c, e, s, v, f: (s[r], 0)),
                  pl.BlockSpec((1, 1, D_MODEL, tf), w_in),
                  pl.BlockSpec((1, 1, D_MODEL, tf), w_in),
                  pl.BlockSpec((1, 1, tf, D_MODEL), w_dn)],
        out_specs=pl.BlockSpec((ROW_TILE, half), lambda r, c, e, s, v, f: (r, 0)),
        scratch_shapes=[pltpu.VMEM((n_ch, D_MODEL, tf), BF16), pltpu.VMEM((n_ch, D_MODEL, tf), BF16),
                        pltpu.VMEM((n_ch, tf, D_MODEL), BF16), pltpu.VMEM((ROW_TILE, D_MODEL), BF16),
                        pltpu.VMEM((ROW_TILE, D_MODEL), F32)])
    return pl.pallas_call(
        _experts_kernel, grid_spec=grid_spec,
        out_shape=jax.ShapeDtypeStruct((n_rows, half), jnp.uint32),
        compiler_params=_cparams(("arbitrary", "arbitrary")),
        name="moe_experts",
    )(*tiles, xs, wg, wu, wd)


def _combine_kernel(a_ref, b_ref, route_ref, x_ref, nw_ref, mod_ref, o_ref):
    route = route_ref[...]
    y = route[:, 2:3] * _unpack_pairs(a_ref[...]).astype(F32) + route[:, 3:4] * _unpack_pairs(b_ref[...]).astype(F32)
    gt2 = _mod_rows(mod_ref, 0)[5]
    o_ref[...] = x_ref[...] + gt2 * _rms(y, nw_ref[0, 3:4, :])


def _combine(rows, route, x3, nw, mod):
    n = x3.shape[0]
    nt = n // ROW_TILE
    half = rows.shape[1]
    return pl.pallas_call(
        _combine_kernel,
        out_shape=jax.ShapeDtypeStruct((n, D_MODEL), F32),
        grid=(nt,),
        in_specs=[pl.BlockSpec((ROW_TILE, half), lambda i: (i, 0)),
                  pl.BlockSpec((ROW_TILE, half), lambda i: (i + nt, 0)),
                  pl.BlockSpec((ROW_TILE, LANES), lambda i: (i, 0)),
                  pl.BlockSpec((ROW_TILE, D_MODEL), lambda i: (i, 0)),
                  pl.BlockSpec((1, 4, D_MODEL), lambda i: (0, 0, 0)),
                  pl.BlockSpec((1, 8, 6 * D_MODEL), lambda i: (0, 0, 0))],
        out_specs=pl.BlockSpec((ROW_TILE, D_MODEL), lambda i: (i, 0)),
        compiler_params=_cparams(("parallel",)),
        name="moe_combine",
    )(rows, rows, route, x3, nw, mod)


def _moe(h4p, route, sel, lrank, x3, wg, wu, wd, nw, mod):
    n = h4p.shape[0]
    n_rows = 2 * n + N_EXPERTS * ROW_TILE
    pos, tiles = _route(route, sel, lrank, n_rows // ROW_TILE)
    xs = _sc_scatter_rows(h4p, pos, n_rows)
    ys = _experts(tiles, xs, wg, wu, wd, tf=512)
    rows = _sc_gather_rows(ys, pos)
    return _combine(rows, route, x3, nw, mod)


def kernel(x, c, ctx, c_ctx, ada_w, ada_b, norm_w, e_w_in, e_q_gain, e_k_gain, e_w_out, e_ffn_gate,
           e_ffn_up, e_ffn_down, o_w_in, o_conv_w, o_w_out, o_router, o_exp_gate, o_exp_up, o_exp_down):
    assert x.shape[0] == 1 and x.shape[2] == D_MODEL and ada_w.shape[0] == 2
    n = x.shape[1]
    x2d = x[0]
    ctx2d = ctx[0]
    mod = _ada(c, c_ctx, ada_w, ada_b)
    mod0, mod1 = mod[0:1], mod[1:2]
    nw0, nw1 = norm_w[0:1], norm_w[1:2]

    w_in = e_w_in[0].astype(BF16)
    scale = HEAD_DIM ** -0.5 * np.log2(np.e)
    gain = jnp.concatenate([jnp.tile(e_q_gain[0], N_Q_HEADS) * scale,
                            jnp.tile(e_k_gain[0], N_KV_HEADS)])[None, :]
    score_bound = 1.02 * HEAD_DIM * scale * jnp.max(jnp.abs(e_q_gain[0])) * jnp.max(jnp.abs(e_k_gain[0]))
    qT, k, vT, f = _evenproj(x2d, nw0, mod0, w_in, gain, latent=True, tm=512)
    kc, vcT = _evenproj(ctx2d, nw0, mod0, w_in, gain, latent=False, tm=ctx2d.shape[0])
    attnT = _attention(qT, k, vT, kc, vcT, score_bound, tq=512, tk=512)
    four = _fourier(f)
    w_out = e_w_out[0].astype(BF16)
    x1, h2 = _outproj(attnT, four, x2d, w_out[:ATTN_WIDTH], w_out[ATTN_WIDTH:], nw0, mod0, tm=512)
    x2, h3 = _ffn(h2, x1, e_ffn_gate[0].astype(BF16), e_ffn_up[0].astype(BF16),
                  e_ffn_down[0].astype(BF16), nw0, mod0, nw1, mod1, tm=1024, tf=1408)

    ow_in = o_w_in[0].astype(BF16)
    v_first, v_last = _convedge(h3, ow_in, tm=ROW_TILE)
    router_pad = jnp.pad(o_router[0], ((0, 0), (0, LANES - N_EXPERTS)))
    x3, h4p, route, sel, lrank = _convmix(h3, x2, v_first, v_last, ow_in, o_conv_w, o_w_out[0].astype(BF16),
                                        nw1, mod1, router_pad, tm=ROW_TILE)
    out = _moe(h4p, route, sel, lrank, x3, o_exp_gate, o_exp_up, o_exp_down, nw1, mod1)
    return out[None]
```

```python
import functools

import numpy as np
import jax
import jax.numpy as jnp
from jax import lax
from jax.experimental import pallas as pl
from jax.experimental.pallas import tpu as pltpu
from jax.experimental.pallas import tpu_sc as plsc

D_MODEL = 1024
GRID_W = 64
HEAD_DIM = 64
N_Q_HEADS = 12
N_KV_HEADS = 4
Q_PER_KV = N_Q_HEADS // N_KV_HEADS
ATTN_WIDTH = N_Q_HEADS * HEAD_DIM
KV_WIDTH = N_KV_HEADS * HEAD_DIM
QK_WIDTH = ATTN_WIDTH + KV_WIDTH
N_FOURIER_GROUPS = 4
FOURIER_GROUP_DIM = 64
FOURIER_WIDTH = N_FOURIER_GROUPS * FOURIER_GROUP_DIM
EVEN_IN_WIDTH = ATTN_WIDTH + 2 * KV_WIDTH + FOURIER_WIDTH
D_FF = 2816
N_EXPERTS = 8
D_EXPERT = 3584
ONES_ROWS = 16
V_ROWS = HEAD_DIM + ONES_ROWS
FAST_SCORE_BOUND = 24.0
ROPE_THETA = 10000.0
ROPE_HALF = HEAD_DIM // 4
NORM_EPS = 1e-6

LANES = 128
VMEM_LIMIT = 56 * 1024 * 1024

BF16 = jnp.bfloat16
FP8 = jnp.float8_e4m3fn
F32 = jnp.float32


def _cparams(semantics, vmem=VMEM_LIMIT):
    return pltpu.CompilerParams(dimension_semantics=semantics, vmem_limit_bytes=vmem)


def _dot(a, b):
    return jnp.dot(a, b, preferred_element_type=F32)


def _split_bf16(x):
    hi = x.astype(BF16)
    lo = (x - hi.astype(F32)).astype(BF16)
    return hi, lo


def _rms(x, g):
    return x * lax.rsqrt(jnp.mean(x * x, axis=-1, keepdims=True) + NORM_EPS) * g


def _mod_rows(mod_ref, row):
    return [mod_ref[0, row:row + 1, i * D_MODEL:(i + 1) * D_MODEL] for i in range(6)]


def _pack_pairs(x):
    k = x.shape[1] // 2
    bits = lax.bitcast_convert_type(x.astype(BF16).astype(F32), jnp.uint32)
    return (bits[:, :k] >> 16) | (bits[:, k:] & jnp.uint32(0xFFFF0000))


def _unpack_pairs(w):
    lo = lax.bitcast_convert_type(w << 16, F32).astype(BF16)
    hi = lax.bitcast_convert_type(w & jnp.uint32(0xFFFF0000), F32).astype(BF16)
    return jnp.concatenate([lo, hi], axis=1)


@functools.lru_cache(maxsize=None)
def _rope_tables(n_tokens):
    t = np.arange(n_tokens)
    row = (t // GRID_W).astype(np.float64)
    col = (t % GRID_W).astype(np.float64)
    inv = ROPE_THETA ** (-np.arange(ROPE_HALF, dtype=np.float64) / ROPE_HALF)
    ar, ac = row[:, None] * inv, col[:, None] * inv
    cos = np.concatenate([np.cos(ar), np.cos(ar), np.cos(ac), np.cos(ac)], axis=-1)
    sin = np.concatenate([-np.sin(ar), np.sin(ar), -np.sin(ac), np.sin(ac)], axis=-1)
    reps = LANES // HEAD_DIM
    return (np.tile(cos, (1, reps)).astype(np.float32), np.tile(sin, (1, reps)).astype(np.float32))


@functools.lru_cache(maxsize=None)
def _head_matrices():
    head = np.arange(QK_WIDTH) // HEAD_DIM
    red = (head[:, None] == np.arange(LANES)[None, :]).astype(np.float32) / HEAD_DIM
    exp = (np.arange(LANES)[:, None] == head[None, :]).astype(np.float32)
    return red, exp


@functools.lru_cache(maxsize=None)
def _fourier_tables(n_tokens, kb):
    n2 = LANES
    n1 = n_tokens // n2
    c = np.arange(FOURIER_GROUP_DIM)
    ang = 2 * np.pi * np.outer(c, c) / FOURIER_GROUP_DIM
    eye = np.eye(N_FOURIER_GROUPS)
    cs = np.concatenate([np.kron(eye, np.cos(ang)), np.kron(eye, np.sin(ang))], axis=1)
    k1 = np.arange(n1)
    th = 2 * np.pi * np.outer(k1, k1) / n1
    cr, ci = np.cos(th), -np.sin(th)
    base = np.block([[cr, ci], [ci, -cr]])
    psi = 2 * np.pi * np.outer(k1, np.arange(n2)) / n_tokens
    twr = np.cos(psi).reshape(n1, n2 // kb, kb).transpose(1, 0, 2)
    twi = (-np.sin(psi)).reshape(n1, n2 // kb, kb).transpose(1, 0, 2)
    k2 = np.arange(n2)
    ph = 2 * np.pi * np.outer(k2, k2) / n2
    fr, fi = np.cos(ph), -np.sin(ph)
    scale = 1.0 / np.sqrt(n_tokens * FOURIER_GROUP_DIM)
    m3 = np.stack([fr, -fi], axis=-1) * scale
    wb = np.einsum('knr,uv->kunrv', m3, np.eye(kb)).reshape(n2 * kb, n2 * 2 * kb)
    f32 = np.float32
    return cs.astype(f32), base.astype(f32), twr.astype(f32), twi.astype(f32), wb.astype(f32)


def _ada_kernel(cb_ref, w_ref, b_ref, o_ref):
    tn = o_ref.shape[-1]
    o_ref[...] = jnp.zeros(o_ref.shape, F32)
    for r in range(2):
        cb = cb_ref[r]
        s = cb * jax.nn.sigmoid(cb)
        for j in range(tn // LANES):
            sl = slice(j * LANES, (j + 1) * LANES)
            col = jnp.sum(s * w_ref[0, :, sl], axis=0, keepdims=True)
            o_ref[0, r:r + 1, sl] = col + b_ref[0, :, sl]


def _ada(c, c_ctx, ada_w, ada_b):
    depth = ada_w.shape[0]
    n = ada_w.shape[-1]
    tn = 1536
    cb = jnp.stack([jnp.broadcast_to(c[0][:, None], (D_MODEL, LANES)),
                    jnp.broadcast_to(c_ctx[:, None], (D_MODEL, LANES))])
    return pl.pallas_call(
        _ada_kernel,
        out_shape=jax.ShapeDtypeStruct((depth, 8, n), F32),
        grid=(depth, n // tn),
        in_specs=[pl.BlockSpec((2, D_MODEL, LANES), lambda i, j: (0, 0, 0)),
                  pl.BlockSpec((1, D_MODEL, tn), lambda i, j: (i, 0, j)),
                  pl.BlockSpec((1, 1, tn), lambda i, j: (i, 0, j))],
        out_specs=pl.BlockSpec((1, 8, tn), lambda i, j: (i, 0, j)),
        compiler_params=_cparams(("parallel", "parallel")),
        name="ada",
    )(cb, ada_w, ada_b[:, None, :])


def _evenproj_kernel(*refs, row, latent, qk_dtype):
    if latent:
        (x_ref, nw_ref, mod_ref, w_ref, gain_ref, red_ref, exp_ref, cos_ref, sin_ref,
         qT_ref, k_ref, vT_ref, f_ref) = refs
    else:
        x_ref, nw_ref, mod_ref, w_ref, gain_ref, red_ref, exp_ref, k_ref, vT_ref = refs
    sh, sc = _mod_rows(mod_ref, row)[:2]
    h = (_rms(x_ref[...], nw_ref[0, 0:1, :]) * (1.0 + sc) + sh).astype(BF16)
    z = _dot(h, w_ref[...])
    zqk = z[:, :QK_WIDTH]
    hi, lo = _split_bf16(zqk * zqk)
    red = red_ref[...].astype(BF16)
    ms = _dot(hi, red) + _dot(lo, red)
    rhi, rlo = _split_bf16(lax.rsqrt(ms + NORM_EPS))
    expm = exp_ref[...].astype(BF16)
    yn = zqk * (_dot(rhi, expm) + _dot(rlo, expm)) * gain_ref[...]
    if latent:
        lane = lax.broadcasted_iota(jnp.int32, (1, LANES), 1)
        first_half = (lane // ROPE_HALF) % 2 == 0
        cos, sin = cos_ref[...], sin_ref[...]
        chunks = []
        for c in range(QK_WIDTH // LANES):
            yc = yn[:, c * LANES:(c + 1) * LANES]
            partner = jnp.where(first_half, pltpu.roll(yc, LANES - ROPE_HALF, axis=1),
                                pltpu.roll(yc, ROPE_HALF, axis=1))
            chunks.append(yc * cos + partner * sin)
        yn = jnp.concatenate(chunks, axis=1)
        qT_ref[...] = yn[:, :ATTN_WIDTH].T.astype(qk_dtype)
        f_ref[...] = z[:, QK_WIDTH + KV_WIDTH:].astype(BF16)
    for g in range(N_KV_HEADS):
        k_ref[g] = yn[:, ATTN_WIDTH + g * HEAD_DIM:ATTN_WIDTH + (g + 1) * HEAD_DIM].astype(qk_dtype)
    vT = z[:, QK_WIDTH:QK_WIDTH + KV_WIDTH].T.astype(BF16)
    ones = jnp.ones((ONES_ROWS, vT.shape[1]), BF16)
    for g in range(N_KV_HEADS):
        vT_ref[g * V_ROWS:g * V_ROWS + HEAD_DIM, :] = vT[g * HEAD_DIM:(g + 1) * HEAD_DIM]
        vT_ref[g * V_ROWS + HEAD_DIM:(g + 1) * V_ROWS, :] = ones


def _evenproj(x2d, nw, mod, w_bf, gain, *, latent, tm, qk_dtype):
    n = x2d.shape[0]
    red, expm = _head_matrices()
    const = lambda i: (0, 0)
    in_specs = [pl.BlockSpec((tm, D_MODEL), lambda i: (i, 0)),
                pl.BlockSpec((1, 4, D_MODEL), lambda i: (0, 0, 0)),
                pl.BlockSpec((1, 8, 6 * D_MODEL), lambda i: (0, 0, 0)),
                pl.BlockSpec((D_MODEL, EVEN_IN_WIDTH), const),
                pl.BlockSpec((1, QK_WIDTH), const),
                pl.BlockSpec((QK_WIDTH, LANES), const),
                pl.BlockSpec((LANES, QK_WIDTH), const)]
    args = [x2d, nw, mod, w_bf, gain, jnp.asarray(red), jnp.asarray(expm)]
    k_shape = jax.ShapeDtypeStruct((N_KV_HEADS, n, HEAD_DIM), qk_dtype)
    vT_shape = jax.ShapeDtypeStruct((N_KV_HEADS * V_ROWS, n), BF16)
    k_spec = pl.BlockSpec((N_KV_HEADS, tm, HEAD_DIM), lambda i: (0, i, 0))
    vT_spec = pl.BlockSpec((N_KV_HEADS * V_ROWS, tm), lambda i: (0, i))
    if latent:
        cos, sin = _rope_tables(n)
        in_specs += [pl.BlockSpec((tm, LANES), lambda i: (i, 0))] * 2
        args += [jnp.asarray(cos), jnp.asarray(sin)]
        out_shape = (jax.ShapeDtypeStruct((ATTN_WIDTH, n), qk_dtype), k_shape, vT_shape,
                     jax.ShapeDtypeStruct((n, FOURIER_WIDTH), BF16))
        out_specs = (pl.BlockSpec((ATTN_WIDTH, tm), lambda i: (0, i)), k_spec, vT_spec,
                     pl.BlockSpec((tm, FOURIER_WIDTH), lambda i: (i, 0)))
    else:
        out_shape = (k_shape, vT_shape)
        out_specs = (k_spec, vT_spec)
    return pl.pallas_call(
        functools.partial(_evenproj_kernel, row=0 if latent else 1, latent=latent, qk_dtype=qk_dtype),
        out_shape=out_shape, grid=(n // tm,), in_specs=in_specs, out_specs=out_specs,
        compiler_params=_cparams(("parallel",)),
        name="evenproj_lat" if latent else "evenproj_ctx",
    )(*args)


def _visit_all(visit, k_ref, vT_ref, kc_ref, vcT_ref, tk):
    def body(c, carry):
        off = pl.multiple_of(c * tk, tk)
        visit(k_ref[0, pl.ds(off, tk), :], vT_ref[:, pl.ds(off, tk)])
        return carry

    lax.fori_loop(0, k_ref.shape[1] // tk, body, 0)
    visit(kc_ref[0], vcT_ref[...])


def _attn_bounded_kernel(qT_ref, k_ref, vT_ref, kc_ref, vcT_ref, o_ref, acc_sc, s_sc, *, tk):
    n_tiles = k_ref.shape[1] // tk
    acc_sc[...] = jnp.zeros(acc_sc.shape, F32)

    def q(j):
        return qT_ref[j * HEAD_DIM:(j + 1) * HEAD_DIM, :]

    def keys(c):
        return k_ref[0, pl.ds(pl.multiple_of(c * tk, tk), tk), :]

    def consume(j, s, vt):
        acc_sc[j] += _dot(vt, jnp.exp2(s).astype(BF16))

    s_sc[...] = _dot(keys(0), q(0))

    def body(c, carry):
        kt = keys(c)
        vt = vT_ref[:, pl.ds(pl.multiple_of(c * tk, tk), tk)]
        s = s_sc[...]
        for j in range(Q_PER_KV):
            if j + 1 < Q_PER_KV:
                s_next = _dot(kt, q(j + 1))
            else:
                s_next = _dot(keys(jnp.minimum(c + 1, n_tiles - 1)), q(0))
            consume(j, s, vt)
            s = s_next
        s_sc[...] = s
        return carry

    lax.fori_loop(0, n_tiles, body, 0, unroll=8)
    kc, vc = kc_ref[0], vcT_ref[...]
    s = _dot(kc, q(0))
    for j in range(Q_PER_KV):
        s_next = _dot(kc, q(j + 1)) if j + 1 < Q_PER_KV else None
        consume(j, s, vc)
        s = s_next
    for j in range(Q_PER_KV):
        acc = acc_sc[j]
        o_ref[j * HEAD_DIM:(j + 1) * HEAD_DIM, :] = (acc[:HEAD_DIM] / acc[HEAD_DIM:HEAD_DIM + 1]).astype(BF16)


def _attn_online_kernel(qT_ref, k_ref, vT_ref, kc_ref, vcT_ref, o_ref, m_sc, acc_sc, *, tk):
    m_sc[...] = jnp.full(m_sc.shape, -jnp.inf, F32)
    acc_sc[...] = jnp.zeros(acc_sc.shape, F32)

    def visit(kt, vt):
        for j in range(Q_PER_KV):
            s = _dot(kt, qT_ref[j * HEAD_DIM:(j + 1) * HEAD_DIM, :])
            m_old = m_sc[j]
            m_new = jnp.maximum(m_old, jnp.max(s, axis=0, keepdims=True))
            p = jnp.exp2(s - m_new).astype(BF16)
            acc_sc[j] = jnp.exp2(m_old - m_new) * acc_sc[j] + _dot(vt, p)
            m_sc[j] = m_new

    _visit_all(visit, k_ref, vT_ref, kc_ref, vcT_ref, tk)
    for j in range(Q_PER_KV):
        acc = acc_sc[j]
        o_ref[j * HEAD_DIM:(j + 1) * HEAD_DIM, :] = (acc[:HEAD_DIM] / acc[HEAD_DIM:HEAD_DIM + 1]).astype(BF16)


def _attention(qT, k, vT, kc, vcT, *, bounded, tq, tk):
    n = qT.shape[1]
    n_ctx = kc.shape[1]
    gw = Q_PER_KV * HEAD_DIM
    common = dict(
        out_shape=jax.ShapeDtypeStruct((ATTN_WIDTH, n), BF16),
        grid=(N_KV_HEADS, n // tq),
        in_specs=[pl.BlockSpec((gw, tq), lambda g, i: (g, i)),
                  pl.BlockSpec((1, n, HEAD_DIM), lambda g, i: (g, 0, 0)),
                  pl.BlockSpec((V_ROWS, n), lambda g, i: (g, 0)),
                  pl.BlockSpec((1, n_ctx, HEAD_DIM), lambda g, i: (g, 0, 0)),
                  pl.BlockSpec((V_ROWS, n_ctx), lambda g, i: (g, 0))],
        out_specs=pl.BlockSpec((gw, tq), lambda g, i: (g, i)),
        compiler_params=_cparams(("parallel", "parallel")),
    )
    acc = pltpu.VMEM((Q_PER_KV, V_ROWS, tq), F32)
    if bounded:
        call = pl.pallas_call(functools.partial(_attn_bounded_kernel, tk=tk),
                              scratch_shapes=[acc, pltpu.VMEM((tk, tq), F32)], name="attn_bounded", **common)
    else:
        call = pl.pallas_call(functools.partial(_attn_online_kernel, tk=tk),
                              scratch_shapes=[pltpu.VMEM((Q_PER_KV, 1, tq), F32), acc],
                              name="attn_online", **common)
    return call(qT, k, vT, kc, vcT)


def _four_a_kernel(f_ref, cs_ref, base_ref, twr_ref, twi_ref, y_ref, *, nb):
    n1 = f_ref.shape[0]
    cs = cs_ref[...].astype(BF16)
    base = base_ref[...].astype(BF16)
    for u in range(nb):
        xb = f_ref[:, u * FOURIER_WIDTH:(u + 1) * FOURIER_WIDTH]
        ab = _dot(xb, cs)
        stacked = jnp.concatenate([ab[:, :FOURIER_WIDTH], ab[:, FOURIER_WIDTH:]], axis=0)
        p = _dot(base, stacked.astype(BF16))
        pr, pi = p[:n1], p[n1:]
        tr = twr_ref[0, :, u:u + 1]
        ti = twi_ref[0, :, u:u + 1]
        y_ref[u, 0] = tr * pr - ti * pi
        y_ref[u, 1] = tr * pi + ti * pr


def _four_b_kernel(y_ref, wb_ref, o_ref):
    n2, _, kb, w = y_ref.shape
    y = y_ref[...].reshape(n2 * 2 * kb, w).astype(BF16)
    o_ref[...] = _dot(wb_ref[...].astype(BF16), y).reshape(n2, kb, w)


def _fourier(f):
    n = f.shape[0]
    n2 = LANES
    n1 = n // n2
    nb = kb = 8
    cs, base, twr, twi, wb = (jnp.asarray(t) for t in _fourier_tables(n, kb))
    f2d = f.reshape(n1, n2 * FOURIER_WIDTH)
    y = pl.pallas_call(
        functools.partial(_four_a_kernel, nb=nb),
        out_shape=jax.ShapeDtypeStruct((n2, 2, n1, FOURIER_WIDTH), F32),
        grid=(n2 // nb,),
        in_specs=[pl.BlockSpec((n1, nb * FOURIER_WIDTH), lambda s: (0, s)),
                  pl.BlockSpec(cs.shape, lambda s: (0, 0)),
                  pl.BlockSpec(base.shape, lambda s: (0, 0)),
                  pl.BlockSpec((1, n1, nb), lambda s: (s, 0, 0)),
                  pl.BlockSpec((1, n1, nb), lambda s: (s, 0, 0))],
        out_specs=pl.BlockSpec((nb, 2, n1, FOURIER_WIDTH), lambda s: (s, 0, 0, 0)),
        compiler_params=_cparams(("parallel",)),
        name="four_a",
    )(f2d, cs, base, twr, twi)
    out = pl.pallas_call(
        _four_b_kernel,
        out_shape=jax.ShapeDtypeStruct((n2, n1, FOURIER_WIDTH), F32),
        grid=(n1 // kb,),
        in_specs=[pl.BlockSpec((n2, 2, kb, FOURIER_WIDTH), lambda s: (0, 0, s, 0)),
                  pl.BlockSpec(wb.shape, lambda s: (0, 0))],
        out_specs=pl.BlockSpec((n2, kb, FOURIER_WIDTH), lambda s: (0, s, 0)),
        compiler_params=_cparams(("parallel",)),
        name="four_b",
    )(y, wb)
    return out.reshape(n, FOURIER_WIDTH)


def _outproj_kernel(aT_ref, four_ref, x_ref, wa_ref, wf_ref, nw_ref, mod_ref, x1_ref, h_ref):
    _, _, gt1, sh2, sc2, _ = _mod_rows(mod_ref, 0)
    y = lax.dot_general(aT_ref[...], wa_ref[...], (((0,), (0,)), ((), ())), preferred_element_type=F32)
    y = y + _dot(four_ref[...].astype(BF16), wf_ref[...])
    x1 = x_ref[...] + gt1 * _rms(y, nw_ref[0, 1:2, :])
    x1_ref[...] = x1
    h_ref[...] = (_rms(x1, nw_ref[0, 2:3, :]) * (1.0 + sc2) + sh2).astype(BF16)


def _outproj(attnT, four, x2d, wa, wf, nw, mod, *, tm):
    n = x2d.shape[0]
    const = lambda i: (0, 0)
    return pl.pallas_call(
        _outproj_kernel,
        out_shape=(jax.ShapeDtypeStruct((n, D_MODEL), F32), jax.ShapeDtypeStruct((n, D_MODEL), BF16)),
        grid=(n // tm,),
        in_specs=[pl.BlockSpec((ATTN_WIDTH, tm), lambda i: (0, i)),
                  pl.BlockSpec((tm, FOURIER_WIDTH), lambda i: (i, 0)),
                  pl.BlockSpec((tm, D_MODEL), lambda i: (i, 0)),
                  pl.BlockSpec((ATTN_WIDTH, D_MODEL), const),
                  pl.BlockSpec((FOURIER_WIDTH, D_MODEL), const),
                  pl.BlockSpec((1, 4, D_MODEL), lambda i: (0, 0, 0)),
                  pl.BlockSpec((1, 8, 6 * D_MODEL), lambda i: (0, 0, 0))],
        out_specs=(pl.BlockSpec((tm, D_MODEL), lambda i: (i, 0)),
                   pl.BlockSpec((tm, D_MODEL), lambda i: (i, 0))),
        compiler_params=_cparams(("parallel",)),
        name="outproj",
    )(attnT, four, x2d, wa, wf, nw, mod)


def _ffn_kernel(h_ref, x_ref, wg_ref, wu_ref, wd_ref, nw_ref, mod_ref, nw1_ref, mod1_ref,
                x2_ref, h3_ref, acc_ref):
    c = pl.program_id(1)
    h = h_ref[...]
    g = _dot(h, wg_ref[...])
    u = _dot(h, wu_ref[...])
    part = _dot((g * jax.nn.sigmoid(g) * u).astype(BF16), wd_ref[...])

    @pl.when(c == 0)
    def _():
        acc_ref[...] = part

    @pl.when(c > 0)
    def _():
        acc_ref[...] += part

    @pl.when(c == pl.num_programs(1) - 1)
    def _():
        gt2 = _mod_rows(mod_ref, 0)[5]
        sh, sc = _mod_rows(mod1_ref, 0)[:2]
        x2 = x_ref[...] + gt2 * _rms(acc_ref[...], nw_ref[0, 3:4, :])
        x2_ref[...] = x2
        h3_ref[...] = (_rms(x2, nw1_ref[0, 0:1, :]) * (1.0 + sc) + sh).astype(BF16)


def _ffn(h, x1, wg, wu, wd, nw, mod, nw1, mod1, *, tm, tf):
    n = h.shape[0]
    nwspec = pl.BlockSpec((1, 4, D_MODEL), lambda i, c: (0, 0, 0))
    modspec = pl.BlockSpec((1, 8, 6 * D_MODEL), lambda i, c: (0, 0, 0))
    return pl.pallas_call(
        _ffn_kernel,
        out_shape=(jax.ShapeDtypeStruct((n, D_MODEL), F32), jax.ShapeDtypeStruct((n, D_MODEL), BF16)),
        grid=(n // tm, D_FF // tf),
        in_specs=[pl.BlockSpec((tm, D_MODEL), lambda i, c: (i, 0)),
                  pl.BlockSpec((tm, D_MODEL), lambda i, c: (i, 0)),
                  pl.BlockSpec((D_MODEL, tf), lambda i, c: (0, c)),
                  pl.BlockSpec((D_MODEL, tf), lambda i, c: (0, c)),
                  pl.BlockSpec((tf, D_MODEL), lambda i, c: (c, 0)),
                  nwspec, modspec, nwspec, modspec],
        out_specs=(pl.BlockSpec((tm, D_MODEL), lambda i, c: (i, 0)),
                   pl.BlockSpec((tm, D_MODEL), lambda i, c: (i, 0))),
        scratch_shapes=[pltpu.VMEM((tm, D_MODEL), F32)],
        compiler_params=_cparams(("parallel", "arbitrary")),
        name="ffn",
    )(h, x1, wg, wu, wd, nw, mod, nw1, mod1)


EDGE_ROWS = 16


def _convedge_kernel(hf_ref, hl_ref, wc_ref, wu_ref, vf_ref, vl_ref):
    nt = hf_ref.shape[0]
    for h_ref, v_ref in ((hf_ref, vf_ref), (hl_ref, vl_ref)):
        h = h_ref[...].reshape(nt * EDGE_ROWS, D_MODEL)
        v = _dot(h, wc_ref[...]) * _dot(h, wu_ref[...])
        v_ref[...] = v.reshape(nt, EDGE_ROWS, D_MODEL)


def _convedge(h3, w_in_bf, *, tm):
    n = h3.shape[0]
    nt = n // tm
    h3t = h3.reshape(nt, tm, D_MODEL)
    last = tm // EDGE_ROWS - 1
    shape = jax.ShapeDtypeStruct((nt, EDGE_ROWS, D_MODEL), F32)
    return pl.pallas_call(
        _convedge_kernel,
        out_shape=(shape, shape),
        grid=(1,),
        in_specs=[pl.BlockSpec((nt, EDGE_ROWS, D_MODEL), lambda i: (0, 0, 0)),
                  pl.BlockSpec((nt, EDGE_ROWS, D_MODEL), lambda i: (0, last, 0)),
                  pl.BlockSpec((D_MODEL, D_MODEL), lambda i: (0, 1)),
                  pl.BlockSpec((D_MODEL, D_MODEL), lambda i: (0, 2))],
        out_specs=(pl.BlockSpec((nt, EDGE_ROWS, D_MODEL), lambda i: (0, 0, 0)),
                   pl.BlockSpec((nt, EDGE_ROWS, D_MODEL), lambda i: (0, 0, 0))),
        compiler_params=_cparams(("arbitrary",)),
        name="convedge",
    )(h3t, h3t, w_in_bf, w_in_bf)


def _convmix_kernel(h_ref, x_ref, vl_ref, vf_ref, win_ref, cw_ref, wout_ref, nw_ref, mod_ref, r_ref,
                    x3_ref, h4_ref, route_ref, sel_ref, rank_ref):
    i = pl.program_id(0)
    tm = h_ref.shape[0]
    _, _, gt1, sh2, sc2, _ = _mod_rows(mod_ref, 0)
    z = _dot(h_ref[...], win_ref[...])
    b = z[:, :D_MODEL]
    v = z[:, D_MODEL:2 * D_MODEL] * z[:, 2 * D_MODEL:]
    has_prev = (i > 0).astype(F32)
    has_next = (i < pl.num_programs(0) - 1).astype(F32)
    prev_row = vl_ref[0, EDGE_ROWS - 1:EDGE_ROWS, :] * has_prev
    next_row = vf_ref[0, 0:1, :] * has_next
    rows = lax.broadcasted_iota(jnp.int32, (tm, 1), 0)
    v_dn = jnp.where(rows == 0, prev_row, pltpu.roll(v, 1, axis=0))
    v_up = jnp.where(rows == tm - 1, next_row, pltpu.roll(v, tm - 1, axis=0))
    conv = v_dn * cw_ref[0, 0:1, :] + v * cw_ref[0, 1:2, :] + v_up * cw_ref[0, 2:3, :]
    y = _dot((b * conv).astype(BF16), wout_ref[...])
    x3 = x_ref[...] + gt1 * _rms(y, nw_ref[0, 1:2, :])
    x3_ref[...] = x3
    h4 = _rms(x3, nw_ref[0, 2:3, :]) * (1.0 + sc2) + sh2
    h4_ref[...] = _pack_pairs(h4)
    hhi, hlo = _split_bf16(h4)
    rhi, rlo = _split_bf16(r_ref[...])
    logits = _dot(hhi, rhi) + (_dot(hlo, rhi) + _dot(hhi, rlo))
    lane = lax.broadcasted_iota(jnp.int32, logits.shape, 1)
    logits = jnp.where(lane < N_EXPERTS, logits, -jnp.inf)
    e = jnp.exp(logits - jnp.max(logits, axis=-1, keepdims=True))
    probs = e / jnp.sum(e, axis=-1, keepdims=True)
    v1 = jnp.max(probs, axis=-1, keepdims=True)
    i1 = jnp.min(jnp.where(probs == v1, lane, LANES), axis=-1, keepdims=True)
    rest = jnp.where(lane == i1, -1.0, probs)
    v2 = jnp.max(rest, axis=-1, keepdims=True)
    i2 = jnp.min(jnp.where(rest == v2, lane, LANES), axis=-1, keepdims=True)
    tot = v1 + v2
    route_ref[...] = (jnp.where(lane == 0, i1.astype(F32), 0.0) + jnp.where(lane == 1, i2.astype(F32), 0.0)
                      + jnp.where(lane == 2, v1 / tot, 0.0) + jnp.where(lane == 3, v2 / tot, 0.0))
    sel = jnp.where(jnp.logical_or(lane == i1, lane == i2), 1.0, 0.0)
    sel_ref[...] = sel
    earlier = lax.broadcasted_iota(jnp.int32, (tm, tm), 1) < lax.broadcasted_iota(jnp.int32, (tm, tm), 0)
    rank_ref[...] = _dot(jnp.where(earlier, 1.0, 0.0).astype(BF16), sel.astype(BF16))


def _convmix(h3, x2, v_first, v_last, w_in_bf, conv_w, w_out_bf, nw, mod, router_pad, *, tm):
    n = h3.shape[0]
    nt = n // tm
    const = lambda i: (0, 0)
    lanes_shape = jax.ShapeDtypeStruct((n, LANES), F32)
    lanes_spec = pl.BlockSpec((tm, LANES), lambda i: (i, 0))
    return pl.pallas_call(
        _convmix_kernel,
        out_shape=(jax.ShapeDtypeStruct((n, D_MODEL), F32), jax.ShapeDtypeStruct((n, D_MODEL // 2), jnp.uint32),
                   lanes_shape, lanes_shape, lanes_shape),
        grid=(nt,),
        in_specs=[pl.BlockSpec((tm, D_MODEL), lambda i: (i, 0)),
                  pl.BlockSpec((tm, D_MODEL), lambda i: (i, 0)),
                  pl.BlockSpec((1, EDGE_ROWS, D_MODEL), lambda i: (jnp.maximum(i - 1, 0), 0, 0)),
                  pl.BlockSpec((1, EDGE_ROWS, D_MODEL), lambda i: (jnp.minimum(i + 1, nt - 1), 0, 0)),
                  pl.BlockSpec((D_MODEL, 3 * D_MODEL), const),
                  pl.BlockSpec((1, 3, D_MODEL), lambda i: (0, 0, 0)),
                  pl.BlockSpec((D_MODEL, D_MODEL), const),
                  pl.BlockSpec((1, 4, D_MODEL), lambda i: (0, 0, 0)),
                  pl.BlockSpec((1, 8, 6 * D_MODEL), lambda i: (0, 0, 0)),
                  pl.BlockSpec((D_MODEL, LANES), const)],
        out_specs=(pl.BlockSpec((tm, D_MODEL), lambda i: (i, 0)),
                   pl.BlockSpec((tm, D_MODEL // 2), lambda i: (i, 0)),
                   lanes_spec, lanes_spec, lanes_spec),
        compiler_params=_cparams(("parallel",)),
        name="convmix",
    )(h3, x2, v_last, v_first, w_in_bf, conv_w, w_out_bf, nw, mod, router_pad)


ROW_TILE = 512


def _count_le(sorted_vals, x):
    return jnp.sum((sorted_vals[None, :] <= x[:, None]).astype(jnp.int32), axis=1)


def _route(route, sel, lrank, n_row_tiles):
    n = sel.shape[0]
    n_tok_tiles = n // ROW_TILE
    i32 = jnp.int32
    sel8 = sel[:, :N_EXPERTS].astype(i32)
    cnt_tile = jnp.sum(sel8.reshape(n_tok_tiles, ROW_TILE, N_EXPERTS), axis=1)
    cum_end = jnp.cumsum(cnt_tile, axis=0)
    cum_beg = cum_end - cnt_tile
    cnt = cum_end[-1]
    cnt_pad = (cnt + ROW_TILE - 1) // ROW_TILE * ROW_TILE
    grp_end = jnp.cumsum(cnt_pad)
    start = grp_end - cnt_pad
    pos = start[None, :] + lrank[:, :N_EXPERTS].astype(i32) + jnp.repeat(cum_beg, ROW_TILE, axis=0)
    experts = jnp.clip(route[:, :2].astype(i32), 0, N_EXPERTS - 1)
    pos2 = jnp.take_along_axis(pos, experts, axis=1)
    pos2 = jnp.clip(pos2, 0, n_row_tiles * ROW_TILE - 1)
    r = jnp.arange(n_row_tiles, dtype=i32)
    base = r * ROW_TILE
    tile_valid = base < grp_end[-1]
    n_valid = jnp.clip(grp_end[-1] // ROW_TILE, 1, n_row_tiles)
    tile_exp = jnp.minimum(_count_le(grp_end, base), N_EXPERTS - 1)
    tile_exp = jnp.where(tile_valid, tile_exp, tile_exp[n_valid - 1])
    tile_src = jnp.where(tile_valid, r, n_valid - 1)
    tile_first = jnp.logical_and(tile_valid, base == start[tile_exp])
    tiles = (tile_exp, tile_src, tile_valid.astype(i32), tile_first.astype(i32))
    return pos2.T.reshape(-1), tiles


SC_CHUNK = 128


def _sc_workers():
    info = pltpu.get_tpu_info().sparse_core
    return info.num_cores, info.num_cores * info.num_subcores


def _sc_scatter_rows(rows, idx, n_out):
    n_src, d = rows.shape
    n_cores, n_workers = _sc_workers()
    per_worker = idx.shape[0] // n_workers
    assert idx.shape[0] % (n_workers * SC_CHUNK) == 0 and n_src % per_worker == 0
    idx2d = idx.reshape(-1, SC_CHUNK)
    mesh = plsc.VectorSubcoreMesh(core_axis_name="c", subcore_axis_name="s")

    @functools.partial(pl.kernel, mesh=mesh, out_type=jax.ShapeDtypeStruct((n_out, d), rows.dtype),
                       scratch_types=[pltpu.VMEM((1, SC_CHUNK), jnp.int32), pltpu.VMEM((SC_CHUNK, d), rows.dtype)])
    def scatter(rows_hbm, idx_hbm, out_hbm, idx_v, rows_v):
        wid = lax.axis_index("s") * n_cores + lax.axis_index("c")

        @pl.loop(0, per_worker // SC_CHUNK)
        def _(j):
            a = wid * per_worker + j * SC_CHUNK
            pltpu.sync_copy(idx_hbm.at[pl.ds(a // SC_CHUNK, 1)], idx_v)
            pltpu.sync_copy(rows_hbm.at[pl.ds(lax.rem(a, n_src), SC_CHUNK)], rows_v)
            pltpu.sync_copy(rows_v, out_hbm.at[idx_v.at[0]])

    return scatter(rows, idx2d)


def _sc_gather_rows(table, idx):
    d = table.shape[1]
    n_cores, n_workers = _sc_workers()
    per_worker = idx.shape[0] // n_workers
    assert idx.shape[0] % (n_workers * SC_CHUNK) == 0
    idx2d = idx.reshape(-1, SC_CHUNK)
    mesh = plsc.VectorSubcoreMesh(core_axis_name="c", subcore_axis_name="s")

    @functools.partial(pl.kernel, mesh=mesh, out_type=jax.ShapeDtypeStruct((idx.shape[0], d), table.dtype),
                       scratch_types=[pltpu.VMEM((1, SC_CHUNK), jnp.int32), pltpu.VMEM((SC_CHUNK, d), table.dtype)])
    def gather(table_hbm, idx_hbm, out_hbm, idx_v, rows_v):
        wid = lax.axis_index("s") * n_cores + lax.axis_index("c")

        @pl.loop(0, per_worker // SC_CHUNK)
        def _(j):
            a = wid * per_worker + j * SC_CHUNK
            pltpu.sync_copy(idx_hbm.at[pl.ds(a // SC_CHUNK, 1)], idx_v)
            pltpu.sync_copy(table_hbm.at[idx_v.at[0]], rows_v)
            pltpu.sync_copy(rows_v, out_hbm.at[pl.ds(a, SC_CHUNK)])

    return gather(table, idx2d)


def _experts_kernel(t_exp, t_src, t_valid, t_first, xs_ref, wg_ref, wu_ref, wd_ref, ys_ref,
                    wg_c, wu_c, wd_c, x_sc, acc_ref):
    r = pl.program_id(0)
    c = pl.program_id(1)
    last = c == pl.num_programs(1) - 1

    @pl.when(t_first[r] != 0)
    def _():
        wg_c[c] = wg_ref[0, 0].astype(BF16)
        wu_c[c] = wu_ref[0, 0].astype(BF16)
        wd_c[c] = wd_ref[0, 0].astype(BF16)

    @pl.when(t_valid[r] != 0)
    def _():
        @pl.when(c == 0)
        def _():
            x_sc[...] = _unpack_pairs(xs_ref[...])

        x = x_sc[...]
        g = _dot(x, wg_c[c])
        u = _dot(x, wu_c[c])
        part = _dot((g * jax.nn.sigmoid(g) * u).astype(BF16), wd_c[c])

        @pl.when(c == 0)
        def _():
            acc_ref[...] = part

        @pl.when(c > 0)
        def _():
            acc_ref[...] += part

        @pl.when(last)
        def _():
            ys_ref[...] = _pack_pairs(acc_ref[...])

    @pl.when(jnp.logical_and(t_valid[r] == 0, last))
    def _():
        ys_ref[...] = jnp.zeros(ys_ref.shape, ys_ref.dtype)


def _experts(tiles, xs, wg, wu, wd, *, tf):
    n_rows, half = xs.shape
    n_ch = D_EXPERT // tf

    def w_in(r, c, e, s, v, f):
        return (0, e[r], 0, jnp.where(f[r] != 0, c, n_ch - 1))

    def w_dn(r, c, e, s, v, f):
        return (0, e[r], jnp.where(f[r] != 0, c, n_ch - 1), 0)

    grid_spec = pltpu.PrefetchScalarGridSpec(
        num_scalar_prefetch=4, grid=(n_rows // ROW_TILE, n_ch),
        in_specs=[pl.BlockSpec((ROW_TILE, half), lambda r, c, e, s, v, f: (s[r], 0)),
                  pl.BlockSpec((1, 1, D_MODEL, tf), w_in),
                  pl.BlockSpec((1, 1, D_MODEL, tf), w_in),
                  pl.BlockSpec((1, 1, tf, D_MODEL), w_dn)],
        out_specs=pl.BlockSpec((ROW_TILE, half), lambda r, c, e, s, v, f: (r, 0)),
        scratch_shapes=[pltpu.VMEM((n_ch, D_MODEL, tf), BF16), pltpu.VMEM((n_ch, D_MODEL, tf), BF16),
                        pltpu.VMEM((n_ch, tf, D_MODEL), BF16), pltpu.VMEM((ROW_TILE, D_MODEL), BF16),
                        pltpu.VMEM((ROW_TILE, D_MODEL), F32)])
    return pl.pallas_call(
        _experts_kernel, grid_spec=grid_spec,
        out_shape=jax.ShapeDtypeStruct((n_rows, half), jnp.uint32),
        compiler_params=_cparams(("arbitrary", "arbitrary")),
        name="moe_experts",
    )(*tiles, xs, wg, wu, wd)


def _combine_kernel(a_ref, b_ref, route_ref, x_ref, nw_ref, mod_ref, o_ref):
    route = route_ref[...]
    y = route[:, 2:3] * _unpack_pairs(a_ref[...]).astype(F32) + route[:, 3:4] * _unpack_pairs(b_ref[...]).astype(F32)
    gt2 = _mod_rows(mod_ref, 0)[5]
    o_ref[...] = x_ref[...] + gt2 * _rms(y, nw_ref[0, 3:4, :])


def _combine(rows, route, x3, nw, mod):
    n = x3.shape[0]
    nt = n // ROW_TILE
    half = rows.shape[1]
    return pl.pallas_call(
        _combine_kernel,
        out_shape=jax.ShapeDtypeStruct((n, D_MODEL), F32),
        grid=(nt,),
        in_specs=[pl.BlockSpec((ROW_TILE, half), lambda i: (i, 0)),
                  pl.BlockSpec((ROW_TILE, half), lambda i: (i + nt, 0)),
                  pl.BlockSpec((ROW_TILE, LANES), lambda i: (i, 0)),
                  pl.BlockSpec((ROW_TILE, D_MODEL), lambda i: (i, 0)),
                  pl.BlockSpec((1, 4, D_MODEL), lambda i: (0, 0, 0)),
                  pl.BlockSpec((1, 8, 6 * D_MODEL), lambda i: (0, 0, 0))],
        out_specs=pl.BlockSpec((ROW_TILE, D_MODEL), lambda i: (i, 0)),
        compiler_params=_cparams(("parallel",)),
        name="moe_combine",
    )(rows, rows, route, x3, nw, mod)


def _moe(h4p, route, sel, lrank, x3, wg, wu, wd, nw, mod):
    n = h4p.shape[0]
    n_rows = 2 * n + N_EXPERTS * ROW_TILE
    pos, tiles = _route(route, sel, lrank, n_rows // ROW_TILE)
    xs = _sc_scatter_rows(h4p, pos, n_rows)
    ys = _experts(tiles, xs, wg, wu, wd, tf=512)
    rows = _sc_gather_rows(ys, pos)
    return _combine(rows, route, x3, nw, mod)


def kernel(x, c, ctx, c_ctx, ada_w, ada_b, norm_w, e_w_in, e_q_gain, e_k_gain, e_w_out, e_ffn_gate,
           e_ffn_up, e_ffn_down, o_w_in, o_conv_w, o_w_out, o_router, o_exp_gate, o_exp_up, o_exp_down):
    assert x.shape[0] == 1 and x.shape[2] == D_MODEL and ada_w.shape[0] == 2
    n = x.shape[1]
    x2d = x[0]
    ctx2d = ctx[0]
    mod = _ada(c, c_ctx, ada_w, ada_b)
    mod0, mod1 = mod[0:1], mod[1:2]
    nw0, nw1 = norm_w[0:1], norm_w[1:2]

    w_in = e_w_in[0].astype(BF16)
    scale = HEAD_DIM ** -0.5 * np.log2(np.e)
    gain = jnp.concatenate([jnp.tile(e_q_gain[0], N_Q_HEADS) * scale,
                            jnp.tile(e_k_gain[0], N_KV_HEADS)])[None, :]
    score_bound = 1.02 * HEAD_DIM * scale * jnp.max(jnp.abs(e_q_gain[0])) * jnp.max(jnp.abs(e_k_gain[0]))

    def mixer_inputs(fast):
        qk_dtype = FP8 if fast else BF16
        qT, k, vT, f = _evenproj(x2d, nw0, mod0, w_in, gain, latent=True, tm=512, qk_dtype=qk_dtype)
        kc, vcT = _evenproj(ctx2d, nw0, mod0, w_in, gain, latent=False, tm=ctx2d.shape[0], qk_dtype=qk_dtype)
        return _attention(qT, k, vT, kc, vcT, bounded=fast, tq=256, tk=1024), f

    attnT, f = lax.cond(score_bound <= FAST_SCORE_BOUND, functools.partial(mixer_inputs, True),
                        functools.partial(mixer_inputs, False))
    four = _fourier(f)
    w_out = e_w_out[0].astype(BF16)
    x1, h2 = _outproj(attnT, four, x2d, w_out[:ATTN_WIDTH], w_out[ATTN_WIDTH:], nw0, mod0, tm=512)
    x2, h3 = _ffn(h2, x1, e_ffn_gate[0].astype(BF16), e_ffn_up[0].astype(BF16),
                  e_ffn_down[0].astype(BF16), nw0, mod0, nw1, mod1, tm=1024, tf=1408)

    ow_in = o_w_in[0].astype(BF16)
    v_first, v_last = _convedge(h3, ow_in, tm=ROW_TILE)
    router_pad = jnp.pad(o_router[0], ((0, 0), (0, LANES - N_EXPERTS)))
    x3, h4p, route, sel, lrank = _convmix(h3, x2, v_first, v_last, ow_in, o_conv_w, o_w_out[0].astype(BF16),
                                        nw1, mod1, router_pad, tm=ROW_TILE)
    out = _moe(h4p, route, sel, lrank, x3, o_exp_gate, o_exp_up, o_exp_down, nw1, mod1)
    return out[None]
```

```python
import functools

import numpy as np
import jax
import jax.numpy as jnp
from jax import lax
from jax.experimental import pallas as pl
from jax.experimental.pallas import tpu as pltpu
from jax.experimental.pallas import tpu_sc as plsc

D_MODEL = 1024
GRID_W = 64
HEAD_DIM = 64
N_Q_HEADS = 12
N_KV_HEADS = 4
Q_PER_KV = N_Q_HEADS // N_KV_HEADS
ATTN_WIDTH = N_Q_HEADS * HEAD_DIM
KV_WIDTH = N_KV_HEADS * HEAD_DIM
QK_WIDTH = ATTN_WIDTH + KV_WIDTH
N_FOURIER_GROUPS = 4
FOURIER_GROUP_DIM = 64
FOURIER_WIDTH = N_FOURIER_GROUPS * FOURIER_GROUP_DIM
EVEN_IN_WIDTH = ATTN_WIDTH + 2 * KV_WIDTH + FOURIER_WIDTH
D_FF = 2816
N_EXPERTS = 8
D_EXPERT = 3584
ONES_ROWS = 16
V_ROWS = HEAD_DIM + ONES_ROWS
FAST_SCORE_BOUND = 24.0
ROPE_THETA = 10000.0
ROPE_HALF = HEAD_DIM // 4
NORM_EPS = 1e-6

LANES = 128
VMEM_LIMIT = 56 * 1024 * 1024

BF16 = jnp.bfloat16
FP8 = jnp.float8_e4m3fn
F32 = jnp.float32


def _cparams(semantics, vmem=VMEM_LIMIT):
    return pltpu.CompilerParams(dimension_semantics=semantics, vmem_limit_bytes=vmem)


def _dot(a, b):
    return jnp.dot(a, b, preferred_element_type=F32)


def _split_bf16(x):
    hi = x.astype(BF16)
    lo = (x - hi.astype(F32)).astype(BF16)
    return hi, lo


def _rms(x, g):
    return x * lax.rsqrt(jnp.mean(x * x, axis=-1, keepdims=True) + NORM_EPS) * g


def _mod_rows(mod_ref, row):
    return [mod_ref[0, row:row + 1, i * D_MODEL:(i + 1) * D_MODEL] for i in range(6)]


def _pack_pairs(x):
    k = x.shape[1] // 2
    bits = lax.bitcast_convert_type(x.astype(BF16).astype(F32), jnp.uint32)
    return (bits[:, :k] >> 16) | (bits[:, k:] & jnp.uint32(0xFFFF0000))


def _unpack_pairs(w):
    lo = lax.bitcast_convert_type(w << 16, F32).astype(BF16)
    hi = lax.bitcast_convert_type(w & jnp.uint32(0xFFFF0000), F32).astype(BF16)
    return jnp.concatenate([lo, hi], axis=1)


@functools.lru_cache(maxsize=None)
def _rope_tables(n_tokens):
    t = np.arange(n_tokens)
    row = (t // GRID_W).astype(np.float64)
    col = (t % GRID_W).astype(np.float64)
    inv = ROPE_THETA ** (-np.arange(ROPE_HALF, dtype=np.float64) / ROPE_HALF)
    ar, ac = row[:, None] * inv, col[:, None] * inv
    cos = np.concatenate([np.cos(ar), np.cos(ar), np.cos(ac), np.cos(ac)], axis=-1)
    sin = np.concatenate([-np.sin(ar), np.sin(ar), -np.sin(ac), np.sin(ac)], axis=-1)
    reps = LANES // HEAD_DIM
    return (np.tile(cos, (1, reps)).astype(np.float32), np.tile(sin, (1, reps)).astype(np.float32))


@functools.lru_cache(maxsize=None)
def _head_matrices():
    head = np.arange(QK_WIDTH) // HEAD_DIM
    red = (head[:, None] == np.arange(LANES)[None, :]).astype(np.float32) / HEAD_DIM
    exp = (np.arange(LANES)[:, None] == head[None, :]).astype(np.float32)
    return red, exp


@functools.lru_cache(maxsize=None)
def _fourier_tables(n_tokens, kb):
    n2 = LANES
    n1 = n_tokens // n2
    c = np.arange(FOURIER_GROUP_DIM)
    ang = 2 * np.pi * np.outer(c, c) / FOURIER_GROUP_DIM
    eye = np.eye(N_FOURIER_GROUPS)
    cs = np.concatenate([np.kron(eye, np.cos(ang)), np.kron(eye, np.sin(ang))], axis=1)
    k1 = np.arange(n1)
    th = 2 * np.pi * np.outer(k1, k1) / n1
    cr, ci = np.cos(th), -np.sin(th)
    base = np.block([[cr, ci], [ci, -cr]])
    psi = 2 * np.pi * np.outer(k1, np.arange(n2)) / n_tokens
    twr = np.cos(psi).reshape(n1, n2 // kb, kb).transpose(1, 0, 2)
    twi = (-np.sin(psi)).reshape(n1, n2 // kb, kb).transpose(1, 0, 2)
    k2 = np.arange(n2)
    ph = 2 * np.pi * np.outer(k2, k2) / n2
    fr, fi = np.cos(ph), -np.sin(ph)
    scale = 1.0 / np.sqrt(n_tokens * FOURIER_GROUP_DIM)
    m3 = np.stack([fr, -fi], axis=-1) * scale
    wb = np.einsum('knr,uv->kunrv', m3, np.eye(kb)).reshape(n2 * kb, n2 * 2 * kb)
    f32 = np.float32
    return cs.astype(f32), base.astype(f32), twr.astype(f32), twi.astype(f32), wb.astype(f32)


def _ada_kernel(cb_ref, w_ref, b_ref, o_ref):
    tn = o_ref.shape[-1]
    o_ref[...] = jnp.zeros(o_ref.shape, F32)
    for r in range(2):
        cb = cb_ref[r]
        s = cb * jax.nn.sigmoid(cb)
        for j in range(tn // LANES):
            sl = slice(j * LANES, (j + 1) * LANES)
            col = jnp.sum(s * w_ref[0, :, sl], axis=0, keepdims=True)
            o_ref[0, r:r + 1, sl] = col + b_ref[0, :, sl]


def _ada(c, c_ctx, ada_w, ada_b):
    depth = ada_w.shape[0]
    n = ada_w.shape[-1]
    tn = 1536
    cb = jnp.stack([jnp.broadcast_to(c[0][:, None], (D_MODEL, LANES)),
                    jnp.broadcast_to(c_ctx[:, None], (D_MODEL, LANES))])
    return pl.pallas_call(
        _ada_kernel,
        out_shape=jax.ShapeDtypeStruct((depth, 8, n), F32),
        grid=(depth, n // tn),
        in_specs=[pl.BlockSpec((2, D_MODEL, LANES), lambda i, j: (0, 0, 0)),
                  pl.BlockSpec((1, D_MODEL, tn), lambda i, j: (i, 0, j)),
                  pl.BlockSpec((1, 1, tn), lambda i, j: (i, 0, j))],
        out_specs=pl.BlockSpec((1, 8, tn), lambda i, j: (i, 0, j)),
        compiler_params=_cparams(("parallel", "parallel")),
        name="ada",
    )(cb, ada_w, ada_b[:, None, :])


def _evenproj_kernel(*refs, row, latent, qk_dtype):
    if latent:
        (x_ref, nw_ref, mod_ref, w_ref, gain_ref, red_ref, exp_ref, cos_ref, sin_ref,
         qT_ref, k_ref, vT_ref, f_ref) = refs
    else:
        x_ref, nw_ref, mod_ref, w_ref, gain_ref, red_ref, exp_ref, k_ref, vT_ref = refs
    sh, sc = _mod_rows(mod_ref, row)[:2]
    h = (_rms(x_ref[...], nw_ref[0, 0:1, :]) * (1.0 + sc) + sh).astype(BF16)
    z = _dot(h, w_ref[...])
    zqk = z[:, :QK_WIDTH]
    hi, lo = _split_bf16(zqk * zqk)
    red = red_ref[...].astype(BF16)
    ms = _dot(hi, red) + _dot(lo, red)
    rhi, rlo = _split_bf16(lax.rsqrt(ms + NORM_EPS))
    expm = exp_ref[...].astype(BF16)
    yn = zqk * (_dot(rhi, expm) + _dot(rlo, expm)) * gain_ref[...]
    if latent:
        lane = lax.broadcasted_iota(jnp.int32, (1, LANES), 1)
        first_half = (lane // ROPE_HALF) % 2 == 0
        cos, sin = cos_ref[...], sin_ref[...]
        chunks = []
        for c in range(QK_WIDTH // LANES):
            yc = yn[:, c * LANES:(c + 1) * LANES]
            partner = jnp.where(first_half, pltpu.roll(yc, LANES - ROPE_HALF, axis=1),
                                pltpu.roll(yc, ROPE_HALF, axis=1))
            chunks.append(yc * cos + partner * sin)
        yn = jnp.concatenate(chunks, axis=1)
        qT_ref[...] = yn[:, :ATTN_WIDTH].T.astype(qk_dtype)
        f_ref[...] = z[:, QK_WIDTH + KV_WIDTH:].astype(BF16)
    for g in range(N_KV_HEADS):
        k_ref[g] = yn[:, ATTN_WIDTH + g * HEAD_DIM:ATTN_WIDTH + (g + 1) * HEAD_DIM].astype(qk_dtype)
    vT = z[:, QK_WIDTH:QK_WIDTH + KV_WIDTH].T.astype(BF16)
    ones = jnp.ones((ONES_ROWS, vT.shape[1]), BF16)
    for g in range(N_KV_HEADS):
        vT_ref[g * V_ROWS:g * V_ROWS + HEAD_DIM, :] = vT[g * HEAD_DIM:(g + 1) * HEAD_DIM]
        vT_ref[g * V_ROWS + HEAD_DIM:(g + 1) * V_ROWS, :] = ones


def _evenproj(x2d, nw, mod, w_bf, gain, *, latent, tm, qk_dtype):
    n = x2d.shape[0]
    red, expm = _head_matrices()
    const = lambda i: (0, 0)
    in_specs = [pl.BlockSpec((tm, D_MODEL), lambda i: (i, 0)),
                pl.BlockSpec((1, 4, D_MODEL), lambda i: (0, 0, 0)),
                pl.BlockSpec((1, 8, 6 * D_MODEL), lambda i: (0, 0, 0)),
                pl.BlockSpec((D_MODEL, EVEN_IN_WIDTH), const),
                pl.BlockSpec((1, QK_WIDTH), const),
                pl.BlockSpec((QK_WIDTH, LANES), const),
                pl.BlockSpec((LANES, QK_WIDTH), const)]
    args = [x2d, nw, mod, w_bf, gain, jnp.asarray(red), jnp.asarray(expm)]
    k_shape = jax.ShapeDtypeStruct((N_KV_HEADS, n, HEAD_DIM), qk_dtype)
    vT_shape = jax.ShapeDtypeStruct((N_KV_HEADS * V_ROWS, n), BF16)
    k_spec = pl.BlockSpec((N_KV_HEADS, tm, HEAD_DIM), lambda i: (0, i, 0))
    vT_spec = pl.BlockSpec((N_KV_HEADS * V_ROWS, tm), lambda i: (0, i))
    if latent:
        cos, sin = _rope_tables(n)
        in_specs += [pl.BlockSpec((tm, LANES), lambda i: (i, 0))] * 2
        args += [jnp.asarray(cos), jnp.asarray(sin)]
        out_shape = (jax.ShapeDtypeStruct((ATTN_WIDTH, n), qk_dtype), k_shape, vT_shape,
                     jax.ShapeDtypeStruct((n, FOURIER_WIDTH), BF16))
        out_specs = (pl.BlockSpec((ATTN_WIDTH, tm), lambda i: (0, i)), k_spec, vT_spec,
                     pl.BlockSpec((tm, FOURIER_WIDTH), lambda i: (i, 0)))
    else:
        out_shape = (k_shape, vT_shape)
        out_specs = (k_spec, vT_spec)
    return pl.pallas_call(
        functools.partial(_evenproj_kernel, row=0 if latent else 1, latent=latent, qk_dtype=qk_dtype),
        out_shape=out_shape, grid=(n // tm,), in_specs=in_specs, out_specs=out_specs,
        compiler_params=_cparams(("parallel",)),
        name="evenproj_lat" if latent else "evenproj_ctx",
    )(*args)


def _visit_all(visit, k_ref, vT_ref, kc_ref, vcT_ref, tk):
    def body(c, carry):
        off = pl.multiple_of(c * tk, tk)
        visit(k_ref[0, pl.ds(off, tk), :], vT_ref[:, pl.ds(off, tk)])
        return carry

    lax.fori_loop(0, k_ref.shape[1] // tk, body, 0)
    visit(kc_ref[0], vcT_ref[...])


def _attn_bounded_kernel(qT_ref, k_ref, vT_ref, kc_ref, vcT_ref, o_ref, acc_sc, s_sc, *, tk):
    n_tiles = k_ref.shape[1] // tk
    acc_sc[...] = jnp.zeros(acc_sc.shape, F32)

    def q(j):
        return qT_ref[j * HEAD_DIM:(j + 1) * HEAD_DIM, :]

    def keys(c):
        return k_ref[0, pl.ds(pl.multiple_of(c * tk, tk), tk), :]

    def consume(j, s, vt):
        acc_sc[j] += _dot(vt, jnp.exp2(s).astype(BF16))

    s_sc[...] = _dot(keys(0), q(0))

    def body(c, carry):
        kt = keys(c)
        vt = vT_ref[:, pl.ds(pl.multiple_of(c * tk, tk), tk)]
        s = s_sc[...]
        for j in range(Q_PER_KV):
            if j + 1 < Q_PER_KV:
                s_next = _dot(kt, q(j + 1))
            else:
                s_next = _dot(keys(jnp.minimum(c + 1, n_tiles - 1)), q(0))
            consume(j, s, vt)
            s = s_next
        s_sc[...] = s
        return carry

    lax.fori_loop(0, n_tiles, body, 0, unroll=8)
    kc, vc = kc_ref[0], vcT_ref[...]
    s = _dot(kc, q(0))
    for j in range(Q_PER_KV):
        s_next = _dot(kc, q(j + 1)) if j + 1 < Q_PER_KV else None
        consume(j, s, vc)
        s = s_next
    for j in range(Q_PER_KV):
        acc = acc_sc[j]
        o_ref[j * HEAD_DIM:(j + 1) * HEAD_DIM, :] = (acc[:HEAD_DIM] / acc[HEAD_DIM:HEAD_DIM + 1]).astype(BF16)


def _attn_online_kernel(qT_ref, k_ref, vT_ref, kc_ref, vcT_ref, o_ref, m_sc, acc_sc, *, tk):
    m_sc[...] = jnp.full(m_sc.shape, -jnp.inf, F32)
    acc_sc[...] = jnp.zeros(acc_sc.shape, F32)

    def visit(kt, vt):
        for j in range(Q_PER_KV):
            s = _dot(kt, qT_ref[j * HEAD_DIM:(j + 1) * HEAD_DIM, :])
            m_old = m_sc[j]
            m_new = jnp.maximum(m_old, jnp.max(s, axis=0, keepdims=True))
            p = jnp.exp2(s - m_new).astype(BF16)
            acc_sc[j] = jnp.exp2(m_old - m_new) * acc_sc[j] + _dot(vt, p)
            m_sc[j] = m_new

    _visit_all(visit, k_ref, vT_ref, kc_ref, vcT_ref, tk)
    for j in range(Q_PER_KV):
        acc = acc_sc[j]
        o_ref[j * HEAD_DIM:(j + 1) * HEAD_DIM, :] = (acc[:HEAD_DIM] / acc[HEAD_DIM:HEAD_DIM + 1]).astype(BF16)


def _attention(qT, k, vT, kc, vcT, *, bounded, tq, tk):
    n = qT.shape[1]
    n_ctx = kc.shape[1]
    gw = Q_PER_KV * HEAD_DIM
    common = dict(
        out_shape=jax.ShapeDtypeStruct((ATTN_WIDTH, n), BF16),
        grid=(N_KV_HEADS, n // tq),
        in_specs=[pl.BlockSpec((gw, tq), lambda g, i: (g, i)),
                  pl.BlockSpec((1, n, HEAD_DIM), lambda g, i: (g, 0, 0)),
                  pl.BlockSpec((V_ROWS, n), lambda g, i: (g, 0)),
                  pl.BlockSpec((1, n_ctx, HEAD_DIM), lambda g, i: (g, 0, 0)),
                  pl.BlockSpec((V_ROWS, n_ctx), lambda g, i: (g, 0))],
        out_specs=pl.BlockSpec((gw, tq), lambda g, i: (g, i)),
        compiler_params=_cparams(("parallel", "parallel")),
    )
    acc = pltpu.VMEM((Q_PER_KV, V_ROWS, tq), F32)
    if bounded:
        call = pl.pallas_call(functools.partial(_attn_bounded_kernel, tk=tk),
                              scratch_shapes=[acc, pltpu.VMEM((tk, tq), F32)], name="attn_bounded", **common)
    else:
        call = pl.pallas_call(functools.partial(_attn_online_kernel, tk=tk),
                              scratch_shapes=[pltpu.VMEM((Q_PER_KV, 1, tq), F32), acc],
                              name="attn_online", **common)
    return call(qT, k, vT, kc, vcT)


def _four_a_kernel(f_ref, cs_ref, base_ref, twr_ref, twi_ref, y_ref, *, nb):
    n1 = f_ref.shape[0]
    cs = cs_ref[...].astype(BF16)
    base = base_ref[...].astype(BF16)
    for u in range(nb):
        xb = f_ref[:, u * FOURIER_WIDTH:(u + 1) * FOURIER_WIDTH]
        ab = _dot(xb, cs)
        stacked = jnp.concatenate([ab[:, :FOURIER_WIDTH], ab[:, FOURIER_WIDTH:]], axis=0)
        p = _dot(base, stacked.astype(BF16))
        pr, pi = p[:n1], p[n1:]
        tr = twr_ref[0, :, u:u + 1]
        ti = twi_ref[0, :, u:u + 1]
        y_ref[u, 0] = tr * pr - ti * pi
        y_ref[u, 1] = tr * pi + ti * pr


def _four_b_kernel(y_ref, wb_ref, o_ref):
    n2, _, kb, w = y_ref.shape
    y = y_ref[...].reshape(n2 * 2 * kb, w).astype(BF16)
    o_ref[...] = _dot(wb_ref[...].astype(BF16), y).reshape(n2, kb, w)


def _fourier(f):
    n = f.shape[0]
    n2 = LANES
    n1 = n // n2
    nb = kb = 8
    cs, base, twr, twi, wb = (jnp.asarray(t) for t in _fourier_tables(n, kb))
    f2d = f.reshape(n1, n2 * FOURIER_WIDTH)
    y = pl.pallas_call(
        functools.partial(_four_a_kernel, nb=nb),
        out_shape=jax.ShapeDtypeStruct((n2, 2, n1, FOURIER_WIDTH), F32),
        grid=(n2 // nb,),
        in_specs=[pl.BlockSpec((n1, nb * FOURIER_WIDTH), lambda s: (0, s)),
                  pl.BlockSpec(cs.shape, lambda s: (0, 0)),
                  pl.BlockSpec(base.shape, lambda s: (0, 0)),
                  pl.BlockSpec((1, n1, nb), lambda s: (s, 0, 0)),
                  pl.BlockSpec((1, n1, nb), lambda s: (s, 0, 0))],
        out_specs=pl.BlockSpec((nb, 2, n1, FOURIER_WIDTH), lambda s: (s, 0, 0, 0)),
        compiler_params=_cparams(("parallel",)),
        name="four_a",
    )(f2d, cs, base, twr, twi)
    out = pl.pallas_call(
        _four_b_kernel,
        out_shape=jax.ShapeDtypeStruct((n2, n1, FOURIER_WIDTH), F32),
        grid=(n1 // kb,),
        in_specs=[pl.BlockSpec((n2, 2, kb, FOURIER_WIDTH), lambda s: (0, 0, s, 0)),
                  pl.BlockSpec(wb.shape, lambda s: (0, 0))],
        out_specs=pl.BlockSpec((n2, kb, FOURIER_WIDTH), lambda s: (0, s, 0)),
        compiler_params=_cparams(("parallel",)),
        name="four_b",
    )(y, wb)
    return out.reshape(n, FOURIER_WIDTH)


def _outproj_kernel(aT_ref, four_ref, x_ref, wa_ref, wf_ref, nw_ref, mod_ref, x1_ref, h_ref):
    _, _, gt1, sh2, sc2, _ = _mod_rows(mod_ref, 0)
    y = lax.dot_general(aT_ref[...], wa_ref[...], (((0,), (0,)), ((), ())), preferred_element_type=F32)
    y = y + _dot(four_ref[...].astype(BF16), wf_ref[...])
    x1 = x_ref[...] + gt1 * _rms(y, nw_ref[0, 1:2, :])
    x1_ref[...] = x1
    h_ref[...] = (_rms(x1, nw_ref[0, 2:3, :]) * (1.0 + sc2) + sh2).astype(BF16)


def _outproj(attnT, four, x2d, wa, wf, nw, mod, *, tm):
    n = x2d.shape[0]
    const = lambda i: (0, 0)
    return pl.pallas_call(
        _outproj_kernel,
        out_shape=(jax.ShapeDtypeStruct((n, D_MODEL), F32), jax.ShapeDtypeStruct((n, D_MODEL), BF16)),
        grid=(n // tm,),
        in_specs=[pl.BlockSpec((ATTN_WIDTH, tm), lambda i: (0, i)),
                  pl.BlockSpec((tm, FOURIER_WIDTH), lambda i: (i, 0)),
                  pl.BlockSpec((tm, D_MODEL), lambda i: (i, 0)),
                  pl.BlockSpec((ATTN_WIDTH, D_MODEL), const),
                  pl.BlockSpec((FOURIER_WIDTH, D_MODEL), const),
                  pl.BlockSpec((1, 4, D_MODEL), lambda i: (0, 0, 0)),
                  pl.BlockSpec((1, 8, 6 * D_MODEL), lambda i: (0, 0, 0))],
        out_specs=(pl.BlockSpec((tm, D_MODEL), lambda i: (i, 0)),
                   pl.BlockSpec((tm, D_MODEL), lambda i: (i, 0))),
        compiler_params=_cparams(("parallel",)),
        name="outproj",
    )(attnT, four, x2d, wa, wf, nw, mod)


def _swiglu(x, wg, wu, wd):
    g = _dot(x, wg)
    return _dot((g * jax.nn.sigmoid(g) * _dot(x, wu)).astype(BF16), wd)


def _ffn_kernel(h_ref, x_ref, wg_ref, wu_ref, wd_ref, nw_ref, mod_ref, nw1_ref, mod1_ref, x2_ref, h3_ref):
    gt2 = _mod_rows(mod_ref, 0)[5]
    sh, sc = _mod_rows(mod1_ref, 0)[:2]
    y = _swiglu(h_ref[...], wg_ref[...], wu_ref[...], wd_ref[...])
    x2 = x_ref[...] + gt2 * _rms(y, nw_ref[0, 3:4, :])
    x2_ref[...] = x2
    h3_ref[...] = (_rms(x2, nw1_ref[0, 0:1, :]) * (1.0 + sc) + sh).astype(BF16)


def _ffn(h, x1, wg, wu, wd, nw, mod, nw1, mod1, *, tm):
    n = h.shape[0]
    nwspec = pl.BlockSpec((1, 4, D_MODEL), lambda i: (0, 0, 0))
    modspec = pl.BlockSpec((1, 8, 6 * D_MODEL), lambda i: (0, 0, 0))
    resident = dict(index_map=lambda i: (0, 0), pipeline_mode=pl.Buffered(1))
    return pl.pallas_call(
        _ffn_kernel,
        out_shape=(jax.ShapeDtypeStruct((n, D_MODEL), F32), jax.ShapeDtypeStruct((n, D_MODEL), BF16)),
        grid=(n // tm,),
        in_specs=[pl.BlockSpec((tm, D_MODEL), lambda i: (i, 0)),
                  pl.BlockSpec((tm, D_MODEL), lambda i: (i, 0)),
                  pl.BlockSpec((D_MODEL, D_FF), **resident),
                  pl.BlockSpec((D_MODEL, D_FF), **resident),
                  pl.BlockSpec((D_FF, D_MODEL), **resident),
                  nwspec, modspec, nwspec, modspec],
        out_specs=(pl.BlockSpec((tm, D_MODEL), lambda i: (i, 0)),
                   pl.BlockSpec((tm, D_MODEL), lambda i: (i, 0))),
        compiler_params=_cparams(("parallel",)),
        name="ffn",
    )(h, x1, wg, wu, wd, nw, mod, nw1, mod1)


EDGE_ROWS = 16


def _convedge_kernel(hf_ref, hl_ref, wc_ref, wu_ref, vf_ref, vl_ref):
    nt = hf_ref.shape[0]
    for h_ref, v_ref in ((hf_ref, vf_ref), (hl_ref, vl_ref)):
        h = h_ref[...].reshape(nt * EDGE_ROWS, D_MODEL)
        v = _dot(h, wc_ref[...]) * _dot(h, wu_ref[...])
        v_ref[...] = v.reshape(nt, EDGE_ROWS, D_MODEL)


def _convedge(h3, w_in_bf, *, tm):
    n = h3.shape[0]
    nt = n // tm
    h3t = h3.reshape(nt, tm, D_MODEL)
    last = tm // EDGE_ROWS - 1
    shape = jax.ShapeDtypeStruct((nt, EDGE_ROWS, D_MODEL), F32)
    return pl.pallas_call(
        _convedge_kernel,
        out_shape=(shape, shape),
        grid=(1,),
        in_specs=[pl.BlockSpec((nt, EDGE_ROWS, D_MODEL), lambda i: (0, 0, 0)),
                  pl.BlockSpec((nt, EDGE_ROWS, D_MODEL), lambda i: (0, last, 0)),
                  pl.BlockSpec((D_MODEL, D_MODEL), lambda i: (0, 1)),
                  pl.BlockSpec((D_MODEL, D_MODEL), lambda i: (0, 2))],
        out_specs=(pl.BlockSpec((nt, EDGE_ROWS, D_MODEL), lambda i: (0, 0, 0)),
                   pl.BlockSpec((nt, EDGE_ROWS, D_MODEL), lambda i: (0, 0, 0))),
        compiler_params=_cparams(("arbitrary",)),
        name="convedge",
    )(h3t, h3t, w_in_bf, w_in_bf)


def _convmix_kernel(h_ref, x_ref, vl_ref, vf_ref, win_ref, cw_ref, wout_ref, nw_ref, mod_ref, r_ref,
                    x3_ref, h4_ref, route_ref, sel_ref, rank_ref):
    i = pl.program_id(0)
    tm = h_ref.shape[0]
    _, _, gt1, sh2, sc2, _ = _mod_rows(mod_ref, 0)
    z = _dot(h_ref[...], win_ref[...])
    b = z[:, :D_MODEL]
    v = z[:, D_MODEL:2 * D_MODEL] * z[:, 2 * D_MODEL:]
    has_prev = (i > 0).astype(F32)
    has_next = (i < pl.num_programs(0) - 1).astype(F32)
    prev_row = vl_ref[0, EDGE_ROWS - 1:EDGE_ROWS, :] * has_prev
    next_row = vf_ref[0, 0:1, :] * has_next
    rows = lax.broadcasted_iota(jnp.int32, (tm, 1), 0)
    v_dn = jnp.where(rows == 0, prev_row, pltpu.roll(v, 1, axis=0))
    v_up = jnp.where(rows == tm - 1, next_row, pltpu.roll(v, tm - 1, axis=0))
    conv = v_dn * cw_ref[0, 0:1, :] + v * cw_ref[0, 1:2, :] + v_up * cw_ref[0, 2:3, :]
    y = _dot((b * conv).astype(BF16), wout_ref[...])
    x3 = x_ref[...] + gt1 * _rms(y, nw_ref[0, 1:2, :])
    x3_ref[...] = x3
    h4 = _rms(x3, nw_ref[0, 2:3, :]) * (1.0 + sc2) + sh2
    h4_ref[...] = _pack_pairs(h4)
    hhi, hlo = _split_bf16(h4)
    rhi, rlo = _split_bf16(r_ref[...])
    logits = _dot(hhi, rhi) + (_dot(hlo, rhi) + _dot(hhi, rlo))
    lane = lax.broadcasted_iota(jnp.int32, logits.shape, 1)
    logits = jnp.where(lane < N_EXPERTS, logits, -jnp.inf)
    e = jnp.exp(logits - jnp.max(logits, axis=-1, keepdims=True))
    probs = e / jnp.sum(e, axis=-1, keepdims=True)
    v1 = jnp.max(probs, axis=-1, keepdims=True)
    i1 = jnp.min(jnp.where(probs == v1, lane, LANES), axis=-1, keepdims=True)
    rest = jnp.where(lane == i1, -1.0, probs)
    v2 = jnp.max(rest, axis=-1, keepdims=True)
    i2 = jnp.min(jnp.where(rest == v2, lane, LANES), axis=-1, keepdims=True)
    tot = v1 + v2
    route_ref[...] = (jnp.where(lane == 0, i1.astype(F32), 0.0) + jnp.where(lane == 1, i2.astype(F32), 0.0)
                      + jnp.where(lane == 2, v1 / tot, 0.0) + jnp.where(lane == 3, v2 / tot, 0.0))
    sel = jnp.where(jnp.logical_or(lane == i1, lane == i2), 1.0, 0.0)
    sel_ref[...] = sel
    earlier = lax.broadcasted_iota(jnp.int32, (tm, tm), 1) < lax.broadcasted_iota(jnp.int32, (tm, tm), 0)
    rank_ref[...] = _dot(jnp.where(earlier, 1.0, 0.0).astype(BF16), sel.astype(BF16))


def _convmix(h3, x2, v_first, v_last, w_in_bf, conv_w, w_out_bf, nw, mod, router_pad, *, tm):
    n = h3.shape[0]
    nt = n // tm
    const = lambda i: (0, 0)
    lanes_shape = jax.ShapeDtypeStruct((n, LANES), F32)
    lanes_spec = pl.BlockSpec((tm, LANES), lambda i: (i, 0))
    return pl.pallas_call(
        _convmix_kernel,
        out_shape=(jax.ShapeDtypeStruct((n, D_MODEL), F32), jax.ShapeDtypeStruct((n, D_MODEL // 2), jnp.uint32),
                   lanes_shape, lanes_shape, lanes_shape),
        grid=(nt,),
        in_specs=[pl.BlockSpec((tm, D_MODEL), lambda i: (i, 0)),
                  pl.BlockSpec((tm, D_MODEL), lambda i: (i, 0)),
                  pl.BlockSpec((1, EDGE_ROWS, D_MODEL), lambda i: (jnp.maximum(i - 1, 0), 0, 0)),
                  pl.BlockSpec((1, EDGE_ROWS, D_MODEL), lambda i: (jnp.minimum(i + 1, nt - 1), 0, 0)),
                  pl.BlockSpec((D_MODEL, 3 * D_MODEL), const),
                  pl.BlockSpec((1, 3, D_MODEL), lambda i: (0, 0, 0)),
                  pl.BlockSpec((D_MODEL, D_MODEL), const),
                  pl.BlockSpec((1, 4, D_MODEL), lambda i: (0, 0, 0)),
                  pl.BlockSpec((1, 8, 6 * D_MODEL), lambda i: (0, 0, 0)),
                  pl.BlockSpec((D_MODEL, LANES), const)],
        out_specs=(pl.BlockSpec((tm, D_MODEL), lambda i: (i, 0)),
                   pl.BlockSpec((tm, D_MODEL // 2), lambda i: (i, 0)),
                   lanes_spec, lanes_spec, lanes_spec),
        compiler_params=_cparams(("parallel",)),
        name="convmix",
    )(h3, x2, v_last, v_first, w_in_bf, conv_w, w_out_bf, nw, mod, router_pad)


ROW_TILE = 512


def _count_le(sorted_vals, x):
    return jnp.sum((sorted_vals[None, :] <= x[:, None]).astype(jnp.int32), axis=1)


def _route(route, sel, lrank, n_row_tiles):
    n = sel.shape[0]
    n_tok_tiles = n // ROW_TILE
    i32 = jnp.int32
    sel8 = sel[:, :N_EXPERTS].astype(i32)
    cnt_tile = jnp.sum(sel8.reshape(n_tok_tiles, ROW_TILE, N_EXPERTS), axis=1)
    cum_end = jnp.cumsum(cnt_tile, axis=0)
    cum_beg = cum_end - cnt_tile
    cnt = cum_end[-1]
    cnt_pad = (cnt + ROW_TILE - 1) // ROW_TILE * ROW_TILE
    grp_end = jnp.cumsum(cnt_pad)
    start = grp_end - cnt_pad
    pos = start[None, :] + lrank[:, :N_EXPERTS].astype(i32) + jnp.repeat(cum_beg, ROW_TILE, axis=0)
    experts = jnp.clip(route[:, :2].astype(i32), 0, N_EXPERTS - 1)
    pos2 = jnp.take_along_axis(pos, experts, axis=1)
    pos2 = jnp.clip(pos2, 0, n_row_tiles * ROW_TILE - 1)
    r = jnp.arange(n_row_tiles, dtype=i32)
    base = r * ROW_TILE
    tile_valid = base < grp_end[-1]
    n_valid = jnp.clip(grp_end[-1] // ROW_TILE, 1, n_row_tiles)
    tile_exp = jnp.minimum(_count_le(grp_end, base), N_EXPERTS - 1)
    tile_exp = jnp.where(tile_valid, tile_exp, tile_exp[n_valid - 1])
    tile_src = jnp.where(tile_valid, r, n_valid - 1)
    tile_first = jnp.logical_and(tile_valid, base == start[tile_exp])
    tiles = (tile_exp, tile_src, tile_valid.astype(i32), tile_first.astype(i32))
    return pos2.T.reshape(-1), tiles


SC_CHUNK = 128


def _sc_workers():
    info = pltpu.get_tpu_info().sparse_core
    return info.num_cores, info.num_cores * info.num_subcores


def _sc_scatter_rows(rows, idx, n_out):
    n_src, d = rows.shape
    n_cores, n_workers = _sc_workers()
    per_worker = idx.shape[0] // n_workers
    assert idx.shape[0] % (n_workers * SC_CHUNK) == 0 and n_src % per_worker == 0
    idx2d = idx.reshape(-1, SC_CHUNK)
    mesh = plsc.VectorSubcoreMesh(core_axis_name="c", subcore_axis_name="s")

    @functools.partial(pl.kernel, mesh=mesh, out_type=jax.ShapeDtypeStruct((n_out, d), rows.dtype),
                       scratch_types=[pltpu.VMEM((1, SC_CHUNK), jnp.int32), pltpu.VMEM((SC_CHUNK, d), rows.dtype)])
    def scatter(rows_hbm, idx_hbm, out_hbm, idx_v, rows_v):
        wid = lax.axis_index("s") * n_cores + lax.axis_index("c")

        @pl.loop(0, per_worker // SC_CHUNK)
        def _(j):
            a = wid * per_worker + j * SC_CHUNK
            pltpu.sync_copy(idx_hbm.at[pl.ds(a // SC_CHUNK, 1)], idx_v)
            pltpu.sync_copy(rows_hbm.at[pl.ds(lax.rem(a, n_src), SC_CHUNK)], rows_v)
            pltpu.sync_copy(rows_v, out_hbm.at[idx_v.at[0]])

    return scatter(rows, idx2d)


def _sc_gather_rows(table, idx):
    d = table.shape[1]
    n_cores, n_workers = _sc_workers()
    per_worker = idx.shape[0] // n_workers
    assert idx.shape[0] % (n_workers * SC_CHUNK) == 0
    idx2d = idx.reshape(-1, SC_CHUNK)
    mesh = plsc.VectorSubcoreMesh(core_axis_name="c", subcore_axis_name="s")

    @functools.partial(pl.kernel, mesh=mesh, out_type=jax.ShapeDtypeStruct((idx.shape[0], d), table.dtype),
                       scratch_types=[pltpu.VMEM((1, SC_CHUNK), jnp.int32), pltpu.VMEM((SC_CHUNK, d), table.dtype)])
    def gather(table_hbm, idx_hbm, out_hbm, idx_v, rows_v):
        wid = lax.axis_index("s") * n_cores + lax.axis_index("c")

        @pl.loop(0, per_worker // SC_CHUNK)
        def _(j):
            a = wid * per_worker + j * SC_CHUNK
            pltpu.sync_copy(idx_hbm.at[pl.ds(a // SC_CHUNK, 1)], idx_v)
            pltpu.sync_copy(table_hbm.at[idx_v.at[0]], rows_v)
            pltpu.sync_copy(rows_v, out_hbm.at[pl.ds(a, SC_CHUNK)])

    return gather(table, idx2d)


def _experts_kernel(t_exp, t_src, t_valid, t_first, xs_ref, wg_ref, wu_ref, wd_ref, ys_ref,
                    wg_c, wu_c, wd_c, x_sc, acc_ref):
    r = pl.program_id(0)
    c = pl.program_id(1)
    last = c == pl.num_programs(1) - 1

    @pl.when(t_first[r] != 0)
    def _():
        wg_c[c] = wg_ref[0, 0].astype(BF16)
        wu_c[c] = wu_ref[0, 0].astype(BF16)
        wd_c[c] = wd_ref[0, 0].astype(BF16)

    @pl.when(t_valid[r] != 0)
    def _():
        @pl.when(c == 0)
        def _():
            x_sc[...] = _unpack_pairs(xs_ref[...])
            acc_ref[...] = jnp.zeros(acc_ref.shape, F32)

        acc_ref[...] += _swiglu(x_sc[...], wg_c[c], wu_c[c], wd_c[c])

        @pl.when(last)
        def _():
            ys_ref[...] = _pack_pairs(acc_ref[...])

    @pl.when(jnp.logical_and(t_valid[r] == 0, last))
    def _():
        ys_ref[...] = jnp.zeros(ys_ref.shape, ys_ref.dtype)


def _experts(tiles, xs, wg, wu, wd, *, tf):
    n_rows, half = xs.shape
    n_ch = D_EXPERT // tf

    def w_in(r, c, e, s, v, f):
        return (0, e[r], 0, jnp.where(f[r] != 0, c, n_ch - 1))

    def w_dn(r, c, e, s, v, f):
        return (0, e[r], jnp.where(f[r] != 0, c, n_ch - 1), 0)

    grid_spec = pltpu.PrefetchScalarGridSpec(
        num_scalar_prefetch=4, grid=(n_rows // ROW_TILE, n_ch),
        in_specs=[pl.BlockSpec((ROW_TILE, half), lambda r, c, e, s, v, f: (s[r], 0)),
                  pl.BlockSpec((1, 1, D_MODEL, tf), w_in),
                  pl.BlockSpec((1, 1, D_MODEL, tf), w_in),
                  pl.BlockSpec((1, 1, tf, D_MODEL), w_dn)],
        out_specs=pl.BlockSpec((ROW_TILE, half), lambda r, c, e, s, v, f: (r, 0)),
        scratch_shapes=[pltpu.VMEM((n_ch, D_MODEL, tf), BF16), pltpu.VMEM((n_ch, D_MODEL, tf), BF16),
                        pltpu.VMEM((n_ch, tf, D_MODEL), BF16), pltpu.VMEM((ROW_TILE, D_MODEL), BF16),
                        pltpu.VMEM((ROW_TILE, D_MODEL), F32)])
    return pl.pallas_call(
        _experts_kernel, grid_spec=grid_spec,
        out_shape=jax.ShapeDtypeStruct((n_rows, half), jnp.uint32),
        compiler_params=_cparams(("arbitrary", "arbitrary")),
        name="moe_experts",
    )(*tiles, xs, wg, wu, wd)


def _combine_kernel(a_ref, b_ref, route_ref, x_ref, nw_ref, mod_ref, o_ref):
    route = route_ref[...]
    y = route[:, 2:3] * _unpack_pairs(a_ref[...]).astype(F32) + route[:, 3:4] * _unpack_pairs(b_ref[...]).astype(F32)
    gt2 = _mod_rows(mod_ref, 0)[5]
    o_ref[...] = x_ref[...] + gt2 * _rms(y, nw_ref[0, 3:4, :])


def _combine(rows, route, x3, nw, mod):
    n = x3.shape[0]
    nt = n // ROW_TILE
    half = rows.shape[1]
    return pl.pallas_call(
        _combine_kernel,
        out_shape=jax.ShapeDtypeStruct((n, D_MODEL), F32),
        grid=(nt,),
        in_specs=[pl.BlockSpec((ROW_TILE, half), lambda i: (i, 0)),
                  pl.BlockSpec((ROW_TILE, half), lambda i: (i + nt, 0)),
                  pl.BlockSpec((ROW_TILE, LANES), lambda i: (i, 0)),
                  pl.BlockSpec((ROW_TILE, D_MODEL), lambda i: (i, 0)),
                  pl.BlockSpec((1, 4, D_MODEL), lambda i: (0, 0, 0)),
                  pl.BlockSpec((1, 8, 6 * D_MODEL), lambda i: (0, 0, 0))],
        out_specs=pl.BlockSpec((ROW_TILE, D_MODEL), lambda i: (i, 0)),
        compiler_params=_cparams(("parallel",)),
        name="moe_combine",
    )(rows, rows, route, x3, nw, mod)


def _moe(h4p, route, sel, lrank, x3, wg, wu, wd, nw, mod):
    n = h4p.shape[0]
    n_rows = 2 * n + N_EXPERTS * ROW_TILE
    pos, tiles = _route(route, sel, lrank, n_rows // ROW_TILE)
    xs = _sc_scatter_rows(h4p, pos, n_rows)
    ys = _experts(tiles, xs, wg, wu, wd, tf=512)
    rows = _sc_gather_rows(ys, pos)
    return _combine(rows, route, x3, nw, mod)


def kernel(x, c, ctx, c_ctx, ada_w, ada_b, norm_w, e_w_in, e_q_gain, e_k_gain, e_w_out, e_ffn_gate,
           e_ffn_up, e_ffn_down, o_w_in, o_conv_w, o_w_out, o_router, o_exp_gate, o_exp_up, o_exp_down):
    assert x.shape[0] == 1 and x.shape[2] == D_MODEL and ada_w.shape[0] == 2
    n = x.shape[1]
    x2d = x[0]
    ctx2d = ctx[0]
    mod = _ada(c, c_ctx, ada_w, ada_b)
    mod0, mod1 = mod[0:1], mod[1:2]
    nw0, nw1 = norm_w[0:1], norm_w[1:2]

    w_in = e_w_in[0].astype(BF16)
    scale = HEAD_DIM ** -0.5 * np.log2(np.e)
    gain = jnp.concatenate([jnp.tile(e_q_gain[0], N_Q_HEADS) * scale,
                            jnp.tile(e_k_gain[0], N_KV_HEADS)])[None, :]
    score_bound = 1.02 * HEAD_DIM * scale * jnp.max(jnp.abs(e_q_gain[0])) * jnp.max(jnp.abs(e_k_gain[0]))

    def mixer_inputs(fast):
        qk_dtype = FP8 if fast else BF16
        qT, k, vT, f = _evenproj(x2d, nw0, mod0, w_in, gain, latent=True, tm=512, qk_dtype=qk_dtype)
        kc, vcT = _evenproj(ctx2d, nw0, mod0, w_in, gain, latent=False, tm=ctx2d.shape[0], qk_dtype=qk_dtype)
        return _attention(qT, k, vT, kc, vcT, bounded=fast, tq=256, tk=1024), f

    attnT, f = lax.cond(score_bound <= FAST_SCORE_BOUND, functools.partial(mixer_inputs, True),
                        functools.partial(mixer_inputs, False))
    four = _fourier(f)
    w_out = e_w_out[0].astype(BF16)
    x1, h2 = _outproj(attnT, four, x2d, w_out[:ATTN_WIDTH], w_out[ATTN_WIDTH:], nw0, mod0, tm=512)
    x2, h3 = _ffn(h2, x1, e_ffn_gate[0].astype(BF16), e_ffn_up[0].astype(BF16),
                  e_ffn_down[0].astype(BF16), nw0, mod0, nw1, mod1, tm=512)

    ow_in = o_w_in[0].astype(BF16)
    v_first, v_last = _convedge(h3, ow_in, tm=ROW_TILE)
    router_pad = jnp.pad(o_router[0], ((0, 0), (0, LANES - N_EXPERTS)))
    x3, h4p, route, sel, lrank = _convmix(h3, x2, v_first, v_last, ow_in, o_conv_w, o_w_out[0].astype(BF16),
                                        nw1, mod1, router_pad, tm=ROW_TILE)
    out = _moe(h4p, route, sel, lrank, x3, o_exp_gate, o_exp_up, o_exp_down, nw1, mod1)
    return out[None]
```

```python
import functools

import numpy as np
import jax
import jax.numpy as jnp
from jax import lax
from jax.experimental import pallas as pl
from jax.experimental.pallas import tpu as pltpu
from jax.experimental.pallas import tpu_sc as plsc

D_MODEL = 1024
GRID_W = 64
HEAD_DIM = 64
N_Q_HEADS = 12
N_KV_HEADS = 4
Q_PER_KV = N_Q_HEADS // N_KV_HEADS
ATTN_WIDTH = N_Q_HEADS * HEAD_DIM
KV_WIDTH = N_KV_HEADS * HEAD_DIM
QK_WIDTH = ATTN_WIDTH + KV_WIDTH
N_FOURIER_GROUPS = 4
FOURIER_GROUP_DIM = 64
FOURIER_WIDTH = N_FOURIER_GROUPS * FOURIER_GROUP_DIM
EVEN_IN_WIDTH = ATTN_WIDTH + 2 * KV_WIDTH + FOURIER_WIDTH
D_FF = 2816
N_EXPERTS = 8
D_EXPERT = 3584
ONES_ROWS = 16
V_ROWS = HEAD_DIM + ONES_ROWS
FAST_SCORE_BOUND = 24.0
ROPE_THETA = 10000.0
ROPE_HALF = HEAD_DIM // 4
NORM_EPS = 1e-6

LANES = 128
VMEM_LIMIT = 56 * 1024 * 1024

BF16 = jnp.bfloat16
FP8 = jnp.float8_e4m3fn
F32 = jnp.float32


def _cparams(semantics, vmem=VMEM_LIMIT):
    return pltpu.CompilerParams(dimension_semantics=semantics, vmem_limit_bytes=vmem)


def _dot(a, b):
    return jnp.dot(a, b, preferred_element_type=F32)


def _split_bf16(x):
    hi = x.astype(BF16)
    lo = (x - hi.astype(F32)).astype(BF16)
    return hi, lo


def _rms(x, g):
    return x * lax.rsqrt(jnp.mean(x * x, axis=-1, keepdims=True) + NORM_EPS) * g


def _mod_rows(mod_ref, row):
    return [mod_ref[0, row:row + 1, i * D_MODEL:(i + 1) * D_MODEL] for i in range(6)]


def _pack_pairs(x):
    k = x.shape[1] // 2
    bits = lax.bitcast_convert_type(x.astype(BF16).astype(F32), jnp.uint32)
    return (bits[:, :k] >> 16) | (bits[:, k:] & jnp.uint32(0xFFFF0000))


def _unpack_pairs(w):
    lo = lax.bitcast_convert_type(w << 16, F32).astype(BF16)
    hi = lax.bitcast_convert_type(w & jnp.uint32(0xFFFF0000), F32).astype(BF16)
    return jnp.concatenate([lo, hi], axis=1)


@functools.lru_cache(maxsize=None)
def _rope_tables(n_tokens):
    t = np.arange(n_tokens)
    row = (t // GRID_W).astype(np.float64)
    col = (t % GRID_W).astype(np.float64)
    inv = ROPE_THETA ** (-np.arange(ROPE_HALF, dtype=np.float64) / ROPE_HALF)
    ar, ac = row[:, None] * inv, col[:, None] * inv
    cos = np.concatenate([np.cos(ar), np.cos(ar), np.cos(ac), np.cos(ac)], axis=-1)
    sin = np.concatenate([-np.sin(ar), np.sin(ar), -np.sin(ac), np.sin(ac)], axis=-1)
    reps = LANES // HEAD_DIM
    return (np.tile(cos, (1, reps)).astype(np.float32), np.tile(sin, (1, reps)).astype(np.float32))


@functools.lru_cache(maxsize=None)
def _head_matrices():
    head = np.arange(QK_WIDTH) // HEAD_DIM
    red = (head[:, None] == np.arange(LANES)[None, :]).astype(np.float32) / HEAD_DIM
    exp = (np.arange(LANES)[:, None] == head[None, :]).astype(np.float32)
    return red, exp


@functools.lru_cache(maxsize=None)
def _fourier_tables(n_tokens, kb):
    n2 = LANES
    n1 = n_tokens // n2
    c = np.arange(FOURIER_GROUP_DIM)
    ang = 2 * np.pi * np.outer(c, c) / FOURIER_GROUP_DIM
    eye = np.eye(N_FOURIER_GROUPS)
    cs = np.concatenate([np.kron(eye, np.cos(ang)), np.kron(eye, np.sin(ang))], axis=1)
    k1 = np.arange(n1)
    th = 2 * np.pi * np.outer(k1, k1) / n1
    cr, ci = np.cos(th), -np.sin(th)
    base = np.block([[cr, ci], [ci, -cr]])
    psi = 2 * np.pi * np.outer(k1, np.arange(n2)) / n_tokens
    twr = np.cos(psi).reshape(n1, n2 // kb, kb).transpose(1, 0, 2)
    twi = (-np.sin(psi)).reshape(n1, n2 // kb, kb).transpose(1, 0, 2)
    k2 = np.arange(n2)
    ph = 2 * np.pi * np.outer(k2, k2) / n2
    fr, fi = np.cos(ph), -np.sin(ph)
    scale = 1.0 / np.sqrt(n_tokens * FOURIER_GROUP_DIM)
    m3 = np.stack([fr, -fi], axis=-1) * scale
    wb = np.einsum('knr,uv->kunrv', m3, np.eye(kb)).reshape(n2 * kb, n2 * 2 * kb)
    f32 = np.float32
    return cs.astype(f32), base.astype(f32), twr.astype(f32), twi.astype(f32), wb.astype(f32)


def _ada_kernel(cb_ref, w_ref, b_ref, o_ref):
    tn = o_ref.shape[-1]
    o_ref[...] = jnp.zeros(o_ref.shape, F32)
    for r in range(2):
        cb = cb_ref[r]
        s = cb * jax.nn.sigmoid(cb)
        for j in range(tn // LANES):
            sl = slice(j * LANES, (j + 1) * LANES)
            col = jnp.sum(s * w_ref[0, :, sl], axis=0, keepdims=True)
            o_ref[0, r:r + 1, sl] = col + b_ref[0, :, sl]


def _ada(c, c_ctx, ada_w, ada_b):
    depth = ada_w.shape[0]
    n = ada_w.shape[-1]
    tn = 1536
    cb = jnp.stack([jnp.broadcast_to(c[0][:, None], (D_MODEL, LANES)),
                    jnp.broadcast_to(c_ctx[:, None], (D_MODEL, LANES))])
    return pl.pallas_call(
        _ada_kernel,
        out_shape=jax.ShapeDtypeStruct((depth, 8, n), F32),
        grid=(depth, n // tn),
        in_specs=[pl.BlockSpec((2, D_MODEL, LANES), lambda i, j: (0, 0, 0)),
                  pl.BlockSpec((1, D_MODEL, tn), lambda i, j: (i, 0, j)),
                  pl.BlockSpec((1, 1, tn), lambda i, j: (i, 0, j))],
        out_specs=pl.BlockSpec((1, 8, tn), lambda i, j: (i, 0, j)),
        compiler_params=_cparams(("parallel", "parallel")),
        name="ada",
    )(cb, ada_w, ada_b[:, None, :])


def _evenproj_kernel(*refs, row, latent, qk_dtype):
    if latent:
        (x_ref, nw_ref, mod_ref, w_ref, gain_ref, red_ref, exp_ref, cos_ref, sin_ref,
         qT_ref, k_ref, vT_ref, f_ref) = refs
    else:
        x_ref, nw_ref, mod_ref, w_ref, gain_ref, red_ref, exp_ref, k_ref, vT_ref = refs
    sh, sc = _mod_rows(mod_ref, row)[:2]
    h = (_rms(x_ref[...], nw_ref[0, 0:1, :]) * (1.0 + sc) + sh).astype(BF16)
    z = _dot(h, w_ref[...])
    zqk = z[:, :QK_WIDTH]
    hi, lo = _split_bf16(zqk * zqk)
    red = red_ref[...].astype(BF16)
    ms = _dot(hi, red) + _dot(lo, red)
    rhi, rlo = _split_bf16(lax.rsqrt(ms + NORM_EPS))
    expm = exp_ref[...].astype(BF16)
    yn = zqk * (_dot(rhi, expm) + _dot(rlo, expm)) * gain_ref[...]
    if latent:
        lane = lax.broadcasted_iota(jnp.int32, (1, LANES), 1)
        first_half = (lane // ROPE_HALF) % 2 == 0
        cos, sin = cos_ref[...], sin_ref[...]
        chunks = []
        for c in range(QK_WIDTH // LANES):
            yc = yn[:, c * LANES:(c + 1) * LANES]
            partner = jnp.where(first_half, pltpu.roll(yc, LANES - ROPE_HALF, axis=1),
                                pltpu.roll(yc, ROPE_HALF, axis=1))
            chunks.append(yc * cos + partner * sin)
        yn = jnp.concatenate(chunks, axis=1)
        qT_ref[...] = yn[:, :ATTN_WIDTH].T.astype(qk_dtype)
        f_ref[...] = z[:, QK_WIDTH + KV_WIDTH:].astype(BF16)
    for g in range(N_KV_HEADS):
        k_ref[g] = yn[:, ATTN_WIDTH + g * HEAD_DIM:ATTN_WIDTH + (g + 1) * HEAD_DIM].astype(qk_dtype)
    vT = z[:, QK_WIDTH:QK_WIDTH + KV_WIDTH].T.astype(BF16)
    ones = jnp.ones((ONES_ROWS, vT.shape[1]), BF16)
    for g in range(N_KV_HEADS):
        vT_ref[g * V_ROWS:g * V_ROWS + HEAD_DIM, :] = vT[g * HEAD_DIM:(g + 1) * HEAD_DIM]
        vT_ref[g * V_ROWS + HEAD_DIM:(g + 1) * V_ROWS, :] = ones


def _evenproj(x2d, nw, mod, w_bf, gain, *, latent, tm, qk_dtype):
    n = x2d.shape[0]
    red, expm = _head_matrices()
    const = lambda i: (0, 0)
    in_specs = [pl.BlockSpec((tm, D_MODEL), lambda i: (i, 0)),
                pl.BlockSpec((1, 4, D_MODEL), lambda i: (0, 0, 0)),
                pl.BlockSpec((1, 8, 6 * D_MODEL), lambda i: (0, 0, 0)),
                pl.BlockSpec((D_MODEL, EVEN_IN_WIDTH), const),
                pl.BlockSpec((1, QK_WIDTH), const),
                pl.BlockSpec((QK_WIDTH, LANES), const),
                pl.BlockSpec((LANES, QK_WIDTH), const)]
    args = [x2d, nw, mod, w_bf, gain, jnp.asarray(red), jnp.asarray(expm)]
    k_shape = jax.ShapeDtypeStruct((N_KV_HEADS, n, HEAD_DIM), qk_dtype)
    vT_shape = jax.ShapeDtypeStruct((N_KV_HEADS * V_ROWS, n), BF16)
    k_spec = pl.BlockSpec((N_KV_HEADS, tm, HEAD_DIM), lambda i: (0, i, 0))
    vT_spec = pl.BlockSpec((N_KV_HEADS * V_ROWS, tm), lambda i: (0, i))
    if latent:
        cos, sin = _rope_tables(n)
        in_specs += [pl.BlockSpec((tm, LANES), lambda i: (i, 0))] * 2
        args += [jnp.asarray(cos), jnp.asarray(sin)]
        out_shape = (jax.ShapeDtypeStruct((ATTN_WIDTH, n), qk_dtype), k_shape, vT_shape,
                     jax.ShapeDtypeStruct((n, FOURIER_WIDTH), BF16))
        out_specs = (pl.BlockSpec((ATTN_WIDTH, tm), lambda i: (0, i)), k_spec, vT_spec,
                     pl.BlockSpec((tm, FOURIER_WIDTH), lambda i: (i, 0)))
    else:
        out_shape = (k_shape, vT_shape)
        out_specs = (k_spec, vT_spec)
    return pl.pallas_call(
        functools.partial(_evenproj_kernel, row=0 if latent else 1, latent=latent, qk_dtype=qk_dtype),
        out_shape=out_shape, grid=(n // tm,), in_specs=in_specs, out_specs=out_specs,
        compiler_params=_cparams(("parallel",)),
        name="evenproj_lat" if latent else "evenproj_ctx",
    )(*args)


def _visit_all(visit, k_ref, vT_ref, kc_ref, vcT_ref, tk):
    def body(c, carry):
        off = pl.multiple_of(c * tk, tk)
        visit(k_ref[0, pl.ds(off, tk), :], vT_ref[:, pl.ds(off, tk)])
        return carry

    lax.fori_loop(0, k_ref.shape[1] // tk, body, 0)
    visit(kc_ref[0], vcT_ref[...])


def _attn_bounded_kernel(qT_ref, k_ref, vT_ref, kc_ref, vcT_ref, o_ref, acc_sc, s_sc, *, tk):
    n_tiles = k_ref.shape[1] // tk
    acc_sc[...] = jnp.zeros(acc_sc.shape, F32)

    def q(j):
        return qT_ref[j * HEAD_DIM:(j + 1) * HEAD_DIM, :]

    def keys(c):
        return k_ref[0, pl.ds(pl.multiple_of(c * tk, tk), tk), :]

    def consume(j, s, vt):
        acc_sc[j] += _dot(vt, jnp.exp2(s).astype(BF16))

    s_sc[...] = _dot(keys(0), q(0))

    def body(c, carry):
        kt = keys(c)
        vt = vT_ref[:, pl.ds(pl.multiple_of(c * tk, tk), tk)]
        s = s_sc[...]
        for j in range(Q_PER_KV):
            if j + 1 < Q_PER_KV:
                s_next = _dot(kt, q(j + 1))
            else:
                s_next = _dot(keys(jnp.minimum(c + 1, n_tiles - 1)), q(0))
            consume(j, s, vt)
            s = s_next
        s_sc[...] = s
        return carry

    lax.fori_loop(0, n_tiles, body, 0, unroll=16)
    kc, vc = kc_ref[0], vcT_ref[...]
    s = _dot(kc, q(0))
    for j in range(Q_PER_KV):
        s_next = _dot(kc, q(j + 1)) if j + 1 < Q_PER_KV else None
        consume(j, s, vc)
        s = s_next
    for j in range(Q_PER_KV):
        acc = acc_sc[j]
        o_ref[j * HEAD_DIM:(j + 1) * HEAD_DIM, :] = (acc[:HEAD_DIM] / acc[HEAD_DIM:HEAD_DIM + 1]).astype(BF16)


def _attn_online_kernel(qT_ref, k_ref, vT_ref, kc_ref, vcT_ref, o_ref, m_sc, acc_sc, *, tk):
    m_sc[...] = jnp.full(m_sc.shape, -jnp.inf, F32)
    acc_sc[...] = jnp.zeros(acc_sc.shape, F32)

    def visit(kt, vt):
        for j in range(Q_PER_KV):
            s = _dot(kt, qT_ref[j * HEAD_DIM:(j + 1) * HEAD_DIM, :])
            m_old = m_sc[j]
            m_new = jnp.maximum(m_old, jnp.max(s, axis=0, keepdims=True))
            p = jnp.exp2(s - m_new).astype(BF16)
            acc_sc[j] = jnp.exp2(m_old - m_new) * acc_sc[j] + _dot(vt, p)
            m_sc[j] = m_new

    _visit_all(visit, k_ref, vT_ref, kc_ref, vcT_ref, tk)
    for j in range(Q_PER_KV):
        acc = acc_sc[j]
        o_ref[j * HEAD_DIM:(j + 1) * HEAD_DIM, :] = (acc[:HEAD_DIM] / acc[HEAD_DIM:HEAD_DIM + 1]).astype(BF16)


def _attention(qT, k, vT, kc, vcT, *, bounded, tq, tk):
    n = qT.shape[1]
    n_ctx = kc.shape[1]
    gw = Q_PER_KV * HEAD_DIM
    common = dict(
        out_shape=jax.ShapeDtypeStruct((ATTN_WIDTH, n), BF16),
        grid=(N_KV_HEADS, n // tq),
        in_specs=[pl.BlockSpec((gw, tq), lambda g, i: (g, i)),
                  pl.BlockSpec((1, n, HEAD_DIM), lambda g, i: (g, 0, 0)),
                  pl.BlockSpec((V_ROWS, n), lambda g, i: (g, 0)),
                  pl.BlockSpec((1, n_ctx, HEAD_DIM), lambda g, i: (g, 0, 0)),
                  pl.BlockSpec((V_ROWS, n_ctx), lambda g, i: (g, 0))],
        out_specs=pl.BlockSpec((gw, tq), lambda g, i: (g, i)),
        compiler_params=_cparams(("parallel", "parallel")),
    )
    acc = pltpu.VMEM((Q_PER_KV, V_ROWS, tq), F32)
    if bounded:
        call = pl.pallas_call(functools.partial(_attn_bounded_kernel, tk=tk),
                              scratch_shapes=[acc, pltpu.VMEM((tk, tq), F32)], name="attn_bounded", **common)
    else:
        call = pl.pallas_call(functools.partial(_attn_online_kernel, tk=tk),
                              scratch_shapes=[pltpu.VMEM((Q_PER_KV, 1, tq), F32), acc],
                              name="attn_online", **common)
    return call(qT, k, vT, kc, vcT)


def _four_a_kernel(f_ref, cs_ref, base_ref, twr_ref, twi_ref, y_ref, *, nb):
    n1 = f_ref.shape[0]
    cs = cs_ref[...].astype(BF16)
    base = base_ref[...].astype(BF16)
    for u in range(nb):
        xb = f_ref[:, u * FOURIER_WIDTH:(u + 1) * FOURIER_WIDTH]
        ab = _dot(xb, cs)
        stacked = jnp.concatenate([ab[:, :FOURIER_WIDTH], ab[:, FOURIER_WIDTH:]], axis=0)
        p = _dot(base, stacked.astype(BF16))
        pr, pi = p[:n1], p[n1:]
        tr = twr_ref[0, :, u:u + 1]
        ti = twi_ref[0, :, u:u + 1]
        y_ref[u, 0] = tr * pr - ti * pi
        y_ref[u, 1] = tr * pi + ti * pr


def _four_b_kernel(y_ref, wb_ref, o_ref):
    n2, _, kb, w = y_ref.shape
    y = y_ref[...].reshape(n2 * 2 * kb, w).astype(BF16)
    o_ref[...] = _dot(wb_ref[...].astype(BF16), y).reshape(n2, kb, w)


def _fourier(f):
    n = f.shape[0]
    n2 = LANES
    n1 = n // n2
    nb = kb = 8
    cs, base, twr, twi, wb = (jnp.asarray(t) for t in _fourier_tables(n, kb))
    f2d = f.reshape(n1, n2 * FOURIER_WIDTH)
    y = pl.pallas_call(
        functools.partial(_four_a_kernel, nb=nb),
        out_shape=jax.ShapeDtypeStruct((n2, 2, n1, FOURIER_WIDTH), F32),
        grid=(n2 // nb,),
        in_specs=[pl.BlockSpec((n1, nb * FOURIER_WIDTH), lambda s: (0, s)),
                  pl.BlockSpec(cs.shape, lambda s: (0, 0)),
                  pl.BlockSpec(base.shape, lambda s: (0, 0)),
                  pl.BlockSpec((1, n1, nb), lambda s: (s, 0, 0)),
                  pl.BlockSpec((1, n1, nb), lambda s: (s, 0, 0))],
        out_specs=pl.BlockSpec((nb, 2, n1, FOURIER_WIDTH), lambda s: (s, 0, 0, 0)),
        compiler_params=_cparams(("parallel",)),
        name="four_a",
    )(f2d, cs, base, twr, twi)
    out = pl.pallas_call(
        _four_b_kernel,
        out_shape=jax.ShapeDtypeStruct((n2, n1, FOURIER_WIDTH), F32),
        grid=(n1 // kb,),
        in_specs=[pl.BlockSpec((n2, 2, kb, FOURIER_WIDTH), lambda s: (0, 0, s, 0)),
                  pl.BlockSpec(wb.shape, lambda s: (0, 0))],
        out_specs=pl.BlockSpec((n2, kb, FOURIER_WIDTH), lambda s: (0, s, 0)),
        compiler_params=_cparams(("parallel",)),
        name="four_b",
    )(y, wb)
    return out.reshape(n, FOURIER_WIDTH)


def _outproj_kernel(aT_ref, four_ref, x_ref, wa_ref, wf_ref, nw_ref, mod_ref, x1_ref, h_ref):
    _, _, gt1, sh2, sc2, _ = _mod_rows(mod_ref, 0)
    y = lax.dot_general(aT_ref[...], wa_ref[...], (((0,), (0,)), ((), ())), preferred_element_type=F32)
    y = y + _dot(four_ref[...].astype(BF16), wf_ref[...])
    x1 = x_ref[...] + gt1 * _rms(y, nw_ref[0, 1:2, :])
    x1_ref[...] = x1
    h_ref[...] = (_rms(x1, nw_ref[0, 2:3, :]) * (1.0 + sc2) + sh2).astype(BF16)


def _outproj(attnT, four, x2d, wa, wf, nw, mod, *, tm):
    n = x2d.shape[0]
    const = lambda i: (0, 0)
    return pl.pallas_call(
        _outproj_kernel,
        out_shape=(jax.ShapeDtypeStruct((n, D_MODEL), F32), jax.ShapeDtypeStruct((n, D_MODEL), BF16)),
        grid=(n // tm,),
        in_specs=[pl.BlockSpec((ATTN_WIDTH, tm), lambda i: (0, i)),
                  pl.BlockSpec((tm, FOURIER_WIDTH), lambda i: (i, 0)),
                  pl.BlockSpec((tm, D_MODEL), lambda i: (i, 0)),
                  pl.BlockSpec((ATTN_WIDTH, D_MODEL), const),
                  pl.BlockSpec((FOURIER_WIDTH, D_MODEL), const),
                  pl.BlockSpec((1, 4, D_MODEL), lambda i: (0, 0, 0)),
                  pl.BlockSpec((1, 8, 6 * D_MODEL), lambda i: (0, 0, 0))],
        out_specs=(pl.BlockSpec((tm, D_MODEL), lambda i: (i, 0)),
                   pl.BlockSpec((tm, D_MODEL), lambda i: (i, 0))),
        compiler_params=_cparams(("parallel",)),
        name="outproj",
    )(attnT, four, x2d, wa, wf, nw, mod)


def _swiglu(x, wg, wu, wd):
    g = _dot(x, wg)
    return _dot((g * jax.nn.sigmoid(g) * _dot(x, wu)).astype(BF16), wd)


def _ffn_kernel(h_ref, x_ref, wg_ref, wu_ref, wd_ref, nw_ref, mod_ref, nw1_ref, mod1_ref, x2_ref, h3_ref):
    gt2 = _mod_rows(mod_ref, 0)[5]
    sh, sc = _mod_rows(mod1_ref, 0)[:2]
    y = _swiglu(h_ref[...], wg_ref[...], wu_ref[...], wd_ref[...])
    x2 = x_ref[...] + gt2 * _rms(y, nw_ref[0, 3:4, :])
    x2_ref[...] = x2
    h3_ref[...] = (_rms(x2, nw1_ref[0, 0:1, :]) * (1.0 + sc) + sh).astype(BF16)


def _ffn(h, x1, wg, wu, wd, nw, mod, nw1, mod1, *, tm):
    n = h.shape[0]
    nwspec = pl.BlockSpec((1, 4, D_MODEL), lambda i: (0, 0, 0))
    modspec = pl.BlockSpec((1, 8, 6 * D_MODEL), lambda i: (0, 0, 0))
    resident = dict(index_map=lambda i: (0, 0), pipeline_mode=pl.Buffered(1))
    return pl.pallas_call(
        _ffn_kernel,
        out_shape=(jax.ShapeDtypeStruct((n, D_MODEL), F32), jax.ShapeDtypeStruct((n, D_MODEL), BF16)),
        grid=(n // tm,),
        in_specs=[pl.BlockSpec((tm, D_MODEL), lambda i: (i, 0)),
                  pl.BlockSpec((tm, D_MODEL), lambda i: (i, 0)),
                  pl.BlockSpec((D_MODEL, D_FF), **resident),
                  pl.BlockSpec((D_MODEL, D_FF), **resident),
                  pl.BlockSpec((D_FF, D_MODEL), **resident),
                  nwspec, modspec, nwspec, modspec],
        out_specs=(pl.BlockSpec((tm, D_MODEL), lambda i: (i, 0)),
                   pl.BlockSpec((tm, D_MODEL), lambda i: (i, 0))),
        compiler_params=_cparams(("parallel",)),
        name="ffn",
    )(h, x1, wg, wu, wd, nw, mod, nw1, mod1)


EDGE_ROWS = 16


def _convedge_kernel(hf_ref, hl_ref, wc_ref, wu_ref, vf_ref, vl_ref):
    nt = hf_ref.shape[0]
    for h_ref, v_ref in ((hf_ref, vf_ref), (hl_ref, vl_ref)):
        h = h_ref[...].reshape(nt * EDGE_ROWS, D_MODEL)
        v = _dot(h, wc_ref[...]) * _dot(h, wu_ref[...])
        v_ref[...] = v.reshape(nt, EDGE_ROWS, D_MODEL)


def _convedge(h3, w_in_bf, *, tm):
    n = h3.shape[0]
    nt = n // tm
    h3t = h3.reshape(nt, tm, D_MODEL)
    last = tm // EDGE_ROWS - 1
    shape = jax.ShapeDtypeStruct((nt, EDGE_ROWS, D_MODEL), F32)
    return pl.pallas_call(
        _convedge_kernel,
        out_shape=(shape, shape),
        grid=(1,),
        in_specs=[pl.BlockSpec((nt, EDGE_ROWS, D_MODEL), lambda i: (0, 0, 0)),
                  pl.BlockSpec((nt, EDGE_ROWS, D_MODEL), lambda i: (0, last, 0)),
                  pl.BlockSpec((D_MODEL, D_MODEL), lambda i: (0, 1)),
                  pl.BlockSpec((D_MODEL, D_MODEL), lambda i: (0, 2))],
        out_specs=(pl.BlockSpec((nt, EDGE_ROWS, D_MODEL), lambda i: (0, 0, 0)),
                   pl.BlockSpec((nt, EDGE_ROWS, D_MODEL), lambda i: (0, 0, 0))),
        compiler_params=_cparams(("arbitrary",)),
        name="convedge",
    )(h3t, h3t, w_in_bf, w_in_bf)


def _convmix_kernel(h_ref, x_ref, vl_ref, vf_ref, win_ref, cw_ref, wout_ref, nw_ref, mod_ref, r_ref,
                    x3_ref, h4_ref, route_ref, cnt_ref):
    i = pl.program_id(0)
    tm = h_ref.shape[0]
    _, _, gt1, sh2, sc2, _ = _mod_rows(mod_ref, 0)
    z = _dot(h_ref[...], win_ref[...])
    b = z[:, :D_MODEL]
    v = z[:, D_MODEL:2 * D_MODEL] * z[:, 2 * D_MODEL:]
    has_prev = (i > 0).astype(F32)
    has_next = (i < pl.num_programs(0) - 1).astype(F32)
    prev_row = vl_ref[0, EDGE_ROWS - 1:EDGE_ROWS, :] * has_prev
    next_row = vf_ref[0, 0:1, :] * has_next
    rows = lax.broadcasted_iota(jnp.int32, (tm, 1), 0)
    v_dn = jnp.where(rows == 0, prev_row, pltpu.roll(v, 1, axis=0))
    v_up = jnp.where(rows == tm - 1, next_row, pltpu.roll(v, tm - 1, axis=0))
    conv = v_dn * cw_ref[0, 0:1, :] + v * cw_ref[0, 1:2, :] + v_up * cw_ref[0, 2:3, :]
    y = _dot((b * conv).astype(BF16), wout_ref[...])
    x3 = x_ref[...] + gt1 * _rms(y, nw_ref[0, 1:2, :])
    x3_ref[...] = x3
    h4 = _rms(x3, nw_ref[0, 2:3, :]) * (1.0 + sc2) + sh2
    h4_ref[...] = _pack_pairs(h4)
    hhi, hlo = _split_bf16(h4)
    rhi, rlo = _split_bf16(r_ref[...])
    logits = _dot(hhi, rhi) + (_dot(hlo, rhi) + _dot(hhi, rlo))
    lane = lax.broadcasted_iota(jnp.int32, logits.shape, 1)
    logits = jnp.where(lane < N_EXPERTS, logits, -jnp.inf)
    e = jnp.exp(logits - jnp.max(logits, axis=-1, keepdims=True))
    probs = e / jnp.sum(e, axis=-1, keepdims=True)
    v1 = jnp.max(probs, axis=-1, keepdims=True)
    i1 = jnp.min(jnp.where(probs == v1, lane, LANES), axis=-1, keepdims=True)
    rest = jnp.where(lane == i1, -1.0, probs)
    v2 = jnp.max(rest, axis=-1, keepdims=True)
    i2 = jnp.min(jnp.where(rest == v2, lane, LANES), axis=-1, keepdims=True)
    tot = v1 + v2
    sel = jnp.where(jnp.logical_or(lane == i1, lane == i2), 1.0, 0.0)
    earlier = lax.broadcasted_iota(jnp.int32, (tm, tm), 1) < lax.broadcasted_iota(jnp.int32, (tm, tm), 0)
    rank = _dot(jnp.where(earlier, 1.0, 0.0).astype(BF16), sel.astype(BF16))
    r1 = jnp.sum(jnp.where(lane == i1, rank, 0.0), axis=-1, keepdims=True)
    r2 = jnp.sum(jnp.where(lane == i2, rank, 0.0), axis=-1, keepdims=True)
    route_ref[...] = (jnp.where(lane == 0, i1.astype(F32), 0.0) + jnp.where(lane == 1, i2.astype(F32), 0.0)
                      + jnp.where(lane == 2, v1 / tot, 0.0) + jnp.where(lane == 3, v2 / tot, 0.0)
                      + jnp.where(lane == 4, r1, 0.0) + jnp.where(lane == 5, r2, 0.0))
    cnt_ref[0] = jnp.broadcast_to(jnp.sum(sel, axis=0, keepdims=True), cnt_ref.shape[1:])


def _convmix(h3, x2, v_first, v_last, w_in_bf, conv_w, w_out_bf, nw, mod, router_pad, *, tm):
    n = h3.shape[0]
    nt = n // tm
    const = lambda i: (0, 0)
    return pl.pallas_call(
        _convmix_kernel,
        out_shape=(jax.ShapeDtypeStruct((n, D_MODEL), F32), jax.ShapeDtypeStruct((n, D_MODEL // 2), jnp.uint32),
                   jax.ShapeDtypeStruct((n, LANES), F32), jax.ShapeDtypeStruct((nt, 8, LANES), F32)),
        grid=(nt,),
        in_specs=[pl.BlockSpec((tm, D_MODEL), lambda i: (i, 0)),
                  pl.BlockSpec((tm, D_MODEL), lambda i: (i, 0)),
                  pl.BlockSpec((1, EDGE_ROWS, D_MODEL), lambda i: (jnp.maximum(i - 1, 0), 0, 0)),
                  pl.BlockSpec((1, EDGE_ROWS, D_MODEL), lambda i: (jnp.minimum(i + 1, nt - 1), 0, 0)),
                  pl.BlockSpec((D_MODEL, 3 * D_MODEL), const),
                  pl.BlockSpec((1, 3, D_MODEL), lambda i: (0, 0, 0)),
                  pl.BlockSpec((D_MODEL, D_MODEL), const),
                  pl.BlockSpec((1, 4, D_MODEL), lambda i: (0, 0, 0)),
                  pl.BlockSpec((1, 8, 6 * D_MODEL), lambda i: (0, 0, 0)),
                  pl.BlockSpec((D_MODEL, LANES), const)],
        out_specs=(pl.BlockSpec((tm, D_MODEL), lambda i: (i, 0)),
                   pl.BlockSpec((tm, D_MODEL // 2), lambda i: (i, 0)),
                   pl.BlockSpec((tm, LANES), lambda i: (i, 0)),
                   pl.BlockSpec((1, 8, LANES), lambda i: (i, 0, 0))),
        compiler_params=_cparams(("parallel",)),
        name="convmix",
    )(h3, x2, v_last, v_first, w_in_bf, conv_w, w_out_bf, nw, mod, router_pad)


ROW_TILE = 512


def _count_le(sorted_vals, x):
    return jnp.sum((sorted_vals[None, :] <= x[:, None]).astype(jnp.int32), axis=1)


def _route(route, cnt_tiles, n_row_tiles):
    n = route.shape[0]
    i32 = jnp.int32
    cnt_tile = cnt_tiles[:, 0, :N_EXPERTS].astype(i32)
    cum_end = jnp.cumsum(cnt_tile, axis=0)
    cnt = cum_end[-1]
    cnt_pad = (cnt + ROW_TILE - 1) // ROW_TILE * ROW_TILE
    grp_end = jnp.cumsum(cnt_pad)
    start = grp_end - cnt_pad
    offs = jnp.repeat(start[None, :] + cum_end - cnt_tile, ROW_TILE, axis=0)
    experts = jnp.clip(route[:, :2].astype(i32), 0, N_EXPERTS - 1)
    picked = experts[:, :, None] == jnp.arange(N_EXPERTS, dtype=i32)[None, None, :]
    pos2 = jnp.sum(jnp.where(picked, offs[:, None, :], 0), axis=-1) + route[:, 4:6].astype(i32)
    pos2 = jnp.clip(pos2, 0, n_row_tiles * ROW_TILE - 1)
    r = jnp.arange(n_row_tiles, dtype=i32)
    base = r * ROW_TILE
    tile_valid = base < grp_end[-1]
    n_valid = jnp.clip(grp_end[-1] // ROW_TILE, 1, n_row_tiles)
    tile_exp = jnp.minimum(_count_le(grp_end, base), N_EXPERTS - 1)
    tile_exp = jnp.where(tile_valid, tile_exp, tile_exp[n_valid - 1])
    tile_src = jnp.where(tile_valid, r, n_valid - 1)
    tile_first = jnp.logical_and(tile_valid, base == start[tile_exp])
    tiles = (tile_exp, tile_src, tile_valid.astype(i32), tile_first.astype(i32))
    return pos2.T.reshape(-1), tiles


SC_CHUNK = 128


def _sc_workers():
    info = pltpu.get_tpu_info().sparse_core
    return info.num_cores, info.num_cores * info.num_subcores


def _sc_scatter_rows(rows, idx, n_out):
    n_src, d = rows.shape
    n_cores, n_workers = _sc_workers()
    per_worker = idx.shape[0] // n_workers
    assert idx.shape[0] % (n_workers * SC_CHUNK) == 0 and n_src % per_worker == 0
    idx2d = idx.reshape(-1, SC_CHUNK)
    mesh = plsc.VectorSubcoreMesh(core_axis_name="c", subcore_axis_name="s")

    @functools.partial(pl.kernel, mesh=mesh, out_type=jax.ShapeDtypeStruct((n_out, d), rows.dtype),
                       scratch_types=[pltpu.VMEM((1, SC_CHUNK), jnp.int32), pltpu.VMEM((SC_CHUNK, d), rows.dtype)])
    def scatter(rows_hbm, idx_hbm, out_hbm, idx_v, rows_v):
        wid = lax.axis_index("s") * n_cores + lax.axis_index("c")

        @pl.loop(0, per_worker // SC_CHUNK)
        def _(j):
            a = wid * per_worker + j * SC_CHUNK
            pltpu.sync_copy(idx_hbm.at[pl.ds(a // SC_CHUNK, 1)], idx_v)
            pltpu.sync_copy(rows_hbm.at[pl.ds(lax.rem(a, n_src), SC_CHUNK)], rows_v)
            pltpu.sync_copy(rows_v, out_hbm.at[idx_v.at[0]])

    return scatter(rows, idx2d)


def _sc_gather_rows(table, idx):
    d = table.shape[1]
    n_cores, n_workers = _sc_workers()
    per_worker = idx.shape[0] // n_workers
    assert idx.shape[0] % (n_workers * SC_CHUNK) == 0
    idx2d = idx.reshape(-1, SC_CHUNK)
    mesh = plsc.VectorSubcoreMesh(core_axis_name="c", subcore_axis_name="s")

    @functools.partial(pl.kernel, mesh=mesh, out_type=jax.ShapeDtypeStruct((idx.shape[0], d), table.dtype),
                       scratch_types=[pltpu.VMEM((1, SC_CHUNK), jnp.int32), pltpu.VMEM((SC_CHUNK, d), table.dtype)])
    def gather(table_hbm, idx_hbm, out_hbm, idx_v, rows_v):
        wid = lax.axis_index("s") * n_cores + lax.axis_index("c")

        @pl.loop(0, per_worker // SC_CHUNK)
        def _(j):
            a = wid * per_worker + j * SC_CHUNK
            pltpu.sync_copy(idx_hbm.at[pl.ds(a // SC_CHUNK, 1)], idx_v)
            pltpu.sync_copy(table_hbm.at[idx_v.at[0]], rows_v)
            pltpu.sync_copy(rows_v, out_hbm.at[pl.ds(a, SC_CHUNK)])

    return gather(table, idx2d)


def _experts_kernel(t_exp, t_src, t_valid, t_first, xs_ref, wg_ref, wu_ref, wd_ref, ys_ref,
                    wg_c, wu_c, wd_c, x_sc, acc_ref):
    r = pl.program_id(0)
    c = pl.program_id(1)
    last = c == pl.num_programs(1) - 1

    @pl.when(t_first[r] != 0)
    def _():
        wg_c[c] = wg_ref[0, 0].astype(BF16)
        wu_c[c] = wu_ref[0, 0].astype(BF16)
        wd_c[c] = wd_ref[0, 0].astype(BF16)

    @pl.when(t_valid[r] != 0)
    def _():
        @pl.when(c == 0)
        def _():
            x_sc[...] = _unpack_pairs(xs_ref[...])
            acc_ref[...] = jnp.zeros(acc_ref.shape, F32)

        acc_ref[...] += _swiglu(x_sc[...], wg_c[c], wu_c[c], wd_c[c])

        @pl.when(last)
        def _():
            ys_ref[...] = _pack_pairs(acc_ref[...])

    @pl.when(jnp.logical_and(t_valid[r] == 0, last))
    def _():
        ys_ref[...] = jnp.zeros(ys_ref.shape, ys_ref.dtype)


def _experts(tiles, xs, wg, wu, wd, *, tf):
    n_rows, half = xs.shape
    n_ch = D_EXPERT // tf

    def w_in(r, c, e, s, v, f):
        return (0, e[r], 0, jnp.where(f[r] != 0, c, n_ch - 1))

    def w_dn(r, c, e, s, v, f):
        return (0, e[r], jnp.where(f[r] != 0, c, n_ch - 1), 0)

    grid_spec = pltpu.PrefetchScalarGridSpec(
        num_scalar_prefetch=4, grid=(n_rows // ROW_TILE, n_ch),
        in_specs=[pl.BlockSpec((ROW_TILE, half), lambda r, c, e, s, v, f: (s[r], 0)),
                  pl.BlockSpec((1, 1, D_MODEL, tf), w_in),
                  pl.BlockSpec((1, 1, D_MODEL, tf), w_in),
                  pl.BlockSpec((1, 1, tf, D_MODEL), w_dn)],
        out_specs=pl.BlockSpec((ROW_TILE, half), lambda r, c, e, s, v, f: (r, 0)),
        scratch_shapes=[pltpu.VMEM((n_ch, D_MODEL, tf), BF16), pltpu.VMEM((n_ch, D_MODEL, tf), BF16),
                        pltpu.VMEM((n_ch, tf, D_MODEL), BF16), pltpu.VMEM((ROW_TILE, D_MODEL), BF16),
                        pltpu.VMEM((ROW_TILE, D_MODEL), F32)])
    return pl.pallas_call(
        _experts_kernel, grid_spec=grid_spec,
        out_shape=jax.ShapeDtypeStruct((n_rows, half), jnp.uint32),
        compiler_params=_cparams(("arbitrary", "arbitrary")),
        name="moe_experts",
    )(*tiles, xs, wg, wu, wd)


def _combine_kernel(a_ref, b_ref, route_ref, x_ref, nw_ref, mod_ref, o_ref):
    route = route_ref[...]
    y = route[:, 2:3] * _unpack_pairs(a_ref[...]).astype(F32) + route[:, 3:4] * _unpack_pairs(b_ref[...]).astype(F32)
    gt2 = _mod_rows(mod_ref, 0)[5]
    o_ref[...] = x_ref[...] + gt2 * _rms(y, nw_ref[0, 3:4, :])


def _combine(rows, route, x3, nw, mod):
    n = x3.shape[0]
    nt = n // ROW_TILE
    half = rows.shape[1]
    return pl.pallas_call(
        _combine_kernel,
        out_shape=jax.ShapeDtypeStruct((n, D_MODEL), F32),
        grid=(nt,),
        in_specs=[pl.BlockSpec((ROW_TILE, half), lambda i: (i, 0)),
                  pl.BlockSpec((ROW_TILE, half), lambda i: (i + nt, 0)),
                  pl.BlockSpec((ROW_TILE, LANES), lambda i: (i, 0)),
                  pl.BlockSpec((ROW_TILE, D_MODEL), lambda i: (i, 0)),
                  pl.BlockSpec((1, 4, D_MODEL), lambda i: (0, 0, 0)),
                  pl.BlockSpec((1, 8, 6 * D_MODEL), lambda i: (0, 0, 0))],
        out_specs=pl.BlockSpec((ROW_TILE, D_MODEL), lambda i: (i, 0)),
        compiler_params=_cparams(("parallel",)),
        name="moe_combine",
    )(rows, rows, route, x3, nw, mod)


def _moe(h4p, route, cnt_tiles, x3, wg, wu, wd, nw, mod):
    n = h4p.shape[0]
    n_rows = 2 * n + N_EXPERTS * ROW_TILE
    pos, tiles = _route(route, cnt_tiles, n_rows // ROW_TILE)
    xs = _sc_scatter_rows(h4p, pos, n_rows)
    ys = _experts(tiles, xs, wg, wu, wd, tf=512)
    rows = _sc_gather_rows(ys, pos)
    return _combine(rows, route, x3, nw, mod)


def kernel(x, c, ctx, c_ctx, ada_w, ada_b, norm_w, e_w_in, e_q_gain, e_k_gain, e_w_out, e_ffn_gate,
           e_ffn_up, e_ffn_down, o_w_in, o_conv_w, o_w_out, o_router, o_exp_gate, o_exp_up, o_exp_down):
    assert x.shape[0] == 1 and x.shape[2] == D_MODEL and ada_w.shape[0] == 2
    n = x.shape[1]
    x2d = x[0]
    ctx2d = ctx[0]
    mod = _ada(c, c_ctx, ada_w, ada_b)
    mod0, mod1 = mod[0:1], mod[1:2]
    nw0, nw1 = norm_w[0:1], norm_w[1:2]

    w_in = e_w_in[0].astype(BF16)
    scale = HEAD_DIM ** -0.5 * np.log2(np.e)
    gain = jnp.concatenate([jnp.tile(e_q_gain[0], N_Q_HEADS) * scale,
                            jnp.tile(e_k_gain[0], N_KV_HEADS)])[None, :]
    score_bound = 1.02 * HEAD_DIM * scale * jnp.max(jnp.abs(e_q_gain[0])) * jnp.max(jnp.abs(e_k_gain[0]))

    def mixer_inputs(fast):
        qk_dtype = FP8 if fast else BF16
        qT, k, vT, f = _evenproj(x2d, nw0, mod0, w_in, gain, latent=True, tm=512, qk_dtype=qk_dtype)
        kc, vcT = _evenproj(ctx2d, nw0, mod0, w_in, gain, latent=False, tm=ctx2d.shape[0], qk_dtype=qk_dtype)
        return _attention(qT, k, vT, kc, vcT, bounded=fast, tq=256, tk=1024), f

    attnT, f = lax.cond(score_bound <= FAST_SCORE_BOUND, functools.partial(mixer_inputs, True),
                        functools.partial(mixer_inputs, False))
    four = _fourier(f)
    w_out = e_w_out[0].astype(BF16)
    x1, h2 = _outproj(attnT, four, x2d, w_out[:ATTN_WIDTH], w_out[ATTN_WIDTH:], nw0, mod0, tm=512)
    x2, h3 = _ffn(h2, x1, e_ffn_gate[0].astype(BF16), e_ffn_up[0].astype(BF16),
                  e_ffn_down[0].astype(BF16), nw0, mod0, nw1, mod1, tm=512)

    ow_in = o_w_in[0].astype(BF16)
    v_first, v_last = _convedge(h3, ow_in, tm=ROW_TILE)
    router_pad = jnp.pad(o_router[0], ((0, 0), (0, LANES - N_EXPERTS)))
    x3, h4p, route, cnt_tiles = _convmix(h3, x2, v_first, v_last, ow_in, o_conv_w, o_w_out[0].astype(BF16),
                                         nw1, mod1, router_pad, tm=ROW_TILE)
    out = _moe(h4p, route, cnt_tiles, x3, o_exp_gate, o_exp_up, o_exp_down, nw1, mod1)
    return out[None]
```

```python
import functools

import numpy as np
import jax
import jax.numpy as jnp
from jax import lax
from jax.experimental import pallas as pl
from jax.experimental.pallas import tpu as pltpu
from jax.experimental.pallas import tpu_sc as plsc

D_MODEL = 1024
GRID_W = 64
HEAD_DIM = 64
N_Q_HEADS = 12
N_KV_HEADS = 4
Q_PER_KV = N_Q_HEADS // N_KV_HEADS
ATTN_WIDTH = N_Q_HEADS * HEAD_DIM
KV_WIDTH = N_KV_HEADS * HEAD_DIM
QK_WIDTH = ATTN_WIDTH + KV_WIDTH
N_FOURIER_GROUPS = 4
FOURIER_GROUP_DIM = 64
FOURIER_WIDTH = N_FOURIER_GROUPS * FOURIER_GROUP_DIM
EVEN_IN_WIDTH = ATTN_WIDTH + 2 * KV_WIDTH + FOURIER_WIDTH
D_FF = 2816
N_EXPERTS = 8
D_EXPERT = 3584
ONES_ROWS = 16
V_ROWS = HEAD_DIM + ONES_ROWS
FAST_SCORE_BOUND = 24.0
ROPE_THETA = 10000.0
ROPE_HALF = HEAD_DIM // 4
NORM_EPS = 1e-6

LANES = 128
VMEM_LIMIT = 56 * 1024 * 1024

BF16 = jnp.bfloat16
FP8 = jnp.float8_e4m3fn
F32 = jnp.float32


def _cparams(semantics, vmem=VMEM_LIMIT):
    return pltpu.CompilerParams(dimension_semantics=semantics, vmem_limit_bytes=vmem)


def _dot(a, b):
    return jnp.dot(a, b, preferred_element_type=F32)


def _split_bf16(x):
    hi = x.astype(BF16)
    lo = (x - hi.astype(F32)).astype(BF16)
    return hi, lo


def _rms(x, g):
    return x * lax.rsqrt(jnp.mean(x * x, axis=-1, keepdims=True) + NORM_EPS) * g


def _mod_rows(mod_ref, row):
    return [mod_ref[0, row:row + 1, i * D_MODEL:(i + 1) * D_MODEL] for i in range(6)]


def _pack_pairs(x):
    k = x.shape[1] // 2
    bits = lax.bitcast_convert_type(x.astype(BF16).astype(F32), jnp.uint32)
    return (bits[:, :k] >> 16) | (bits[:, k:] & jnp.uint32(0xFFFF0000))


def _unpack_pairs(w):
    lo = lax.bitcast_convert_type(w << 16, F32).astype(BF16)
    hi = lax.bitcast_convert_type(w & jnp.uint32(0xFFFF0000), F32).astype(BF16)
    return jnp.concatenate([lo, hi], axis=1)


@functools.lru_cache(maxsize=None)
def _rope_tables(n_tokens):
    t = np.arange(n_tokens)
    row = (t // GRID_W).astype(np.float64)
    col = (t % GRID_W).astype(np.float64)
    inv = ROPE_THETA ** (-np.arange(ROPE_HALF, dtype=np.float64) / ROPE_HALF)
    ar, ac = row[:, None] * inv, col[:, None] * inv
    cos = np.concatenate([np.cos(ar), np.cos(ar), np.cos(ac), np.cos(ac)], axis=-1)
    sin = np.concatenate([-np.sin(ar), np.sin(ar), -np.sin(ac), np.sin(ac)], axis=-1)
    reps = LANES // HEAD_DIM
    return (np.tile(cos, (1, reps)).astype(np.float32), np.tile(sin, (1, reps)).astype(np.float32))


@functools.lru_cache(maxsize=None)
def _head_matrices():
    head = np.arange(QK_WIDTH) // HEAD_DIM
    red = (head[:, None] == np.arange(LANES)[None, :]).astype(np.float32) / HEAD_DIM
    exp = (np.arange(LANES)[:, None] == head[None, :]).astype(np.float32)
    return red, exp


@functools.lru_cache(maxsize=None)
def _fourier_tables(n_tokens, kb):
    n2 = LANES
    n1 = n_tokens // n2
    c = np.arange(FOURIER_GROUP_DIM)
    ang = 2 * np.pi * np.outer(c, c) / FOURIER_GROUP_DIM
    eye = np.eye(N_FOURIER_GROUPS)
    cs = np.concatenate([np.kron(eye, np.cos(ang)), np.kron(eye, np.sin(ang))], axis=1)
    k1 = np.arange(n1)
    th = 2 * np.pi * np.outer(k1, k1) / n1
    cr, ci = np.cos(th), -np.sin(th)
    base = np.block([[cr, ci], [ci, -cr]])
    psi = 2 * np.pi * np.outer(k1, np.arange(n2)) / n_tokens
    twr = np.cos(psi).reshape(n1, n2 // kb, kb).transpose(1, 0, 2)
    twi = (-np.sin(psi)).reshape(n1, n2 // kb, kb).transpose(1, 0, 2)
    k2 = np.arange(n2)
    ph = 2 * np.pi * np.outer(k2, k2) / n2
    fr, fi = np.cos(ph), -np.sin(ph)
    scale = 1.0 / np.sqrt(n_tokens * FOURIER_GROUP_DIM)
    m3 = np.stack([fr, -fi], axis=-1) * scale
    wb = np.einsum('knr,uv->kunrv', m3, np.eye(kb)).reshape(n2 * kb, n2 * 2 * kb)
    f32 = np.float32
    return cs.astype(f32), base.astype(f32), twr.astype(f32), twi.astype(f32), wb.astype(f32)


def _ada_kernel(cb_ref, w_ref, b_ref, o_ref):
    tn = o_ref.shape[-1]
    o_ref[...] = jnp.zeros(o_ref.shape, F32)
    for r in range(2):
        cb = cb_ref[r]
        s = cb * jax.nn.sigmoid(cb)
        for j in range(tn // LANES):
            sl = slice(j * LANES, (j + 1) * LANES)
            col = jnp.sum(s * w_ref[0, :, sl], axis=0, keepdims=True)
            o_ref[0, r:r + 1, sl] = col + b_ref[0, :, sl]


def _ada(c, c_ctx, ada_w, ada_b):
    depth = ada_w.shape[0]
    n = ada_w.shape[-1]
    tn = 1536
    cb = jnp.stack([jnp.broadcast_to(c[0][:, None], (D_MODEL, LANES)),
                    jnp.broadcast_to(c_ctx[:, None], (D_MODEL, LANES))])
    return pl.pallas_call(
        _ada_kernel,
        out_shape=jax.ShapeDtypeStruct((depth, 8, n), F32),
        grid=(depth, n // tn),
        in_specs=[pl.BlockSpec((2, D_MODEL, LANES), lambda i, j: (0, 0, 0)),
                  pl.BlockSpec((1, D_MODEL, tn), lambda i, j: (i, 0, j)),
                  pl.BlockSpec((1, 1, tn), lambda i, j: (i, 0, j))],
        out_specs=pl.BlockSpec((1, 8, tn), lambda i, j: (i, 0, j)),
        compiler_params=_cparams(("parallel", "parallel")),
        name="ada",
    )(cb, ada_w, ada_b[:, None, :])


def _evenproj_kernel(*refs, row, latent, qk_dtype):
    if latent:
        (x_ref, nw_ref, mod_ref, w_ref, gain_ref, red_ref, exp_ref, cos_ref, sin_ref,
         qT_ref, k_ref, vT_ref, f_ref) = refs
    else:
        x_ref, nw_ref, mod_ref, w_ref, gain_ref, red_ref, exp_ref, k_ref, vT_ref = refs
    sh, sc = _mod_rows(mod_ref, row)[:2]
    red = red_ref[...].astype(BF16)
    expm = exp_ref[...].astype(BF16)
    tm = x_ref.shape[0]
    n_parts = 2 if latent else 1
    m = tm // n_parts

    def matmuls(r0):
        h = (_rms(x_ref[r0:r0 + m, :], nw_ref[0, 0:1, :]) * (1.0 + sc) + sh).astype(BF16)
        z = _dot(h, w_ref[...])
        zqk = z[:, :QK_WIDTH]
        hi, lo = _split_bf16(zqk * zqk)
        ms = _dot(hi, red) + _dot(lo, red)
        rhi, rlo = _split_bf16(lax.rsqrt(ms + NORM_EPS))
        return z, zqk * (_dot(rhi, expm) + _dot(rlo, expm)) * gain_ref[...]

    def finish(r0, z, yn):
        if latent:
            lane = lax.broadcasted_iota(jnp.int32, (1, LANES), 1)
            first_half = (lane // ROPE_HALF) % 2 == 0
            cos, sin = cos_ref[r0:r0 + m, :], sin_ref[r0:r0 + m, :]
            chunks = []
            for c in range(QK_WIDTH // LANES):
                yc = yn[:, c * LANES:(c + 1) * LANES]
                partner = jnp.where(first_half, pltpu.roll(yc, LANES - ROPE_HALF, axis=1),
                                    pltpu.roll(yc, ROPE_HALF, axis=1))
                chunks.append(yc * cos + partner * sin)
            yn = jnp.concatenate(chunks, axis=1)
            qT_ref[:, r0:r0 + m] = yn[:, :ATTN_WIDTH].T.astype(qk_dtype)
            f_ref[r0:r0 + m, :] = z[:, QK_WIDTH + KV_WIDTH:].astype(BF16)
        for g in range(N_KV_HEADS):
            k_ref[g, r0:r0 + m, :] = yn[:, ATTN_WIDTH + g * HEAD_DIM:ATTN_WIDTH + (g + 1) * HEAD_DIM].astype(qk_dtype)
        vT = z[:, QK_WIDTH:QK_WIDTH + KV_WIDTH].T.astype(BF16)
        ones = jnp.ones((ONES_ROWS, m), BF16)
        for g in range(N_KV_HEADS):
            vT_ref[g * V_ROWS:g * V_ROWS + HEAD_DIM, r0:r0 + m] = vT[g * HEAD_DIM:(g + 1) * HEAD_DIM]
            vT_ref[g * V_ROWS + HEAD_DIM:(g + 1) * V_ROWS, r0:r0 + m] = ones

    parts = [(p * m,) + matmuls(p * m) for p in range(n_parts)]
    for r0, z, yn in parts:
        finish(r0, z, yn)


def _evenproj(x2d, nw, mod, w_bf, gain, *, latent, tm, qk_dtype):
    n = x2d.shape[0]
    red, expm = _head_matrices()
    const = lambda i: (0, 0)
    in_specs = [pl.BlockSpec((tm, D_MODEL), lambda i: (i, 0)),
                pl.BlockSpec((1, 4, D_MODEL), lambda i: (0, 0, 0)),
                pl.BlockSpec((1, 8, 6 * D_MODEL), lambda i: (0, 0, 0)),
                pl.BlockSpec((D_MODEL, EVEN_IN_WIDTH), const),
                pl.BlockSpec((1, QK_WIDTH), const),
                pl.BlockSpec((QK_WIDTH, LANES), const),
                pl.BlockSpec((LANES, QK_WIDTH), const)]
    args = [x2d, nw, mod, w_bf, gain, jnp.asarray(red), jnp.asarray(expm)]
    k_shape = jax.ShapeDtypeStruct((N_KV_HEADS, n, HEAD_DIM), qk_dtype)
    vT_shape = jax.ShapeDtypeStruct((N_KV_HEADS * V_ROWS, n), BF16)
    k_spec = pl.BlockSpec((N_KV_HEADS, tm, HEAD_DIM), lambda i: (0, i, 0))
    vT_spec = pl.BlockSpec((N_KV_HEADS * V_ROWS, tm), lambda i: (0, i))
    if latent:
        cos, sin = _rope_tables(n)
        in_specs += [pl.BlockSpec((tm, LANES), lambda i: (i, 0))] * 2
        args += [jnp.asarray(cos), jnp.asarray(sin)]
        out_shape = (jax.ShapeDtypeStruct((ATTN_WIDTH, n), qk_dtype), k_shape, vT_shape,
                     jax.ShapeDtypeStruct((n, FOURIER_WIDTH), BF16))
        out_specs = (pl.BlockSpec((ATTN_WIDTH, tm), lambda i: (0, i)), k_spec, vT_spec,
                     pl.BlockSpec((tm, FOURIER_WIDTH), lambda i: (i, 0)))
    else:
        out_shape = (k_shape, vT_shape)
        out_specs = (k_spec, vT_spec)
    return pl.pallas_call(
        functools.partial(_evenproj_kernel, row=0 if latent else 1, latent=latent, qk_dtype=qk_dtype),
        out_shape=out_shape, grid=(n // tm,), in_specs=in_specs, out_specs=out_specs,
        compiler_params=_cparams(("parallel",)),
        name="evenproj_lat" if latent else "evenproj_ctx",
    )(*args)


def _visit_all(visit, k_ref, vT_ref, kc_ref, vcT_ref, tk):
    def body(c, carry):
        off = pl.multiple_of(c * tk, tk)
        visit(k_ref[0, pl.ds(off, tk), :], vT_ref[:, pl.ds(off, tk)])
        return carry

    lax.fori_loop(0, k_ref.shape[1] // tk, body, 0)
    visit(kc_ref[0], vcT_ref[...])


def _attn_bounded_kernel(qT_ref, k_ref, vT_ref, kc_ref, vcT_ref, o_ref, acc_sc, s_sc, *, tk):
    n_tiles = k_ref.shape[1] // tk
    acc_sc[...] = jnp.zeros(acc_sc.shape, F32)

    def q(j):
        return qT_ref[j * HEAD_DIM:(j + 1) * HEAD_DIM, :]

    def keys(c):
        return k_ref[0, pl.ds(pl.multiple_of(c * tk, tk), tk), :]

    def consume(j, s, vt):
        acc_sc[j] += _dot(vt, jnp.exp2(s).astype(BF16))

    s_sc[...] = _dot(keys(0), q(0))

    def body(c, carry):
        kt = keys(c)
        vt = vT_ref[:, pl.ds(pl.multiple_of(c * tk, tk), tk)]
        s = s_sc[...]
        for j in range(Q_PER_KV):
            if j + 1 < Q_PER_KV:
                s_next = _dot(kt, q(j + 1))
            else:
                s_next = _dot(keys(jnp.minimum(c + 1, n_tiles - 1)), q(0))
            consume(j, s, vt)
            s = s_next
        s_sc[...] = s
        return carry

    lax.fori_loop(0, n_tiles, body, 0, unroll=16)
    kc, vc = kc_ref[0], vcT_ref[...]
    s = _dot(kc, q(0))
    for j in range(Q_PER_KV):
        s_next = _dot(kc, q(j + 1)) if j + 1 < Q_PER_KV else None
        consume(j, s, vc)
        s = s_next
    for j in range(Q_PER_KV):
        acc = acc_sc[j]
        o_ref[j * HEAD_DIM:(j + 1) * HEAD_DIM, :] = (acc[:HEAD_DIM] / acc[HEAD_DIM:HEAD_DIM + 1]).astype(BF16)


def _attn_online_kernel(qT_ref, k_ref, vT_ref, kc_ref, vcT_ref, o_ref, m_sc, acc_sc, *, tk):
    m_sc[...] = jnp.full(m_sc.shape, -jnp.inf, F32)
    acc_sc[...] = jnp.zeros(acc_sc.shape, F32)

    def visit(kt, vt):
        for j in range(Q_PER_KV):
            s = _dot(kt, qT_ref[j * HEAD_DIM:(j + 1) * HEAD_DIM, :])
            m_old = m_sc[j]
            m_new = jnp.maximum(m_old, jnp.max(s, axis=0, keepdims=True))
            p = jnp.exp2(s - m_new).astype(BF16)
            acc_sc[j] = jnp.exp2(m_old - m_new) * acc_sc[j] + _dot(vt, p)
            m_sc[j] = m_new

    _visit_all(visit, k_ref, vT_ref, kc_ref, vcT_ref, tk)
    for j in range(Q_PER_KV):
        acc = acc_sc[j]
        o_ref[j * HEAD_DIM:(j + 1) * HEAD_DIM, :] = (acc[:HEAD_DIM] / acc[HEAD_DIM:HEAD_DIM + 1]).astype(BF16)


def _attention(qT, k, vT, kc, vcT, *, bounded, tq, tk):
    n = qT.shape[1]
    n_ctx = kc.shape[1]
    gw = Q_PER_KV * HEAD_DIM
    common = dict(
        out_shape=jax.ShapeDtypeStruct((ATTN_WIDTH, n), BF16),
        grid=(N_KV_HEADS, n // tq),
        in_specs=[pl.BlockSpec((gw, tq), lambda g, i: (g, i)),
                  pl.BlockSpec((1, n, HEAD_DIM), lambda g, i: (g, 0, 0)),
                  pl.BlockSpec((V_ROWS, n), lambda g, i: (g, 0)),
                  pl.BlockSpec((1, n_ctx, HEAD_DIM), lambda g, i: (g, 0, 0)),
                  pl.BlockSpec((V_ROWS, n_ctx), lambda g, i: (g, 0))],
        out_specs=pl.BlockSpec((gw, tq), lambda g, i: (g, i)),
        compiler_params=_cparams(("parallel", "parallel")),
    )
    acc = pltpu.VMEM((Q_PER_KV, V_ROWS, tq), F32)
    if bounded:
        call = pl.pallas_call(functools.partial(_attn_bounded_kernel, tk=tk),
                              scratch_shapes=[acc, pltpu.VMEM((tk, tq), F32)], name="attn_bounded", **common)
    else:
        call = pl.pallas_call(functools.partial(_attn_online_kernel, tk=tk),
                              scratch_shapes=[pltpu.VMEM((Q_PER_KV, 1, tq), F32), acc],
                              name="attn_online", **common)
    return call(qT, k, vT, kc, vcT)


def _four_a_kernel(f_ref, cs_ref, base_ref, twr_ref, twi_ref, y_ref, *, nb):
    n1 = f_ref.shape[0]
    cs = cs_ref[...].astype(BF16)
    base = base_ref[...].astype(BF16)
    for u in range(nb):
        xb = f_ref[:, u * FOURIER_WIDTH:(u + 1) * FOURIER_WIDTH]
        ab = _dot(xb, cs)
        stacked = jnp.concatenate([ab[:, :FOURIER_WIDTH], ab[:, FOURIER_WIDTH:]], axis=0)
        p = _dot(base, stacked.astype(BF16))
        pr, pi = p[:n1], p[n1:]
        tr = twr_ref[0, :, u:u + 1]
        ti = twi_ref[0, :, u:u + 1]
        y_ref[u, 0] = tr * pr - ti * pi
        y_ref[u, 1] = tr * pi + ti * pr


def _four_b_kernel(y_ref, wb_ref, o_ref):
    n2, _, kb, w = y_ref.shape
    y = y_ref[...].reshape(n2 * 2 * kb, w).astype(BF16)
    o_ref[...] = _dot(wb_ref[...].astype(BF16), y).reshape(n2, kb, w)


def _fourier(f):
    n = f.shape[0]
    n2 = LANES
    n1 = n // n2
    nb = kb = 8
    cs, base, twr, twi, wb = (jnp.asarray(t) for t in _fourier_tables(n, kb))
    f2d = f.reshape(n1, n2 * FOURIER_WIDTH)
    y = pl.pallas_call(
        functools.partial(_four_a_kernel, nb=nb),
        out_shape=jax.ShapeDtypeStruct((n2, 2, n1, FOURIER_WIDTH), F32),
        grid=(n2 // nb,),
        in_specs=[pl.BlockSpec((n1, nb * FOURIER_WIDTH), lambda s: (0, s)),
                  pl.BlockSpec(cs.shape, lambda s: (0, 0)),
                  pl.BlockSpec(base.shape, lambda s: (0, 0)),
                  pl.BlockSpec((1, n1, nb), lambda s: (s, 0, 0)),
                  pl.BlockSpec((1, n1, nb), lambda s: (s, 0, 0))],
        out_specs=pl.BlockSpec((nb, 2, n1, FOURIER_WIDTH), lambda s: (s, 0, 0, 0)),
        compiler_params=_cparams(("parallel",)),
        name="four_a",
    )(f2d, cs, base, twr, twi)
    out = pl.pallas_call(
        _four_b_kernel,
        out_shape=jax.ShapeDtypeStruct((n2, n1, FOURIER_WIDTH), F32),
        grid=(n1 // kb,),
        in_specs=[pl.BlockSpec((n2, 2, kb, FOURIER_WIDTH), lambda s: (0, 0, s, 0)),
                  pl.BlockSpec(wb.shape, lambda s: (0, 0))],
        out_specs=pl.BlockSpec((n2, kb, FOURIER_WIDTH), lambda s: (0, s, 0)),
        compiler_params=_cparams(("parallel",)),
        name="four_b",
    )(y, wb)
    return out.reshape(n, FOURIER_WIDTH)


def _outproj_kernel(aT_ref, four_ref, x_ref, wa_ref, wf_ref, nw_ref, mod_ref, x1_ref, h_ref):
    _, _, gt1, sh2, sc2, _ = _mod_rows(mod_ref, 0)
    half = x_ref.shape[0] // 2
    starts = (0, half)
    ys = [lax.dot_general(aT_ref[:, r0:r0 + half], wa_ref[...], (((0,), (0,)), ((), ())),
                          preferred_element_type=F32)
          + _dot(four_ref[r0:r0 + half, :].astype(BF16), wf_ref[...]) for r0 in starts]
    for r0, y in zip(starts, ys):
        x1 = x_ref[r0:r0 + half, :] + gt1 * _rms(y, nw_ref[0, 1:2, :])
        x1_ref[r0:r0 + half, :] = x1
        h_ref[r0:r0 + half, :] = (_rms(x1, nw_ref[0, 2:3, :]) * (1.0 + sc2) + sh2).astype(BF16)


def _outproj(attnT, four, x2d, wa, wf, nw, mod, *, tm):
    n = x2d.shape[0]
    const = lambda i: (0, 0)
    return pl.pallas_call(
        _outproj_kernel,
        out_shape=(jax.ShapeDtypeStruct((n, D_MODEL), F32), jax.ShapeDtypeStruct((n, D_MODEL), BF16)),
        grid=(n // tm,),
        in_specs=[pl.BlockSpec((ATTN_WIDTH, tm), lambda i: (0, i)),
                  pl.BlockSpec((tm, FOURIER_WIDTH), lambda i: (i, 0)),
                  pl.BlockSpec((tm, D_MODEL), lambda i: (i, 0)),
                  pl.BlockSpec((ATTN_WIDTH, D_MODEL), const),
                  pl.BlockSpec((FOURIER_WIDTH, D_MODEL), const),
                  pl.BlockSpec((1, 4, D_MODEL), lambda i: (0, 0, 0)),
                  pl.BlockSpec((1, 8, 6 * D_MODEL), lambda i: (0, 0, 0))],
        out_specs=(pl.BlockSpec((tm, D_MODEL), lambda i: (i, 0)),
                   pl.BlockSpec((tm, D_MODEL), lambda i: (i, 0))),
        compiler_params=_cparams(("parallel",)),
        name="outproj",
    )(attnT, four, x2d, wa, wf, nw, mod)


def _swiglu(x, wg, wu, wd):
    g = _dot(x, wg)
    return _dot((g * jax.nn.sigmoid(g) * _dot(x, wu)).astype(BF16), wd)


def _ffn_kernel(h_ref, x_ref, wg_ref, wu_ref, wd_ref, nw_ref, mod_ref, nw1_ref, mod1_ref, x2_ref, h3_ref):
    gt2 = _mod_rows(mod_ref, 0)[5]
    sh, sc = _mod_rows(mod1_ref, 0)[:2]
    half = x_ref.shape[0] // 2
    starts = (0, half)
    ys = [_swiglu(h_ref[r0:r0 + half, :], wg_ref[...], wu_ref[...], wd_ref[...]) for r0 in starts]
    for r0, y in zip(starts, ys):
        x2 = x_ref[r0:r0 + half, :] + gt2 * _rms(y, nw_ref[0, 3:4, :])
        x2_ref[r0:r0 + half, :] = x2
        h3_ref[r0:r0 + half, :] = (_rms(x2, nw1_ref[0, 0:1, :]) * (1.0 + sc) + sh).astype(BF16)


def _ffn(h, x1, wg, wu, wd, nw, mod, nw1, mod1, *, tm):
    n = h.shape[0]
    nwspec = pl.BlockSpec((1, 4, D_MODEL), lambda i: (0, 0, 0))
    modspec = pl.BlockSpec((1, 8, 6 * D_MODEL), lambda i: (0, 0, 0))
    resident = dict(index_map=lambda i: (0, 0), pipeline_mode=pl.Buffered(1))
    return pl.pallas_call(
        _ffn_kernel,
        out_shape=(jax.ShapeDtypeStruct((n, D_MODEL), F32), jax.ShapeDtypeStruct((n, D_MODEL), BF16)),
        grid=(n // tm,),
        in_specs=[pl.BlockSpec((tm, D_MODEL), lambda i: (i, 0)),
                  pl.BlockSpec((tm, D_MODEL), lambda i: (i, 0)),
                  pl.BlockSpec((D_MODEL, D_FF), **resident),
                  pl.BlockSpec((D_MODEL, D_FF), **resident),
                  pl.BlockSpec((D_FF, D_MODEL), **resident),
                  nwspec, modspec, nwspec, modspec],
        out_specs=(pl.BlockSpec((tm, D_MODEL), lambda i: (i, 0)),
                   pl.BlockSpec((tm, D_MODEL), lambda i: (i, 0))),
        compiler_params=_cparams(("parallel",)),
        name="ffn",
    )(h, x1, wg, wu, wd, nw, mod, nw1, mod1)


EDGE_ROWS = 16


def _convedge_kernel(hf_ref, hl_ref, wc_ref, wu_ref, vf_ref, vl_ref):
    nt = hf_ref.shape[0]
    for h_ref, v_ref in ((hf_ref, vf_ref), (hl_ref, vl_ref)):
        h = h_ref[...].reshape(nt * EDGE_ROWS, D_MODEL)
        v = _dot(h, wc_ref[...]) * _dot(h, wu_ref[...])
        v_ref[...] = v.reshape(nt, EDGE_ROWS, D_MODEL)


def _convedge(h3, w_in_bf, *, tm):
    n = h3.shape[0]
    nt = n // tm
    h3t = h3.reshape(nt, tm, D_MODEL)
    last = tm // EDGE_ROWS - 1
    shape = jax.ShapeDtypeStruct((nt, EDGE_ROWS, D_MODEL), F32)
    return pl.pallas_call(
        _convedge_kernel,
        out_shape=(shape, shape),
        grid=(1,),
        in_specs=[pl.BlockSpec((nt, EDGE_ROWS, D_MODEL), lambda i: (0, 0, 0)),
                  pl.BlockSpec((nt, EDGE_ROWS, D_MODEL), lambda i: (0, last, 0)),
                  pl.BlockSpec((D_MODEL, D_MODEL), lambda i: (0, 1)),
                  pl.BlockSpec((D_MODEL, D_MODEL), lambda i: (0, 2))],
        out_specs=(pl.BlockSpec((nt, EDGE_ROWS, D_MODEL), lambda i: (0, 0, 0)),
                   pl.BlockSpec((nt, EDGE_ROWS, D_MODEL), lambda i: (0, 0, 0))),
        compiler_params=_cparams(("arbitrary",)),
        name="convedge",
    )(h3t, h3t, w_in_bf, w_in_bf)


def _convmix_kernel(h_ref, x_ref, vl_ref, vf_ref, win_ref, cw_ref, wout_ref, nw_ref, mod_ref, r_ref,
                    x3_ref, h4_ref, route_ref, cnt_ref):
    i = pl.program_id(0)
    tm = h_ref.shape[0]
    _, _, gt1, sh2, sc2, _ = _mod_rows(mod_ref, 0)
    z = _dot(h_ref[...], win_ref[...])
    b = z[:, :D_MODEL]
    v = z[:, D_MODEL:2 * D_MODEL] * z[:, 2 * D_MODEL:]
    has_prev = (i > 0).astype(F32)
    has_next = (i < pl.num_programs(0) - 1).astype(F32)
    prev_row = vl_ref[0, EDGE_ROWS - 1:EDGE_ROWS, :] * has_prev
    next_row = vf_ref[0, 0:1, :] * has_next
    rows = lax.broadcasted_iota(jnp.int32, (tm, 1), 0)
    v_dn = jnp.where(rows == 0, prev_row, pltpu.roll(v, 1, axis=0))
    v_up = jnp.where(rows == tm - 1, next_row, pltpu.roll(v, tm - 1, axis=0))
    conv = v_dn * cw_ref[0, 0:1, :] + v * cw_ref[0, 1:2, :] + v_up * cw_ref[0, 2:3, :]
    gated = (b * conv).astype(BF16)
    rhi, rlo = _split_bf16(r_ref[...])
    half = tm // 2
    ys = [_dot(gated[r0:r0 + half], wout_ref[...]) for r0 in (0, half)]
    logit_parts = []
    for r0, y in zip((0, half), ys):
        x3 = x_ref[r0:r0 + half, :] + gt1 * _rms(y, nw_ref[0, 1:2, :])
        x3_ref[r0:r0 + half, :] = x3
        h4 = _rms(x3, nw_ref[0, 2:3, :]) * (1.0 + sc2) + sh2
        h4_ref[r0:r0 + half, :] = _pack_pairs(h4)
        hhi, hlo = _split_bf16(h4)
        logit_parts.append(_dot(hhi, rhi) + (_dot(hlo, rhi) + _dot(hhi, rlo)))
    logits = jnp.concatenate(logit_parts, axis=0)
    lt = logits.T[:N_EXPERTS]
    e = jnp.exp(lt - jnp.max(lt, axis=0, keepdims=True))
    probs = e / jnp.sum(e, axis=0, keepdims=True)
    row = lax.broadcasted_iota(jnp.int32, lt.shape, 0).astype(F32)
    v1 = jnp.max(probs, axis=0, keepdims=True)
    i1 = jnp.min(jnp.where(probs == v1, row, float(N_EXPERTS)), axis=0, keepdims=True)
    rest = jnp.where(row == i1, -1.0, probs)
    v2 = jnp.max(rest, axis=0, keepdims=True)
    i2 = jnp.min(jnp.where(rest == v2, row, float(N_EXPERTS)), axis=0, keepdims=True)
    tot = v1 + v2
    sel = jnp.where(jnp.logical_or(row == i1, row == i2), 1.0, 0.0)
    earlier = lax.broadcasted_iota(jnp.int32, (tm, tm), 0) < lax.broadcasted_iota(jnp.int32, (tm, tm), 1)
    sel16 = jnp.concatenate([sel, jnp.zeros_like(sel)], axis=0).astype(BF16)
    rank = _dot(sel16, jnp.where(earlier, 1.0, 0.0).astype(BF16))[:N_EXPERTS]
    r1 = jnp.sum(jnp.where(row == i1, rank, 0.0), axis=0, keepdims=True)
    r2 = jnp.sum(jnp.where(row == i2, rank, 0.0), axis=0, keepdims=True)
    packed = jnp.concatenate([i1, i2, v1 / tot, v2 / tot, r1, r2, jnp.zeros((LANES - 6, tm), F32)], axis=0)
    route_ref[...] = packed.T
    cnt_ref[0] = jnp.broadcast_to(jnp.sum(sel, axis=1, keepdims=True), cnt_ref.shape[1:])


def _convmix(h3, x2, v_first, v_last, w_in_bf, conv_w, w_out_bf, nw, mod, router_pad, *, tm):
    n = h3.shape[0]
    nt = n // tm
    const = lambda i: (0, 0)
    return pl.pallas_call(
        _convmix_kernel,
        out_shape=(jax.ShapeDtypeStruct((n, D_MODEL), F32), jax.ShapeDtypeStruct((n, D_MODEL // 2), jnp.uint32),
                   jax.ShapeDtypeStruct((n, LANES), F32), jax.ShapeDtypeStruct((nt, 8, LANES), F32)),
        grid=(nt,),
        in_specs=[pl.BlockSpec((tm, D_MODEL), lambda i: (i, 0)),
                  pl.BlockSpec((tm, D_MODEL), lambda i: (i, 0)),
                  pl.BlockSpec((1, EDGE_ROWS, D_MODEL), lambda i: (jnp.maximum(i - 1, 0), 0, 0)),
                  pl.BlockSpec((1, EDGE_ROWS, D_MODEL), lambda i: (jnp.minimum(i + 1, nt - 1), 0, 0)),
                  pl.BlockSpec((D_MODEL, 3 * D_MODEL), const),
                  pl.BlockSpec((1, 3, D_MODEL), lambda i: (0, 0, 0)),
                  pl.BlockSpec((D_MODEL, D_MODEL), const),
                  pl.BlockSpec((1, 4, D_MODEL), lambda i: (0, 0, 0)),
                  pl.BlockSpec((1, 8, 6 * D_MODEL), lambda i: (0, 0, 0)),
                  pl.BlockSpec((D_MODEL, LANES), const)],
        out_specs=(pl.BlockSpec((tm, D_MODEL), lambda i: (i, 0)),
                   pl.BlockSpec((tm, D_MODEL // 2), lambda i: (i, 0)),
                   pl.BlockSpec((tm, LANES), lambda i: (i, 0)),
                   pl.BlockSpec((1, 8, LANES), lambda i: (i, 0, 0))),
        compiler_params=_cparams(("parallel",)),
        name="convmix",
    )(h3, x2, v_last, v_first, w_in_bf, conv_w, w_out_bf, nw, mod, router_pad)


ROW_TILE = 512


def _count_le(sorted_vals, x):
    return jnp.sum((sorted_vals[None, :] <= x[:, None]).astype(jnp.int32), axis=1)


def _route(route, cnt_tiles, n_row_tiles):
    n = route.shape[0]
    i32 = jnp.int32
    cnt_tile = cnt_tiles[:, :N_EXPERTS, 0].astype(i32)
    cum_end = jnp.cumsum(cnt_tile, axis=0)
    cnt = cum_end[-1]
    cnt_pad = (cnt + ROW_TILE - 1) // ROW_TILE * ROW_TILE
    grp_end = jnp.cumsum(cnt_pad)
    start = grp_end - cnt_pad
    offs = jnp.repeat(start[None, :] + cum_end - cnt_tile, ROW_TILE, axis=0)
    experts = jnp.clip(route[:, :2].astype(i32), 0, N_EXPERTS - 1)
    picked = experts[:, :, None] == jnp.arange(N_EXPERTS, dtype=i32)[None, None, :]
    pos2 = jnp.sum(jnp.where(picked, offs[:, None, :], 0), axis=-1) + route[:, 4:6].astype(i32)
    pos2 = jnp.clip(pos2, 0, n_row_tiles * ROW_TILE - 1)
    r = jnp.arange(n_row_tiles, dtype=i32)
    base = r * ROW_TILE
    tile_valid = base < grp_end[-1]
    n_valid = jnp.clip(grp_end[-1] // ROW_TILE, 1, n_row_tiles)
    tile_exp = jnp.minimum(_count_le(grp_end, base), N_EXPERTS - 1)
    tile_exp = jnp.where(tile_valid, tile_exp, tile_exp[n_valid - 1])
    tile_src = jnp.where(tile_valid, r, n_valid - 1)
    tile_first = jnp.logical_and(tile_valid, base == start[tile_exp])
    tiles = (tile_exp, tile_src, tile_valid.astype(i32), tile_first.astype(i32))
    return pos2.T.reshape(-1), tiles


SC_CHUNK = 128


def _sc_workers():
    info = pltpu.get_tpu_info().sparse_core
    return info.num_cores, info.num_cores * info.num_subcores


def _sc_scatter_rows(rows, idx, n_out):
    n_src, d = rows.shape
    n_cores, n_workers = _sc_workers()
    per_worker = idx.shape[0] // n_workers
    assert idx.shape[0] % (n_workers * SC_CHUNK) == 0 and n_src % per_worker == 0
    idx2d = idx.reshape(-1, SC_CHUNK)
    mesh = plsc.VectorSubcoreMesh(core_axis_name="c", subcore_axis_name="s")

    @functools.partial(pl.kernel, mesh=mesh, out_type=jax.ShapeDtypeStruct((n_out, d), rows.dtype),
                       scratch_types=[pltpu.VMEM((1, SC_CHUNK), jnp.int32), pltpu.VMEM((SC_CHUNK, d), rows.dtype)])
    def scatter(rows_hbm, idx_hbm, out_hbm, idx_v, rows_v):
        wid = lax.axis_index("s") * n_cores + lax.axis_index("c")

        @pl.loop(0, per_worker // SC_CHUNK)
        def _(j):
            a = wid * per_worker + j * SC_CHUNK
            pltpu.sync_copy(idx_hbm.at[pl.ds(a // SC_CHUNK, 1)], idx_v)
            pltpu.sync_copy(rows_hbm.at[pl.ds(lax.rem(a, n_src), SC_CHUNK)], rows_v)
            pltpu.sync_copy(rows_v, out_hbm.at[idx_v.at[0]])

    return scatter(rows, idx2d)


def _sc_gather_rows(table, idx):
    d = table.shape[1]
    n_cores, n_workers = _sc_workers()
    per_worker = idx.shape[0] // n_workers
    assert idx.shape[0] % (n_workers * SC_CHUNK) == 0
    idx2d = idx.reshape(-1, SC_CHUNK)
    mesh = plsc.VectorSubcoreMesh(core_axis_name="c", subcore_axis_name="s")

    @functools.partial(pl.kernel, mesh=mesh, out_type=jax.ShapeDtypeStruct((idx.shape[0], d), table.dtype),
                       scratch_types=[pltpu.VMEM((1, SC_CHUNK), jnp.int32), pltpu.VMEM((SC_CHUNK, d), table.dtype)])
    def gather(table_hbm, idx_hbm, out_hbm, idx_v, rows_v):
        wid = lax.axis_index("s") * n_cores + lax.axis_index("c")

        @pl.loop(0, per_worker // SC_CHUNK)
        def _(j):
            a = wid * per_worker + j * SC_CHUNK
            pltpu.sync_copy(idx_hbm.at[pl.ds(a // SC_CHUNK, 1)], idx_v)
            pltpu.sync_copy(table_hbm.at[idx_v.at[0]], rows_v)
            pltpu.sync_copy(rows_v, out_hbm.at[pl.ds(a, SC_CHUNK)])

    return gather(table, idx2d)


def _experts_kernel(t_exp, t_src, t_valid, t_first, xs_ref, wg_ref, wu_ref, wd_ref, ys_ref,
                    wg_c, wu_c, wd_c, x_sc, acc_ref):
    r = pl.program_id(0)
    c = pl.program_id(1)
    last = c == pl.num_programs(1) - 1

    @pl.when(t_first[r] != 0)
    def _():
        wg_c[c] = wg_ref[0, 0].astype(BF16)
        wu_c[c] = wu_ref[0, 0].astype(BF16)
        wd_c[c] = wd_ref[0, 0].astype(BF16)

    @pl.when(t_valid[r] != 0)
    def _():
        @pl.when(c == 0)
        def _():
            x_sc[...] = _unpack_pairs(xs_ref[...])
            acc_ref[...] = jnp.zeros(acc_ref.shape, F32)

        acc_ref[...] += _swiglu(x_sc[...], wg_c[c], wu_c[c], wd_c[c])

        @pl.when(last)
        def _():
            ys_ref[...] = _pack_pairs(acc_ref[...])

    @pl.when(jnp.logical_and(t_valid[r] == 0, last))
    def _():
        ys_ref[...] = jnp.zeros(ys_ref.shape, ys_ref.dtype)


def _experts(tiles, xs, wg, wu, wd, *, tf):
    n_rows, half = xs.shape
    n_ch = D_EXPERT // tf

    def w_in(r, c, e, s, v, f):
        return (0, e[r], 0, jnp.where(f[r] != 0, c, n_ch - 1))

    def w_dn(r, c, e, s, v, f):
        return (0, e[r], jnp.where(f[r] != 0, c, n_ch - 1), 0)

    grid_spec = pltpu.PrefetchScalarGridSpec(
        num_scalar_prefetch=4, grid=(n_rows // ROW_TILE, n_ch),
        in_specs=[pl.BlockSpec((ROW_TILE, half), lambda r, c, e, s, v, f: (s[r], 0)),
                  pl.BlockSpec((1, 1, D_MODEL, tf), w_in),
                  pl.BlockSpec((1, 1, D_MODEL, tf), w_in),
                  pl.BlockSpec((1, 1, tf, D_MODEL), w_dn)],
        out_specs=pl.BlockSpec((ROW_TILE, half), lambda r, c, e, s, v, f: (r, 0)),
        scratch_shapes=[pltpu.VMEM((n_ch, D_MODEL, tf), BF16), pltpu.VMEM((n_ch, D_MODEL, tf), BF16),
                        pltpu.VMEM((n_ch, tf, D_MODEL), BF16), pltpu.VMEM((ROW_TILE, D_MODEL), BF16),
                        pltpu.VMEM((ROW_TILE, D_MODEL), F32)])
    return pl.pallas_call(
        _experts_kernel, grid_spec=grid_spec,
        out_shape=jax.ShapeDtypeStruct((n_rows, half), jnp.uint32),
        compiler_params=_cparams(("arbitrary", "arbitrary")),
        name="moe_experts",
    )(*tiles, xs, wg, wu, wd)


def _combine_kernel(a_ref, b_ref, route_ref, x_ref, nw_ref, mod_ref, o_ref):
    route = route_ref[...]
    y = route[:, 2:3] * _unpack_pairs(a_ref[...]).astype(F32) + route[:, 3:4] * _unpack_pairs(b_ref[...]).astype(F32)
    gt2 = _mod_rows(mod_ref, 0)[5]
    o_ref[...] = x_ref[...] + gt2 * _rms(y, nw_ref[0, 3:4, :])


def _combine(rows, route, x3, nw, mod):
    n = x3.shape[0]
    nt = n // ROW_TILE
    half = rows.shape[1]
    return pl.pallas_call(
        _combine_kernel,
        out_shape=jax.ShapeDtypeStruct((n, D_MODEL), F32),
        grid=(nt,),
        in_specs=[pl.BlockSpec((ROW_TILE, half), lambda i: (i, 0)),
                  pl.BlockSpec((ROW_TILE, half), lambda i: (i + nt, 0)),
                  pl.BlockSpec((ROW_TILE, LANES), lambda i: (i, 0)),
                  pl.BlockSpec((ROW_TILE, D_MODEL), lambda i: (i, 0)),
                  pl.BlockSpec((1, 4, D_MODEL), lambda i: (0, 0, 0)),
                  pl.BlockSpec((1, 8, 6 * D_MODEL), lambda i: (0, 0, 0))],
        out_specs=pl.BlockSpec((ROW_TILE, D_MODEL), lambda i: (i, 0)),
        compiler_params=_cparams(("parallel",)),
        name="moe_combine",
    )(rows, rows, route, x3, nw, mod)


def _moe(h4p, route, cnt_tiles, x3, wg, wu, wd, nw, mod):
    n = h4p.shape[0]
    n_rows = 2 * n + N_EXPERTS * ROW_TILE
    pos, tiles = _route(route, cnt_tiles, n_rows // ROW_TILE)
    xs = _sc_scatter_rows(h4p, pos, n_rows)
    ys = _experts(tiles, xs, wg, wu, wd, tf=512)
    rows = _sc_gather_rows(ys, pos)
    return _combine(rows, route, x3, nw, mod)


def kernel(x, c, ctx, c_ctx, ada_w, ada_b, norm_w, e_w_in, e_q_gain, e_k_gain, e_w_out, e_ffn_gate,
           e_ffn_up, e_ffn_down, o_w_in, o_conv_w, o_w_out, o_router, o_exp_gate, o_exp_up, o_exp_down):
    assert x.shape[0] == 1 and x.shape[2] == D_MODEL and ada_w.shape[0] == 2
    n = x.shape[1]
    x2d = x[0]
    ctx2d = ctx[0]
    mod = _ada(c, c_ctx, ada_w, ada_b)
    mod0, mod1 = mod[0:1], mod[1:2]
    nw0, nw1 = norm_w[0:1], norm_w[1:2]

    w_in = e_w_in[0].astype(BF16)
    scale = HEAD_DIM ** -0.5 * np.log2(np.e)
    gain = jnp.concatenate([jnp.tile(e_q_gain[0], N_Q_HEADS) * scale,
                            jnp.tile(e_k_gain[0], N_KV_HEADS)])[None, :]
    score_bound = 1.02 * HEAD_DIM * scale * jnp.max(jnp.abs(e_q_gain[0])) * jnp.max(jnp.abs(e_k_gain[0]))

    def mixer_inputs(fast):
        qk_dtype = FP8 if fast else BF16
        qT, k, vT, f = _evenproj(x2d, nw0, mod0, w_in, gain, latent=True, tm=512, qk_dtype=qk_dtype)
        kc, vcT = _evenproj(ctx2d, nw0, mod0, w_in, gain, latent=False, tm=ctx2d.shape[0], qk_dtype=qk_dtype)
        return _attention(qT, k, vT, kc, vcT, bounded=fast, tq=256, tk=1024), f

    attnT, f = lax.cond(score_bound <= FAST_SCORE_BOUND, functools.partial(mixer_inputs, True),
                        functools.partial(mixer_inputs, False))
    four = _fourier(f)
    w_out = e_w_out[0].astype(BF16)
    x1, h2 = _outproj(attnT, four, x2d, w_out[:ATTN_WIDTH], w_out[ATTN_WIDTH:], nw0, mod0, tm=512)
    x2, h3 = _ffn(h2, x1, e_ffn_gate[0].astype(BF16), e_ffn_up[0].astype(BF16),
                  e_ffn_down[0].astype(BF16), nw0, mod0, nw1, mod1, tm=512)

    ow_in = o_w_in[0].astype(BF16)
    v_first, v_last = _convedge(h3, ow_in, tm=ROW_TILE)
    router_pad = jnp.pad(o_router[0], ((0, 0), (0, LANES - N_EXPERTS)))
    x3, h4p, route, cnt_tiles = _convmix(h3, x2, v_first, v_last, ow_in, o_conv_w, o_w_out[0].astype(BF16),
                                         nw1, mod1, router_pad, tm=ROW_TILE)
    out = _moe(h4p, route, cnt_tiles, x3, o_exp_gate, o_exp_up, o_exp_down, nw1, mod1)
    return out[None]
```

```python
import functools

import numpy as np
import jax
import jax.numpy as jnp
from jax import lax
from jax.experimental import pallas as pl
from jax.experimental.pallas import tpu as pltpu
from jax.experimental.pallas import tpu_sc as plsc

D_MODEL = 1024
GRID_W = 64
HEAD_DIM = 64
N_Q_HEADS = 12
N_KV_HEADS = 4
Q_PER_KV = N_Q_HEADS // N_KV_HEADS
ATTN_WIDTH = N_Q_HEADS * HEAD_DIM
KV_WIDTH = N_KV_HEADS * HEAD_DIM
QK_WIDTH = ATTN_WIDTH + KV_WIDTH
N_FOURIER_GROUPS = 4
FOURIER_GROUP_DIM = 64
FOURIER_WIDTH = N_FOURIER_GROUPS * FOURIER_GROUP_DIM
EVEN_IN_WIDTH = ATTN_WIDTH + 2 * KV_WIDTH + FOURIER_WIDTH
D_FF = 2816
N_EXPERTS = 8
D_EXPERT = 3584
ONES_ROWS = 16
V_ROWS = HEAD_DIM + ONES_ROWS
FAST_SCORE_BOUND = 24.0
ROPE_THETA = 10000.0
ROPE_HALF = HEAD_DIM // 4
NORM_EPS = 1e-6

LANES = 128
VMEM_LIMIT = 56 * 1024 * 1024

TOKEN_TILE = 512
ATTN_Q_TILE = 256
ATTN_KEY_TILE = 1024
EXPERT_CHUNK = 512

BF16 = jnp.bfloat16
FP8 = jnp.float8_e4m3fn
F32 = jnp.float32


def _cparams(semantics, vmem=VMEM_LIMIT):
    return pltpu.CompilerParams(dimension_semantics=semantics, vmem_limit_bytes=vmem)


def _dot(a, b):
    return jnp.dot(a, b, preferred_element_type=F32)


def _split_bf16(x):
    hi = x.astype(BF16)
    lo = (x - hi.astype(F32)).astype(BF16)
    return hi, lo


def _rms(x, g):
    return x * lax.rsqrt(jnp.mean(x * x, axis=-1, keepdims=True) + NORM_EPS) * g


def _mod_rows(mod_ref, row):
    return [mod_ref[0, row:row + 1, i * D_MODEL:(i + 1) * D_MODEL] for i in range(6)]


def _pack_pairs(x):
    k = x.shape[1] // 2
    bits = lax.bitcast_convert_type(x.astype(BF16).astype(F32), jnp.uint32)
    return (bits[:, :k] >> 16) | (bits[:, k:] & jnp.uint32(0xFFFF0000))


def _unpack_pairs(w):
    lo = lax.bitcast_convert_type(w << 16, F32).astype(BF16)
    hi = lax.bitcast_convert_type(w & jnp.uint32(0xFFFF0000), F32).astype(BF16)
    return jnp.concatenate([lo, hi], axis=1)


@functools.lru_cache(maxsize=None)
def _rope_tables(n_tokens):
    t = np.arange(n_tokens)
    row = (t // GRID_W).astype(np.float64)
    col = (t % GRID_W).astype(np.float64)
    inv = ROPE_THETA ** (-np.arange(ROPE_HALF, dtype=np.float64) / ROPE_HALF)
    ar, ac = row[:, None] * inv, col[:, None] * inv
    cos = np.concatenate([np.cos(ar), np.cos(ar), np.cos(ac), np.cos(ac)], axis=-1)
    sin = np.concatenate([-np.sin(ar), np.sin(ar), -np.sin(ac), np.sin(ac)], axis=-1)
    reps = LANES // HEAD_DIM
    return (np.tile(cos, (1, reps)).astype(np.float32), np.tile(sin, (1, reps)).astype(np.float32))


@functools.lru_cache(maxsize=None)
def _head_matrices():
    head = np.arange(QK_WIDTH) // HEAD_DIM
    red = (head[:, None] == np.arange(LANES)[None, :]).astype(np.float32) / HEAD_DIM
    exp = (np.arange(LANES)[:, None] == head[None, :]).astype(np.float32)
    return red, exp


@functools.lru_cache(maxsize=None)
def _fourier_tables(n_tokens, kb):
    n2 = LANES
    n1 = n_tokens // n2
    c = np.arange(FOURIER_GROUP_DIM)
    ang = 2 * np.pi * np.outer(c, c) / FOURIER_GROUP_DIM
    eye = np.eye(N_FOURIER_GROUPS)
    cs = np.concatenate([np.kron(eye, np.cos(ang)), np.kron(eye, np.sin(ang))], axis=1)
    k1 = np.arange(n1)
    th = 2 * np.pi * np.outer(k1, k1) / n1
    cr, ci = np.cos(th), -np.sin(th)
    base = np.block([[cr, ci], [ci, -cr]])
    psi = 2 * np.pi * np.outer(k1, np.arange(n2)) / n_tokens
    twr = np.cos(psi).reshape(n1, n2 // kb, kb).transpose(1, 0, 2)
    twi = (-np.sin(psi)).reshape(n1, n2 // kb, kb).transpose(1, 0, 2)
    k2 = np.arange(n2)
    ph = 2 * np.pi * np.outer(k2, k2) / n2
    fr, fi = np.cos(ph), -np.sin(ph)
    scale = 1.0 / np.sqrt(n_tokens * FOURIER_GROUP_DIM)
    m3 = np.stack([fr, -fi], axis=-1) * scale
    wb = np.einsum('knr,uv->kunrv', m3, np.eye(kb)).reshape(n2 * kb, n2 * 2 * kb)
    f32 = np.float32
    return cs.astype(f32), base.astype(f32), twr.astype(f32), twi.astype(f32), wb.astype(f32)


def _ada_kernel(cb_ref, w_ref, b_ref, o_ref):
    tn = o_ref.shape[-1]
    o_ref[...] = jnp.zeros(o_ref.shape, F32)
    for r in range(2):
        cb = cb_ref[r]
        s = cb * jax.nn.sigmoid(cb)
        for j in range(tn // LANES):
            sl = slice(j * LANES, (j + 1) * LANES)
            col = jnp.sum(s * w_ref[0, :, sl], axis=0, keepdims=True)
            o_ref[0, r:r + 1, sl] = col + b_ref[0, :, sl]


def _ada(c, c_ctx, ada_w, ada_b):
    depth = ada_w.shape[0]
    n = ada_w.shape[-1]
    tn = 1536
    cb = jnp.stack([jnp.broadcast_to(c[0][:, None], (D_MODEL, LANES)),
                    jnp.broadcast_to(c_ctx[:, None], (D_MODEL, LANES))])
    return pl.pallas_call(
        _ada_kernel,
        out_shape=jax.ShapeDtypeStruct((depth, 8, n), F32),
        grid=(depth, n // tn),
        in_specs=[pl.BlockSpec((2, D_MODEL, LANES), lambda i, j: (0, 0, 0)),
                  pl.BlockSpec((1, D_MODEL, tn), lambda i, j: (i, 0, j)),
                  pl.BlockSpec((1, 1, tn), lambda i, j: (i, 0, j))],
        out_specs=pl.BlockSpec((1, 8, tn), lambda i, j: (i, 0, j)),
        compiler_params=_cparams(("parallel", "parallel")),
        name="ada",
    )(cb, ada_w, ada_b[:, None, :])


def _evenproj_kernel(*refs, row, latent, qk_dtype):
    if latent:
        (x_ref, nw_ref, mod_ref, w_ref, gain_ref, red_ref, exp_ref, cos_ref, sin_ref,
         qT_ref, k_ref, vT_ref, f_ref) = refs
    else:
        x_ref, nw_ref, mod_ref, w_ref, gain_ref, red_ref, exp_ref, k_ref, vT_ref = refs
    sh, sc = _mod_rows(mod_ref, row)[:2]
    red = red_ref[...].astype(BF16)
    expm = exp_ref[...].astype(BF16)
    tm = x_ref.shape[0]
    n_parts = 2 if latent else 1
    m = tm // n_parts

    def matmuls(r0):
        h = (_rms(x_ref[r0:r0 + m, :], nw_ref[0, 0:1, :]) * (1.0 + sc) + sh).astype(BF16)
        z = _dot(h, w_ref[...])
        zqk = z[:, :QK_WIDTH]
        hi, lo = _split_bf16(zqk * zqk)
        ms = _dot(hi, red) + _dot(lo, red)
        rhi, rlo = _split_bf16(lax.rsqrt(ms + NORM_EPS))
        return z, zqk * (_dot(rhi, expm) + _dot(rlo, expm)) * gain_ref[...]

    def finish(r0, z, yn):
        if latent:
            lane = lax.broadcasted_iota(jnp.int32, (1, LANES), 1)
            first_half = (lane // ROPE_HALF) % 2 == 0
            cos, sin = cos_ref[r0:r0 + m, :], sin_ref[r0:r0 + m, :]
            chunks = []
            for c in range(QK_WIDTH // LANES):
                yc = yn[:, c * LANES:(c + 1) * LANES]
                partner = jnp.where(first_half, pltpu.roll(yc, LANES - ROPE_HALF, axis=1),
                                    pltpu.roll(yc, ROPE_HALF, axis=1))
                chunks.append(yc * cos + partner * sin)
            yn = jnp.concatenate(chunks, axis=1)
            qT_ref[:, r0:r0 + m] = yn[:, :ATTN_WIDTH].T.astype(qk_dtype)
            f_ref[r0:r0 + m, :] = z[:, QK_WIDTH + KV_WIDTH:].astype(BF16)
        for g in range(N_KV_HEADS):
            k_ref[g, r0:r0 + m, :] = yn[:, ATTN_WIDTH + g * HEAD_DIM:ATTN_WIDTH + (g + 1) * HEAD_DIM].astype(qk_dtype)
        vT = z[:, QK_WIDTH:QK_WIDTH + KV_WIDTH].T.astype(BF16)
        ones = jnp.ones((ONES_ROWS, m), BF16)
        for g in range(N_KV_HEADS):
            vT_ref[g * V_ROWS:g * V_ROWS + HEAD_DIM, r0:r0 + m] = vT[g * HEAD_DIM:(g + 1) * HEAD_DIM]
            vT_ref[g * V_ROWS + HEAD_DIM:(g + 1) * V_ROWS, r0:r0 + m] = ones

    parts = [(p * m,) + matmuls(p * m) for p in range(n_parts)]
    for r0, z, yn in parts:
        finish(r0, z, yn)


def _evenproj(x2d, nw, mod, w_bf, gain, *, latent, tm, qk_dtype):
    n = x2d.shape[0]
    red, expm = _head_matrices()
    const = lambda i: (0, 0)
    in_specs = [pl.BlockSpec((tm, D_MODEL), lambda i: (i, 0)),
                pl.BlockSpec((1, 4, D_MODEL), lambda i: (0, 0, 0)),
                pl.BlockSpec((1, 8, 6 * D_MODEL), lambda i: (0, 0, 0)),
                pl.BlockSpec((D_MODEL, EVEN_IN_WIDTH), const),
                pl.BlockSpec((1, QK_WIDTH), const),
                pl.BlockSpec((QK_WIDTH, LANES), const),
                pl.BlockSpec((LANES, QK_WIDTH), const)]
    args = [x2d, nw, mod, w_bf, gain, jnp.asarray(red), jnp.asarray(expm)]
    k_shape = jax.ShapeDtypeStruct((N_KV_HEADS, n, HEAD_DIM), qk_dtype)
    vT_shape = jax.ShapeDtypeStruct((N_KV_HEADS * V_ROWS, n), BF16)
    k_spec = pl.BlockSpec((N_KV_HEADS, tm, HEAD_DIM), lambda i: (0, i, 0))
    vT_spec = pl.BlockSpec((N_KV_HEADS * V_ROWS, tm), lambda i: (0, i))
    if latent:
        cos, sin = _rope_tables(n)
        in_specs += [pl.BlockSpec((tm, LANES), lambda i: (i, 0))] * 2
        args += [jnp.asarray(cos), jnp.asarray(sin)]
        out_shape = (jax.ShapeDtypeStruct((ATTN_WIDTH, n), qk_dtype), k_shape, vT_shape,
                     jax.ShapeDtypeStruct((n, FOURIER_WIDTH), BF16))
        out_specs = (pl.BlockSpec((ATTN_WIDTH, tm), lambda i: (0, i)), k_spec, vT_spec,
                     pl.BlockSpec((tm, FOURIER_WIDTH), lambda i: (i, 0)))
    else:
        out_shape = (k_shape, vT_shape)
        out_specs = (k_spec, vT_spec)
    return pl.pallas_call(
        functools.partial(_evenproj_kernel, row=0 if latent else 1, latent=latent, qk_dtype=qk_dtype),
        out_shape=out_shape, grid=(n // tm,), in_specs=in_specs, out_specs=out_specs,
        compiler_params=_cparams(("parallel",)),
        name="evenproj_lat" if latent else "evenproj_ctx",
    )(*args)


def _visit_all(visit, k_ref, vT_ref, kc_ref, vcT_ref, tk):
    def body(c, carry):
        off = pl.multiple_of(c * tk, tk)
        visit(k_ref[0, pl.ds(off, tk), :], vT_ref[:, pl.ds(off, tk)])
        return carry

    lax.fori_loop(0, k_ref.shape[1] // tk, body, 0)
    visit(kc_ref[0], vcT_ref[...])


def _attn_bounded_kernel(qT_ref, k_ref, vT_ref, kc_ref, vcT_ref, o_ref, acc_sc, *, tk):
    n_tiles = k_ref.shape[1] // tk
    acc_sc[...] = jnp.zeros(acc_sc.shape, F32)
    tiles = [(k_ref.at[0, c * tk:(c + 1) * tk, :], vT_ref.at[:, c * tk:(c + 1) * tk]) for c in range(n_tiles)]
    tiles.append((kc_ref.at[0], vcT_ref))
    steps = [(kt, vt, j) for kt, vt in tiles for j in range(Q_PER_KV)]

    def scores(step):
        kt, _, j = step
        return _dot(kt[...], qT_ref[j * HEAD_DIM:(j + 1) * HEAD_DIM, :])

    s = scores(steps[0])
    for n, (_, vt, j) in enumerate(steps):
        s_next = scores(steps[n + 1]) if n + 1 < len(steps) else None
        acc_sc[j] += _dot(vt[...], jnp.exp2(s).astype(BF16))
        s = s_next
    for j in range(Q_PER_KV):
        acc = acc_sc[j]
        o_ref[j * HEAD_DIM:(j + 1) * HEAD_DIM, :] = (acc[:HEAD_DIM] / acc[HEAD_DIM:HEAD_DIM + 1]).astype(BF16)


def _attn_online_kernel(qT_ref, k_ref, vT_ref, kc_ref, vcT_ref, o_ref, m_sc, acc_sc, *, tk):
    m_sc[...] = jnp.full(m_sc.shape, -jnp.inf, F32)
    acc_sc[...] = jnp.zeros(acc_sc.shape, F32)

    def visit(kt, vt):
        for j in range(Q_PER_KV):
            s = _dot(kt, qT_ref[j * HEAD_DIM:(j + 1) * HEAD_DIM, :])
            m_old = m_sc[j]
            m_new = jnp.maximum(m_old, jnp.max(s, axis=0, keepdims=True))
            p = jnp.exp2(s - m_new).astype(BF16)
            acc_sc[j] = jnp.exp2(m_old - m_new) * acc_sc[j] + _dot(vt, p)
            m_sc[j] = m_new

    _visit_all(visit, k_ref, vT_ref, kc_ref, vcT_ref, tk)
    for j in range(Q_PER_KV):
        acc = acc_sc[j]
        o_ref[j * HEAD_DIM:(j + 1) * HEAD_DIM, :] = (acc[:HEAD_DIM] / acc[HEAD_DIM:HEAD_DIM + 1]).astype(BF16)


def _attention(qT, k, vT, kc, vcT, *, bounded, tq, tk):
    n = qT.shape[1]
    n_ctx = kc.shape[1]
    gw = Q_PER_KV * HEAD_DIM
    common = dict(
        out_shape=jax.ShapeDtypeStruct((ATTN_WIDTH, n), BF16),
        grid=(N_KV_HEADS, n // tq),
        in_specs=[pl.BlockSpec((gw, tq), lambda g, i: (g, i)),
                  pl.BlockSpec((1, n, HEAD_DIM), lambda g, i: (g, 0, 0)),
                  pl.BlockSpec((V_ROWS, n), lambda g, i: (g, 0)),
                  pl.BlockSpec((1, n_ctx, HEAD_DIM), lambda g, i: (g, 0, 0)),
                  pl.BlockSpec((V_ROWS, n_ctx), lambda g, i: (g, 0))],
        out_specs=pl.BlockSpec((gw, tq), lambda g, i: (g, i)),
        compiler_params=_cparams(("parallel", "parallel")),
    )
    acc = pltpu.VMEM((Q_PER_KV, V_ROWS, tq), F32)
    if bounded:
        call = pl.pallas_call(functools.partial(_attn_bounded_kernel, tk=tk),
                              scratch_shapes=[acc], name="attn_bounded", **common)
    else:
        call = pl.pallas_call(functools.partial(_attn_online_kernel, tk=tk),
                              scratch_shapes=[pltpu.VMEM((Q_PER_KV, 1, tq), F32), acc],
                              name="attn_online", **common)
    return call(qT, k, vT, kc, vcT)


def _four_a_kernel(f_ref, cs_ref, base_ref, twr_ref, twi_ref, y_ref, *, nb):
    n1 = f_ref.shape[0]
    cs = cs_ref[...].astype(BF16)
    base = base_ref[...].astype(BF16)
    for u in range(nb):
        xb = f_ref[:, u * FOURIER_WIDTH:(u + 1) * FOURIER_WIDTH]
        ab = _dot(xb, cs)
        stacked = jnp.concatenate([ab[:, :FOURIER_WIDTH], ab[:, FOURIER_WIDTH:]], axis=0)
        p = _dot(base, stacked.astype(BF16))
        pr, pi = p[:n1], p[n1:]
        tr = twr_ref[0, :, u:u + 1]
        ti = twi_ref[0, :, u:u + 1]
        y_ref[u, 0] = tr * pr - ti * pi
        y_ref[u, 1] = tr * pi + ti * pr


def _four_b_kernel(y_ref, wb_ref, o_ref):
    n2, _, kb, w = y_ref.shape
    y = y_ref[...].reshape(n2 * 2 * kb, w).astype(BF16)
    o_ref[...] = _dot(wb_ref[...].astype(BF16), y).reshape(n2, kb, w)


def _fourier(f):
    n = f.shape[0]
    n2 = LANES
    n1 = n // n2
    nb = kb = 8
    cs, base, twr, twi, wb = (jnp.asarray(t) for t in _fourier_tables(n, kb))
    f2d = f.reshape(n1, n2 * FOURIER_WIDTH)
    y = pl.pallas_call(
        functools.partial(_four_a_kernel, nb=nb),
        out_shape=jax.ShapeDtypeStruct((n2, 2, n1, FOURIER_WIDTH), F32),
        grid=(n2 // nb,),
        in_specs=[pl.BlockSpec((n1, nb * FOURIER_WIDTH), lambda s: (0, s)),
                  pl.BlockSpec(cs.shape, lambda s: (0, 0)),
                  pl.BlockSpec(base.shape, lambda s: (0, 0)),
                  pl.BlockSpec((1, n1, nb), lambda s: (s, 0, 0)),
                  pl.BlockSpec((1, n1, nb), lambda s: (s, 0, 0))],
        out_specs=pl.BlockSpec((nb, 2, n1, FOURIER_WIDTH), lambda s: (s, 0, 0, 0)),
        compiler_params=_cparams(("parallel",)),
        name="four_a",
    )(f2d, cs, base, twr, twi)
    out = pl.pallas_call(
        _four_b_kernel,
        out_shape=jax.ShapeDtypeStruct((n2, n1, FOURIER_WIDTH), F32),
        grid=(n1 // kb,),
        in_specs=[pl.BlockSpec((n2, 2, kb, FOURIER_WIDTH), lambda s: (0, 0, s, 0)),
                  pl.BlockSpec(wb.shape, lambda s: (0, 0))],
        out_specs=pl.BlockSpec((n2, kb, FOURIER_WIDTH), lambda s: (0, s, 0)),
        compiler_params=_cparams(("parallel",)),
        name="four_b",
    )(y, wb)
    return out.reshape(n, FOURIER_WIDTH)


def _outproj_kernel(aT_ref, four_ref, x_ref, wa_ref, wf_ref, nw_ref, mod_ref, x1_ref, h_ref):
    _, _, gt1, sh2, sc2, _ = _mod_rows(mod_ref, 0)
    half = x_ref.shape[0] // 2
    starts = (0, half)
    ys = [lax.dot_general(aT_ref[:, r0:r0 + half], wa_ref[...], (((0,), (0,)), ((), ())),
                          preferred_element_type=F32)
          + _dot(four_ref[r0:r0 + half, :].astype(BF16), wf_ref[...]) for r0 in starts]
    for r0, y in zip(starts, ys):
        x1 = x_ref[r0:r0 + half, :] + gt1 * _rms(y, nw_ref[0, 1:2, :])
        x1_ref[r0:r0 + half, :] = x1
        h_ref[r0:r0 + half, :] = (_rms(x1, nw_ref[0, 2:3, :]) * (1.0 + sc2) + sh2).astype(BF16)


def _outproj(attnT, four, x2d, wa, wf, nw, mod, *, tm):
    n = x2d.shape[0]
    const = lambda i: (0, 0)
    return pl.pallas_call(
        _outproj_kernel,
        out_shape=(jax.ShapeDtypeStruct((n, D_MODEL), F32), jax.ShapeDtypeStruct((n, D_MODEL), BF16)),
        grid=(n // tm,),
        in_specs=[pl.BlockSpec((ATTN_WIDTH, tm), lambda i: (0, i)),
                  pl.BlockSpec((tm, FOURIER_WIDTH), lambda i: (i, 0)),
                  pl.BlockSpec((tm, D_MODEL), lambda i: (i, 0)),
                  pl.BlockSpec((ATTN_WIDTH, D_MODEL), const),
                  pl.BlockSpec((FOURIER_WIDTH, D_MODEL), const),
                  pl.BlockSpec((1, 4, D_MODEL), lambda i: (0, 0, 0)),
                  pl.BlockSpec((1, 8, 6 * D_MODEL), lambda i: (0, 0, 0))],
        out_specs=(pl.BlockSpec((tm, D_MODEL), lambda i: (i, 0)),
                   pl.BlockSpec((tm, D_MODEL), lambda i: (i, 0))),
        compiler_params=_cparams(("parallel",)),
        name="outproj",
    )(attnT, four, x2d, wa, wf, nw, mod)


def _swiglu(x, wg, wu, wd):
    g = _dot(x, wg)
    return _dot((g * jax.nn.sigmoid(g) * _dot(x, wu)).astype(BF16), wd)


def _ffn_kernel(h_ref, x_ref, wg_ref, wu_ref, wd_ref, nw_ref, mod_ref, nw1_ref, mod1_ref, x2_ref, h3_ref):
    gt2 = _mod_rows(mod_ref, 0)[5]
    sh, sc = _mod_rows(mod1_ref, 0)[:2]
    half = x_ref.shape[0] // 2
    starts = (0, half)
    ys = [_swiglu(h_ref[r0:r0 + half, :], wg_ref[...], wu_ref[...], wd_ref[...]) for r0 in starts]
    for r0, y in zip(starts, ys):
        x2 = x_ref[r0:r0 + half, :] + gt2 * _rms(y, nw_ref[0, 3:4, :])
        x2_ref[r0:r0 + half, :] = x2
        h3_ref[r0:r0 + half, :] = (_rms(x2, nw1_ref[0, 0:1, :]) * (1.0 + sc) + sh).astype(BF16)


def _ffn(h, x1, wg, wu, wd, nw, mod, nw1, mod1, *, tm):
    n = h.shape[0]
    nwspec = pl.BlockSpec((1, 4, D_MODEL), lambda i: (0, 0, 0))
    modspec = pl.BlockSpec((1, 8, 6 * D_MODEL), lambda i: (0, 0, 0))
    resident = dict(index_map=lambda i: (0, 0), pipeline_mode=pl.Buffered(1))
    return pl.pallas_call(
        _ffn_kernel,
        out_shape=(jax.ShapeDtypeStruct((n, D_MODEL), F32), jax.ShapeDtypeStruct((n, D_MODEL), BF16)),
        grid=(n // tm,),
        in_specs=[pl.BlockSpec((tm, D_MODEL), lambda i: (i, 0)),
                  pl.BlockSpec((tm, D_MODEL), lambda i: (i, 0)),
                  pl.BlockSpec((D_MODEL, D_FF), **resident),
                  pl.BlockSpec((D_MODEL, D_FF), **resident),
                  pl.BlockSpec((D_FF, D_MODEL), **resident),
                  nwspec, modspec, nwspec, modspec],
        out_specs=(pl.BlockSpec((tm, D_MODEL), lambda i: (i, 0)),
                   pl.BlockSpec((tm, D_MODEL), lambda i: (i, 0))),
        compiler_params=_cparams(("parallel",)),
        name="ffn",
    )(h, x1, wg, wu, wd, nw, mod, nw1, mod1)


EDGE_ROWS = 16


def _convedge_kernel(hf_ref, hl_ref, wc_ref, wu_ref, vf_ref, vl_ref):
    nt = hf_ref.shape[0]
    for h_ref, v_ref in ((hf_ref, vf_ref), (hl_ref, vl_ref)):
        h = h_ref[...].reshape(nt * EDGE_ROWS, D_MODEL)
        v = _dot(h, wc_ref[...]) * _dot(h, wu_ref[...])
        v_ref[...] = v.reshape(nt, EDGE_ROWS, D_MODEL)


def _convedge(h3, w_in_bf, *, tm):
    n = h3.shape[0]
    nt = n // tm
    h3t = h3.reshape(nt, tm, D_MODEL)
    last = tm // EDGE_ROWS - 1
    shape = jax.ShapeDtypeStruct((nt, EDGE_ROWS, D_MODEL), F32)
    return pl.pallas_call(
        _convedge_kernel,
        out_shape=(shape, shape),
        grid=(1,),
        in_specs=[pl.BlockSpec((nt, EDGE_ROWS, D_MODEL), lambda i: (0, 0, 0)),
                  pl.BlockSpec((nt, EDGE_ROWS, D_MODEL), lambda i: (0, last, 0)),
                  pl.BlockSpec((D_MODEL, D_MODEL), lambda i: (0, 1)),
                  pl.BlockSpec((D_MODEL, D_MODEL), lambda i: (0, 2))],
        out_specs=(pl.BlockSpec((nt, EDGE_ROWS, D_MODEL), lambda i: (0, 0, 0)),
                   pl.BlockSpec((nt, EDGE_ROWS, D_MODEL), lambda i: (0, 0, 0))),
        compiler_params=_cparams(("arbitrary",)),
        name="convedge",
    )(h3t, h3t, w_in_bf, w_in_bf)


def _convmix_kernel(h_ref, x_ref, vl_ref, vf_ref, win_ref, cw_ref, wout_ref, nw_ref, mod_ref, r_ref,
                    x3_ref, h4_ref, route_ref, cnt_ref):
    i = pl.program_id(0)
    tm = h_ref.shape[0]
    _, _, gt1, sh2, sc2, _ = _mod_rows(mod_ref, 0)
    z = _dot(h_ref[...], win_ref[...])
    b = z[:, :D_MODEL]
    v = z[:, D_MODEL:2 * D_MODEL] * z[:, 2 * D_MODEL:]
    has_prev = (i > 0).astype(F32)
    has_next = (i < pl.num_programs(0) - 1).astype(F32)
    prev_row = vl_ref[0, EDGE_ROWS - 1:EDGE_ROWS, :] * has_prev
    next_row = vf_ref[0, 0:1, :] * has_next
    rows = lax.broadcasted_iota(jnp.int32, (tm, 1), 0)
    v_dn = jnp.where(rows == 0, prev_row, pltpu.roll(v, 1, axis=0))
    v_up = jnp.where(rows == tm - 1, next_row, pltpu.roll(v, tm - 1, axis=0))
    conv = v_dn * cw_ref[0, 0:1, :] + v * cw_ref[0, 1:2, :] + v_up * cw_ref[0, 2:3, :]
    gated = (b * conv).astype(BF16)
    rhi, rlo = _split_bf16(r_ref[...])
    half = tm // 2
    ys = [_dot(gated[r0:r0 + half], wout_ref[...]) for r0 in (0, half)]
    logit_parts = []
    for r0, y in zip((0, half), ys):
        x3 = x_ref[r0:r0 + half, :] + gt1 * _rms(y, nw_ref[0, 1:2, :])
        x3_ref[r0:r0 + half, :] = x3
        h4 = _rms(x3, nw_ref[0, 2:3, :]) * (1.0 + sc2) + sh2
        h4_ref[r0:r0 + half, :] = _pack_pairs(h4)
        hhi, hlo = _split_bf16(h4)
        logit_parts.append(_dot(hhi, rhi) + (_dot(hlo, rhi) + _dot(hhi, rlo)))
    logits = jnp.concatenate(logit_parts, axis=0)
    lt = logits.T[:N_EXPERTS]
    e = jnp.exp(lt - jnp.max(lt, axis=0, keepdims=True))
    probs = e / jnp.sum(e, axis=0, keepdims=True)
    row = lax.broadcasted_iota(jnp.int32, lt.shape, 0).astype(F32)
    v1 = jnp.max(probs, axis=0, keepdims=True)
    i1 = jnp.min(jnp.where(probs == v1, row, float(N_EXPERTS)), axis=0, keepdims=True)
    rest = jnp.where(row == i1, -1.0, probs)
    v2 = jnp.max(rest, axis=0, keepdims=True)
    i2 = jnp.min(jnp.where(rest == v2, row, float(N_EXPERTS)), axis=0, keepdims=True)
    tot = v1 + v2
    sel = jnp.where(jnp.logical_or(row == i1, row == i2), 1.0, 0.0)
    earlier = lax.broadcasted_iota(jnp.int32, (tm, tm), 0) < lax.broadcasted_iota(jnp.int32, (tm, tm), 1)
    sel16 = jnp.concatenate([sel, jnp.zeros_like(sel)], axis=0).astype(BF16)
    rank = _dot(sel16, jnp.where(earlier, 1.0, 0.0).astype(BF16))[:N_EXPERTS]
    r1 = jnp.sum(jnp.where(row == i1, rank, 0.0), axis=0, keepdims=True)
    r2 = jnp.sum(jnp.where(row == i2, rank, 0.0), axis=0, keepdims=True)
    packed = jnp.concatenate([i1, i2, v1 / tot, v2 / tot, r1, r2, jnp.zeros((LANES - 6, tm), F32)], axis=0)
    route_ref[...] = packed.T
    cnt_ref[0] = jnp.broadcast_to(jnp.sum(sel, axis=1, keepdims=True), cnt_ref.shape[1:])


def _convmix(h3, x2, v_first, v_last, w_in_bf, conv_w, w_out_bf, nw, mod, router_pad, *, tm):
    n = h3.shape[0]
    nt = n // tm
    const = lambda i: (0, 0)
    return pl.pallas_call(
        _convmix_kernel,
        out_shape=(jax.ShapeDtypeStruct((n, D_MODEL), F32), jax.ShapeDtypeStruct((n, D_MODEL // 2), jnp.uint32),
                   jax.ShapeDtypeStruct((n, LANES), F32), jax.ShapeDtypeStruct((nt, 8, LANES), F32)),
        grid=(nt,),
        in_specs=[pl.BlockSpec((tm, D_MODEL), lambda i: (i, 0)),
                  pl.BlockSpec((tm, D_MODEL), lambda i: (i, 0)),
                  pl.BlockSpec((1, EDGE_ROWS, D_MODEL), lambda i: (jnp.maximum(i - 1, 0), 0, 0)),
                  pl.BlockSpec((1, EDGE_ROWS, D_MODEL), lambda i: (jnp.minimum(i + 1, nt - 1), 0, 0)),
                  pl.BlockSpec((D_MODEL, 3 * D_MODEL), const),
                  pl.BlockSpec((1, 3, D_MODEL), lambda i: (0, 0, 0)),
                  pl.BlockSpec((D_MODEL, D_MODEL), const),
                  pl.BlockSpec((1, 4, D_MODEL), lambda i: (0, 0, 0)),
                  pl.BlockSpec((1, 8, 6 * D_MODEL), lambda i: (0, 0, 0)),
                  pl.BlockSpec((D_MODEL, LANES), const)],
        out_specs=(pl.BlockSpec((tm, D_MODEL), lambda i: (i, 0)),
                   pl.BlockSpec((tm, D_MODEL // 2), lambda i: (i, 0)),
                   pl.BlockSpec((tm, LANES), lambda i: (i, 0)),
                   pl.BlockSpec((1, 8, LANES), lambda i: (i, 0, 0))),
        compiler_params=_cparams(("parallel",)),
        name="convmix",
    )(h3, x2, v_last, v_first, w_in_bf, conv_w, w_out_bf, nw, mod, router_pad)


ROW_TILE = TOKEN_TILE


def _count_le(sorted_vals, x):
    return jnp.sum((sorted_vals[None, :] <= x[:, None]).astype(jnp.int32), axis=1)


def _route(route, cnt_tiles, n_row_tiles):
    n = route.shape[0]
    i32 = jnp.int32
    cnt_tile = cnt_tiles[:, :N_EXPERTS, 0].astype(i32)
    cum_end = jnp.cumsum(cnt_tile, axis=0)
    cnt = cum_end[-1]
    cnt_pad = (cnt + ROW_TILE - 1) // ROW_TILE * ROW_TILE
    grp_end = jnp.cumsum(cnt_pad)
    start = grp_end - cnt_pad
    offs = jnp.repeat(start[None, :] + cum_end - cnt_tile, ROW_TILE, axis=0)
    experts = jnp.clip(route[:, :2].astype(i32), 0, N_EXPERTS - 1)
    picked = experts[:, :, None] == jnp.arange(N_EXPERTS, dtype=i32)[None, None, :]
    pos2 = jnp.sum(jnp.where(picked, offs[:, None, :], 0), axis=-1) + route[:, 4:6].astype(i32)
    pos2 = jnp.clip(pos2, 0, n_row_tiles * ROW_TILE - 1)
    r = jnp.arange(n_row_tiles, dtype=i32)
    base = r * ROW_TILE
    tile_valid = base < grp_end[-1]
    n_valid = jnp.clip(grp_end[-1] // ROW_TILE, 1, n_row_tiles)
    tile_exp = jnp.minimum(_count_le(grp_end, base), N_EXPERTS - 1)
    tile_exp = jnp.where(tile_valid, tile_exp, tile_exp[n_valid - 1])
    tile_src = jnp.where(tile_valid, r, n_valid - 1)
    tile_first = jnp.logical_and(tile_valid, base == start[tile_exp])
    tiles = (tile_exp, tile_src, tile_valid.astype(i32), tile_first.astype(i32))
    return pos2.T.reshape(-1), tiles


SC_CHUNK = 128


def _sc_workers():
    info = pltpu.get_tpu_info().sparse_core
    return info.num_cores, info.num_cores * info.num_subcores


def _sc_scatter_rows(rows, idx, n_out):
    n_src, d = rows.shape
    n_cores, n_workers = _sc_workers()
    per_worker = idx.shape[0] // n_workers
    assert idx.shape[0] % (n_workers * SC_CHUNK) == 0 and n_src % per_worker == 0
    idx2d = idx.reshape(-1, SC_CHUNK)
    mesh = plsc.VectorSubcoreMesh(core_axis_name="c", subcore_axis_name="s")

    @functools.partial(pl.kernel, mesh=mesh, out_type=jax.ShapeDtypeStruct((n_out, d), rows.dtype),
                       scratch_types=[pltpu.VMEM((1, SC_CHUNK), jnp.int32), pltpu.VMEM((SC_CHUNK, d), rows.dtype)])
    def scatter(rows_hbm, idx_hbm, out_hbm, idx_v, rows_v):
        wid = lax.axis_index("s") * n_cores + lax.axis_index("c")

        @pl.loop(0, per_worker // SC_CHUNK)
        def _(j):
            a = wid * per_worker + j * SC_CHUNK
            pltpu.sync_copy(idx_hbm.at[pl.ds(a // SC_CHUNK, 1)], idx_v)
            pltpu.sync_copy(rows_hbm.at[pl.ds(lax.rem(a, n_src), SC_CHUNK)], rows_v)
            pltpu.sync_copy(rows_v, out_hbm.at[idx_v.at[0]])

    return scatter(rows, idx2d)


def _sc_gather_rows(table, idx):
    d = table.shape[1]
    n_cores, n_workers = _sc_workers()
    per_worker = idx.shape[0] // n_workers
    assert idx.shape[0] % (n_workers * SC_CHUNK) == 0
    idx2d = idx.reshape(-1, SC_CHUNK)
    mesh = plsc.VectorSubcoreMesh(core_axis_name="c", subcore_axis_name="s")

    @functools.partial(pl.kernel, mesh=mesh, out_type=jax.ShapeDtypeStruct((idx.shape[0], d), table.dtype),
                       scratch_types=[pltpu.VMEM((1, SC_CHUNK), jnp.int32), pltpu.VMEM((SC_CHUNK, d), table.dtype)])
    def gather(table_hbm, idx_hbm, out_hbm, idx_v, rows_v):
        wid = lax.axis_index("s") * n_cores + lax.axis_index("c")

        @pl.loop(0, per_worker // SC_CHUNK)
        def _(j):
            a = wid * per_worker + j * SC_CHUNK
            pltpu.sync_copy(idx_hbm.at[pl.ds(a // SC_CHUNK, 1)], idx_v)
            pltpu.sync_copy(table_hbm.at[idx_v.at[0]], rows_v)
            pltpu.sync_copy(rows_v, out_hbm.at[pl.ds(a, SC_CHUNK)])

    return gather(table, idx2d)


def _experts_kernel(t_exp, t_src, t_valid, t_first, xs_ref, wg_ref, wu_ref, wd_ref, ys_ref,
                    wg_c, wu_c, wd_c, x_sc, acc_ref):
    r = pl.program_id(0)
    c = pl.program_id(1)
    last = c == pl.num_programs(1) - 1

    @pl.when(t_first[r] != 0)
    def _():
        wg_c[c] = wg_ref[0, 0].astype(BF16)
        wu_c[c] = wu_ref[0, 0].astype(BF16)
        wd_c[c] = wd_ref[0, 0].astype(BF16)

    @pl.when(t_valid[r] != 0)
    def _():
        @pl.when(c == 0)
        def _():
            x_sc[...] = _unpack_pairs(xs_ref[...])
            acc_ref[...] = jnp.zeros(acc_ref.shape, F32)

        acc_ref[...] += _swiglu(x_sc[...], wg_c[c], wu_c[c], wd_c[c])

        @pl.when(last)
        def _():
            ys_ref[...] = _pack_pairs(acc_ref[...])

    @pl.when(jnp.logical_and(t_valid[r] == 0, last))
    def _():
        ys_ref[...] = jnp.zeros(ys_ref.shape, ys_ref.dtype)


def _experts(tiles, xs, wg, wu, wd, *, tf):
    n_rows, half = xs.shape
    n_ch = D_EXPERT // tf

    def w_in(r, c, e, s, v, f):
        return (0, e[r], 0, jnp.where(f[r] != 0, c, n_ch - 1))

    def w_dn(r, c, e, s, v, f):
        return (0, e[r], jnp.where(f[r] != 0, c, n_ch - 1), 0)

    grid_spec = pltpu.PrefetchScalarGridSpec(
        num_scalar_prefetch=4, grid=(n_rows // ROW_TILE, n_ch),
        in_specs=[pl.BlockSpec((ROW_TILE, half), lambda r, c, e, s, v, f: (s[r], 0)),
                  pl.BlockSpec((1, 1, D_MODEL, tf), w_in),
                  pl.BlockSpec((1, 1, D_MODEL, tf), w_in),
                  pl.BlockSpec((1, 1, tf, D_MODEL), w_dn)],
        out_specs=pl.BlockSpec((ROW_TILE, half), lambda r, c, e, s, v, f: (r, 0)),
        scratch_shapes=[pltpu.VMEM((n_ch, D_MODEL, tf), BF16), pltpu.VMEM((n_ch, D_MODEL, tf), BF16),
                        pltpu.VMEM((n_ch, tf, D_MODEL), BF16), pltpu.VMEM((ROW_TILE, D_MODEL), BF16),
                        pltpu.VMEM((ROW_TILE, D_MODEL), F32)])
    return pl.pallas_call(
        _experts_kernel, grid_spec=grid_spec,
        out_shape=jax.ShapeDtypeStruct((n_rows, half), jnp.uint32),
        compiler_params=_cparams(("arbitrary", "arbitrary")),
        name="moe_experts",
    )(*tiles, xs, wg, wu, wd)


def _combine_kernel(a_ref, b_ref, route_ref, x_ref, nw_ref, mod_ref, o_ref):
    route = route_ref[...]
    y = route[:, 2:3] * _unpack_pairs(a_ref[...]).astype(F32) + route[:, 3:4] * _unpack_pairs(b_ref[...]).astype(F32)
    gt2 = _mod_rows(mod_ref, 0)[5]
    o_ref[...] = x_ref[...] + gt2 * _rms(y, nw_ref[0, 3:4, :])


def _combine(rows, route, x3, nw, mod):
    n = x3.shape[0]
    nt = n // ROW_TILE
    half = rows.shape[1]
    return pl.pallas_call(
        _combine_kernel,
        out_shape=jax.ShapeDtypeStruct((n, D_MODEL), F32),
        grid=(nt,),
        in_specs=[pl.BlockSpec((ROW_TILE, half), lambda i: (i, 0)),
                  pl.BlockSpec((ROW_TILE, half), lambda i: (i + nt, 0)),
                  pl.BlockSpec((ROW_TILE, LANES), lambda i: (i, 0)),
                  pl.BlockSpec((ROW_TILE, D_MODEL), lambda i: (i, 0)),
                  pl.BlockSpec((1, 4, D_MODEL), lambda i: (0, 0, 0)),
                  pl.BlockSpec((1, 8, 6 * D_MODEL), lambda i: (0, 0, 0))],
        out_specs=pl.BlockSpec((ROW_TILE, D_MODEL), lambda i: (i, 0)),
        compiler_params=_cparams(("parallel",)),
        name="moe_combine",
    )(rows, rows, route, x3, nw, mod)


def _moe(h4p, route, cnt_tiles, x3, wg, wu, wd, nw, mod):
    n = h4p.shape[0]
    n_rows = 2 * n + N_EXPERTS * ROW_TILE
    pos, tiles = _route(route, cnt_tiles, n_rows // ROW_TILE)
    xs = _sc_scatter_rows(h4p, pos, n_rows)
    ys = _experts(tiles, xs, wg, wu, wd, tf=EXPERT_CHUNK)
    rows = _sc_gather_rows(ys, pos)
    return _combine(rows, route, x3, nw, mod)


def kernel(x, c, ctx, c_ctx, ada_w, ada_b, norm_w, e_w_in, e_q_gain, e_k_gain, e_w_out, e_ffn_gate,
           e_ffn_up, e_ffn_down, o_w_in, o_conv_w, o_w_out, o_router, o_exp_gate, o_exp_up, o_exp_down):
    assert x.shape[0] == 1 and x.shape[2] == D_MODEL and ada_w.shape[0] == 2
    n = x.shape[1]
    x2d = x[0]
    ctx2d = ctx[0]
    mod = _ada(c, c_ctx, ada_w, ada_b)
    mod0, mod1 = mod[0:1], mod[1:2]
    nw0, nw1 = norm_w[0:1], norm_w[1:2]

    w_in = e_w_in[0].astype(BF16)
    scale = HEAD_DIM ** -0.5 * np.log2(np.e)
    gain = jnp.concatenate([jnp.tile(e_q_gain[0], N_Q_HEADS) * scale,
                            jnp.tile(e_k_gain[0], N_KV_HEADS)])[None, :]
    score_bound = 1.02 * HEAD_DIM * scale * jnp.max(jnp.abs(e_q_gain[0])) * jnp.max(jnp.abs(e_k_gain[0]))

    def mixer_inputs(fast):
        qk_dtype = FP8 if fast else BF16
        qT, k, vT, f = _evenproj(x2d, nw0, mod0, w_in, gain, latent=True, tm=TOKEN_TILE, qk_dtype=qk_dtype)
        kc, vcT = _evenproj(ctx2d, nw0, mod0, w_in, gain, latent=False, tm=ctx2d.shape[0], qk_dtype=qk_dtype)
        return _attention(qT, k, vT, kc, vcT, bounded=fast, tq=ATTN_Q_TILE, tk=ATTN_KEY_TILE), f

    attnT, f = lax.cond(score_bound <= FAST_SCORE_BOUND, functools.partial(mixer_inputs, True),
                        functools.partial(mixer_inputs, False))
    four = _fourier(f)
    w_out = e_w_out[0].astype(BF16)
    x1, h2 = _outproj(attnT, four, x2d, w_out[:ATTN_WIDTH], w_out[ATTN_WIDTH:], nw0, mod0, tm=TOKEN_TILE)
    x2, h3 = _ffn(h2, x1, e_ffn_gate[0].astype(BF16), e_ffn_up[0].astype(BF16),
                  e_ffn_down[0].astype(BF16), nw0, mod0, nw1, mod1, tm=TOKEN_TILE)

    ow_in = o_w_in[0].astype(BF16)
    v_first, v_last = _convedge(h3, ow_in, tm=ROW_TILE)
    router_pad = jnp.pad(o_router[0], ((0, 0), (0, LANES - N_EXPERTS)))
    x3, h4p, route, cnt_tiles = _convmix(h3, x2, v_first, v_last, ow_in, o_conv_w, o_w_out[0].astype(BF16),
                                         nw1, mod1, router_pad, tm=ROW_TILE)
    out = _moe(h4p, route, cnt_tiles, x3, o_exp_gate, o_exp_up, o_exp_down, nw1, mod1)
    return out[None]
```

```python
import functools

import numpy as np
import jax
import jax.numpy as jnp
from jax import lax
from jax.experimental import pallas as pl
from jax.experimental.pallas import tpu as pltpu
from jax.experimental.pallas import tpu_sc as plsc

D_MODEL = 1024
GRID_W = 64
HEAD_DIM = 64
N_Q_HEADS = 12
N_KV_HEADS = 4
Q_PER_KV = N_Q_HEADS // N_KV_HEADS
ATTN_WIDTH = N_Q_HEADS * HEAD_DIM
KV_WIDTH = N_KV_HEADS * HEAD_DIM
QK_WIDTH = ATTN_WIDTH + KV_WIDTH
N_FOURIER_GROUPS = 4
FOURIER_GROUP_DIM = 64
FOURIER_WIDTH = N_FOURIER_GROUPS * FOURIER_GROUP_DIM
EVEN_IN_WIDTH = ATTN_WIDTH + 2 * KV_WIDTH + FOURIER_WIDTH
D_FF = 2816
N_EXPERTS = 8
D_EXPERT = 3584
ONES_ROWS = 16
V_ROWS = HEAD_DIM + ONES_ROWS
FAST_SCORE_BOUND = 24.0
ROPE_THETA = 10000.0
ROPE_HALF = HEAD_DIM // 4
NORM_EPS = 1e-6

LANES = 128
VMEM_LIMIT = 56 * 1024 * 1024

TOKEN_TILE = 512
ATTN_Q_TILE = 256
ATTN_KEY_TILE = 1024
EXPERT_CHUNK = 512

BF16 = jnp.bfloat16
FP8 = jnp.float8_e4m3fn
F32 = jnp.float32


def _cparams(semantics, vmem=VMEM_LIMIT):
    return pltpu.CompilerParams(dimension_semantics=semantics, vmem_limit_bytes=vmem)


def _dot(a, b):
    return jnp.dot(a, b, preferred_element_type=F32)


def _split_bf16(x):
    hi = x.astype(BF16)
    lo = (x - hi.astype(F32)).astype(BF16)
    return hi, lo


def _rms(x, g):
    return x * lax.rsqrt(jnp.mean(x * x, axis=-1, keepdims=True) + NORM_EPS) * g


def _mod_rows(mod_ref, row):
    return [mod_ref[0, row:row + 1, i * D_MODEL:(i + 1) * D_MODEL] for i in range(6)]


def _pack_pairs(x):
    k = x.shape[1] // 2
    bits = lax.bitcast_convert_type(x.astype(BF16).astype(F32), jnp.uint32)
    return (bits[:, :k] >> 16) | (bits[:, k:] & jnp.uint32(0xFFFF0000))


def _unpack_pairs(w):
    lo = lax.bitcast_convert_type(w << 16, F32).astype(BF16)
    hi = lax.bitcast_convert_type(w & jnp.uint32(0xFFFF0000), F32).astype(BF16)
    return jnp.concatenate([lo, hi], axis=1)


@functools.lru_cache(maxsize=None)
def _rope_tables(n_tokens):
    t = np.arange(n_tokens)
    row = (t // GRID_W).astype(np.float64)
    col = (t % GRID_W).astype(np.float64)
    inv = ROPE_THETA ** (-np.arange(ROPE_HALF, dtype=np.float64) / ROPE_HALF)
    ar, ac = row[:, None] * inv, col[:, None] * inv
    cos = np.concatenate([np.cos(ar), np.cos(ar), np.cos(ac), np.cos(ac)], axis=-1)
    sin = np.concatenate([-np.sin(ar), np.sin(ar), -np.sin(ac), np.sin(ac)], axis=-1)
    reps = LANES // HEAD_DIM
    return (np.tile(cos, (1, reps)).astype(np.float32), np.tile(sin, (1, reps)).astype(np.float32))


@functools.lru_cache(maxsize=None)
def _head_matrices():
    head = np.arange(QK_WIDTH) // HEAD_DIM
    red = (head[:, None] == np.arange(LANES)[None, :]).astype(np.float32) / HEAD_DIM
    exp = (np.arange(LANES)[:, None] == head[None, :]).astype(np.float32)
    return red, exp


@functools.lru_cache(maxsize=None)
def _fourier_tables(n_tokens, kb):
    n2 = LANES
    n1 = n_tokens // n2
    c = np.arange(FOURIER_GROUP_DIM)
    ang = 2 * np.pi * np.outer(c, c) / FOURIER_GROUP_DIM
    eye = np.eye(N_FOURIER_GROUPS)
    cs = np.concatenate([np.kron(eye, np.cos(ang)), np.kron(eye, np.sin(ang))], axis=1)
    k1 = np.arange(n1)
    th = 2 * np.pi * np.outer(k1, k1) / n1
    cr, ci = np.cos(th), -np.sin(th)
    base = np.block([[cr, ci], [ci, -cr]])
    psi = 2 * np.pi * np.outer(k1, np.arange(n2)) / n_tokens
    twr = np.cos(psi).reshape(n1, n2 // kb, kb).transpose(1, 0, 2)
    twi = (-np.sin(psi)).reshape(n1, n2 // kb, kb).transpose(1, 0, 2)
    k2 = np.arange(n2)
    ph = 2 * np.pi * np.outer(k2, k2) / n2
    fr, fi = np.cos(ph), -np.sin(ph)
    scale = 1.0 / np.sqrt(n_tokens * FOURIER_GROUP_DIM)
    m3 = np.stack([fr, -fi], axis=-1) * scale
    wb = np.einsum('knr,uv->kunrv', m3, np.eye(kb)).reshape(n2 * kb, n2 * 2 * kb)
    f32 = np.float32
    return cs.astype(f32), base.astype(f32), twr.astype(f32), twi.astype(f32), wb.astype(f32)


def _ada_kernel(cb_ref, w_ref, b_ref, o_ref):
    tn = o_ref.shape[-1]
    o_ref[...] = jnp.zeros(o_ref.shape, F32)
    for r in range(2):
        cb = cb_ref[r]
        s = cb * jax.nn.sigmoid(cb)
        for j in range(tn // LANES):
            sl = slice(j * LANES, (j + 1) * LANES)
            col = jnp.sum(s * w_ref[0, :, sl], axis=0, keepdims=True)
            o_ref[0, r:r + 1, sl] = col + b_ref[0, :, sl]


def _ada(c, c_ctx, ada_w, ada_b):
    depth = ada_w.shape[0]
    n = ada_w.shape[-1]
    tn = 1536
    cb = jnp.stack([jnp.broadcast_to(c[0][:, None], (D_MODEL, LANES)),
                    jnp.broadcast_to(c_ctx[:, None], (D_MODEL, LANES))])
    return pl.pallas_call(
        _ada_kernel,
        out_shape=jax.ShapeDtypeStruct((depth, 8, n), F32),
        grid=(depth, n // tn),
        in_specs=[pl.BlockSpec((2, D_MODEL, LANES), lambda i, j: (0, 0, 0)),
                  pl.BlockSpec((1, D_MODEL, tn), lambda i, j: (i, 0, j)),
                  pl.BlockSpec((1, 1, tn), lambda i, j: (i, 0, j))],
        out_specs=pl.BlockSpec((1, 8, tn), lambda i, j: (i, 0, j)),
        compiler_params=_cparams(("parallel", "parallel")),
        name="ada",
    )(cb, ada_w, ada_b[:, None, :])


def _evenproj_kernel(*refs, row, latent, qk_dtype):
    if latent:
        (x_ref, nw_ref, mod_ref, w_ref, gain_ref, red_ref, exp_ref, cos_ref, sin_ref,
         qT_ref, k_ref, vT_ref, f_ref) = refs
    else:
        x_ref, nw_ref, mod_ref, w_ref, gain_ref, red_ref, exp_ref, k_ref, vT_ref = refs
    sh, sc = _mod_rows(mod_ref, row)[:2]
    red = red_ref[...].astype(BF16)
    expm = exp_ref[...].astype(BF16)
    tm = x_ref.shape[0]
    n_parts = 2 if latent else 1
    m = tm // n_parts

    def matmuls(r0):
        h = (_rms(x_ref[r0:r0 + m, :], nw_ref[0, 0:1, :]) * (1.0 + sc) + sh).astype(BF16)
        z = _dot(h, w_ref[...])
        zqk = z[:, :QK_WIDTH]
        hi, lo = _split_bf16(zqk * zqk)
        ms = _dot(hi, red) + _dot(lo, red)
        rhi, rlo = _split_bf16(lax.rsqrt(ms + NORM_EPS))
        return z, zqk * (_dot(rhi, expm) + _dot(rlo, expm)) * gain_ref[...]

    def finish(r0, z, yn):
        if latent:
            lane = lax.broadcasted_iota(jnp.int32, (1, LANES), 1)
            first_half = (lane // ROPE_HALF) % 2 == 0
            cos, sin = cos_ref[r0:r0 + m, :], sin_ref[r0:r0 + m, :]
            chunks = []
            for c in range(QK_WIDTH // LANES):
                yc = yn[:, c * LANES:(c + 1) * LANES]
                partner = jnp.where(first_half, pltpu.roll(yc, LANES - ROPE_HALF, axis=1),
                                    pltpu.roll(yc, ROPE_HALF, axis=1))
                chunks.append(yc * cos + partner * sin)
            yn = jnp.concatenate(chunks, axis=1)
            qT_ref[:, r0:r0 + m] = yn[:, :ATTN_WIDTH].T.astype(qk_dtype)
            f_ref[r0:r0 + m, :] = z[:, QK_WIDTH + KV_WIDTH:].astype(BF16)
        for g in range(N_KV_HEADS):
            k_ref[g, r0:r0 + m, :] = yn[:, ATTN_WIDTH + g * HEAD_DIM:ATTN_WIDTH + (g + 1) * HEAD_DIM].astype(qk_dtype)
        vT = z[:, QK_WIDTH:QK_WIDTH + KV_WIDTH].T.astype(BF16)
        ones = jnp.ones((ONES_ROWS, m), BF16)
        for g in range(N_KV_HEADS):
            vT_ref[g * V_ROWS:g * V_ROWS + HEAD_DIM, r0:r0 + m] = vT[g * HEAD_DIM:(g + 1) * HEAD_DIM]
            vT_ref[g * V_ROWS + HEAD_DIM:(g + 1) * V_ROWS, r0:r0 + m] = ones

    parts = [(p * m,) + matmuls(p * m) for p in range(n_parts)]
    for r0, z, yn in parts:
        finish(r0, z, yn)


def _evenproj(x2d, nw, mod, w_bf, gain, *, latent, tm, qk_dtype):
    n = x2d.shape[0]
    red, expm = _head_matrices()
    const = lambda i: (0, 0)
    in_specs = [pl.BlockSpec((tm, D_MODEL), lambda i: (i, 0)),
                pl.BlockSpec((1, 4, D_MODEL), lambda i: (0, 0, 0)),
                pl.BlockSpec((1, 8, 6 * D_MODEL), lambda i: (0, 0, 0)),
                pl.BlockSpec((D_MODEL, EVEN_IN_WIDTH), const),
                pl.BlockSpec((1, QK_WIDTH), const),
                pl.BlockSpec((QK_WIDTH, LANES), const),
                pl.BlockSpec((LANES, QK_WIDTH), const)]
    args = [x2d, nw, mod, w_bf, gain, jnp.asarray(red), jnp.asarray(expm)]
    k_shape = jax.ShapeDtypeStruct((N_KV_HEADS, n, HEAD_DIM), qk_dtype)
    vT_shape = jax.ShapeDtypeStruct((N_KV_HEADS * V_ROWS, n), BF16)
    k_spec = pl.BlockSpec((N_KV_HEADS, tm, HEAD_DIM), lambda i: (0, i, 0))
    vT_spec = pl.BlockSpec((N_KV_HEADS * V_ROWS, tm), lambda i: (0, i))
    if latent:
        cos, sin = _rope_tables(n)
        in_specs += [pl.BlockSpec((tm, LANES), lambda i: (i, 0))] * 2
        args += [jnp.asarray(cos), jnp.asarray(sin)]
        out_shape = (jax.ShapeDtypeStruct((ATTN_WIDTH, n), qk_dtype), k_shape, vT_shape,
                     jax.ShapeDtypeStruct((n, FOURIER_WIDTH), BF16))
        out_specs = (pl.BlockSpec((ATTN_WIDTH, tm), lambda i: (0, i)), k_spec, vT_spec,
                     pl.BlockSpec((tm, FOURIER_WIDTH), lambda i: (i, 0)))
    else:
        out_shape = (k_shape, vT_shape)
        out_specs = (k_spec, vT_spec)
    return pl.pallas_call(
        functools.partial(_evenproj_kernel, row=0 if latent else 1, latent=latent, qk_dtype=qk_dtype),
        out_shape=out_shape, grid=(n // tm,), in_specs=in_specs, out_specs=out_specs,
        compiler_params=_cparams(("parallel",)),
        name="evenproj_lat" if latent else "evenproj_ctx",
    )(*args)


def _visit_all(visit, k_ref, vT_ref, kc_ref, vcT_ref, tk):
    def body(c, carry):
        off = pl.multiple_of(c * tk, tk)
        visit(k_ref[0, pl.ds(off, tk), :], vT_ref[:, pl.ds(off, tk)])
        return carry

    lax.fori_loop(0, k_ref.shape[1] // tk, body, 0)
    visit(kc_ref[0], vcT_ref[...])


def _attn_bounded_kernel(qT_ref, k_ref, vT_ref, kc_ref, vcT_ref, o_ref, acc_sc, *, tk):
    n_tiles = k_ref.shape[1] // tk
    acc_sc[...] = jnp.zeros(acc_sc.shape, F32)
    tiles = [(k_ref.at[0, c * tk:(c + 1) * tk, :], vT_ref.at[:, c * tk:(c + 1) * tk]) for c in range(n_tiles)]
    tiles.append((kc_ref.at[0], vcT_ref))
    steps = [(kt, vt, j) for kt, vt in tiles for j in range(Q_PER_KV)]

    def scores(step):
        kt, _, j = step
        return _dot(kt[...], qT_ref[j * HEAD_DIM:(j + 1) * HEAD_DIM, :])

    s = scores(steps[0])
    for n, (_, vt, j) in enumerate(steps):
        s_next = scores(steps[n + 1]) if n + 1 < len(steps) else None
        acc_sc[j] += _dot(vt[...], jnp.exp2(s).astype(BF16))
        s = s_next
    for j in range(Q_PER_KV):
        acc = acc_sc[j]
        o_ref[j * HEAD_DIM:(j + 1) * HEAD_DIM, :] = (acc[:HEAD_DIM] / acc[HEAD_DIM:HEAD_DIM + 1]).astype(BF16)


def _attn_online_kernel(qT_ref, k_ref, vT_ref, kc_ref, vcT_ref, o_ref, m_sc, acc_sc, *, tk):
    m_sc[...] = jnp.full(m_sc.shape, -jnp.inf, F32)
    acc_sc[...] = jnp.zeros(acc_sc.shape, F32)

    def visit(kt, vt):
        for j in range(Q_PER_KV):
            s = _dot(kt, qT_ref[j * HEAD_DIM:(j + 1) * HEAD_DIM, :])
            m_old = m_sc[j]
            m_new = jnp.maximum(m_old, jnp.max(s, axis=0, keepdims=True))
            p = jnp.exp2(s - m_new).astype(BF16)
            acc_sc[j] = jnp.exp2(m_old - m_new) * acc_sc[j] + _dot(vt, p)
            m_sc[j] = m_new

    _visit_all(visit, k_ref, vT_ref, kc_ref, vcT_ref, tk)
    for j in range(Q_PER_KV):
        acc = acc_sc[j]
        o_ref[j * HEAD_DIM:(j + 1) * HEAD_DIM, :] = (acc[:HEAD_DIM] / acc[HEAD_DIM:HEAD_DIM + 1]).astype(BF16)


def _attention(qT, k, vT, kc, vcT, *, bounded, tq, tk):
    n = qT.shape[1]
    n_ctx = kc.shape[1]
    gw = Q_PER_KV * HEAD_DIM
    common = dict(
        out_shape=jax.ShapeDtypeStruct((ATTN_WIDTH, n), BF16),
        grid=(N_KV_HEADS, n // tq),
        in_specs=[pl.BlockSpec((gw, tq), lambda g, i: (g, i)),
                  pl.BlockSpec((1, n, HEAD_DIM), lambda g, i: (g, 0, 0)),
                  pl.BlockSpec((V_ROWS, n), lambda g, i: (g, 0)),
                  pl.BlockSpec((1, n_ctx, HEAD_DIM), lambda g, i: (g, 0, 0)),
                  pl.BlockSpec((V_ROWS, n_ctx), lambda g, i: (g, 0))],
        out_specs=pl.BlockSpec((gw, tq), lambda g, i: (g, i)),
        compiler_params=_cparams(("parallel", "parallel")),
    )
    acc = pltpu.VMEM((Q_PER_KV, V_ROWS, tq), F32)
    if bounded:
        call = pl.pallas_call(functools.partial(_attn_bounded_kernel, tk=tk),
                              scratch_shapes=[acc], name="attn_bounded", **common)
    else:
        call = pl.pallas_call(functools.partial(_attn_online_kernel, tk=tk),
                              scratch_shapes=[pltpu.VMEM((Q_PER_KV, 1, tq), F32), acc],
                              name="attn_online", **common)
    return call(qT, k, vT, kc, vcT)


def _four_a_kernel(f_ref, cs_ref, base_ref, twr_ref, twi_ref, y_ref, *, nb):
    n1 = f_ref.shape[0]
    cs = cs_ref[...].astype(BF16)
    base = base_ref[...].astype(BF16)
    for u in range(nb):
        xb = f_ref[:, u * FOURIER_WIDTH:(u + 1) * FOURIER_WIDTH]
        ab = _dot(xb, cs)
        stacked = jnp.concatenate([ab[:, :FOURIER_WIDTH], ab[:, FOURIER_WIDTH:]], axis=0)
        p = _dot(base, stacked.astype(BF16))
        pr, pi = p[:n1], p[n1:]
        tr = twr_ref[0, :, u:u + 1]
        ti = twi_ref[0, :, u:u + 1]
        y_ref[u, 0] = tr * pr - ti * pi
        y_ref[u, 1] = tr * pi + ti * pr


def _four_b_kernel(y_ref, wb_ref, o_ref):
    n2, _, kb, w = y_ref.shape
    y = y_ref[...].reshape(n2 * 2 * kb, w).astype(BF16)
    o_ref[...] = _dot(wb_ref[...].astype(BF16), y).reshape(n2, kb, w)


def _fourier(f):
    n = f.shape[0]
    n2 = LANES
    n1 = n // n2
    nb = kb = 8
    cs, base, twr, twi, wb = (jnp.asarray(t) for t in _fourier_tables(n, kb))
    f2d = f.reshape(n1, n2 * FOURIER_WIDTH)
    y = pl.pallas_call(
        functools.partial(_four_a_kernel, nb=nb),
        out_shape=jax.ShapeDtypeStruct((n2, 2, n1, FOURIER_WIDTH), F32),
        grid=(n2 // nb,),
        in_specs=[pl.BlockSpec((n1, nb * FOURIER_WIDTH), lambda s: (0, s)),
                  pl.BlockSpec(cs.shape, lambda s: (0, 0)),
                  pl.BlockSpec(base.shape, lambda s: (0, 0)),
                  pl.BlockSpec((1, n1, nb), lambda s: (s, 0, 0)),
                  pl.BlockSpec((1, n1, nb), lambda s: (s, 0, 0))],
        out_specs=pl.BlockSpec((nb, 2, n1, FOURIER_WIDTH), lambda s: (s, 0, 0, 0)),
        compiler_params=_cparams(("parallel",)),
        name="four_a",
    )(f2d, cs, base, twr, twi)
    out = pl.pallas_call(
        _four_b_kernel,
        out_shape=jax.ShapeDtypeStruct((n2, n1, FOURIER_WIDTH), F32),
        grid=(n1 // kb,),
        in_specs=[pl.BlockSpec((n2, 2, kb, FOURIER_WIDTH), lambda s: (0, 0, s, 0)),
                  pl.BlockSpec(wb.shape, lambda s: (0, 0))],
        out_specs=pl.BlockSpec((n2, kb, FOURIER_WIDTH), lambda s: (0, s, 0)),
        compiler_params=_cparams(("parallel",)),
        name="four_b",
    )(y, wb)
    return out.reshape(n, FOURIER_WIDTH)


def _outproj_kernel(aT_ref, four_ref, x_ref, wa_ref, wf_ref, nw_ref, mod_ref, x1_ref, h_ref):
    _, _, gt1, sh2, sc2, _ = _mod_rows(mod_ref, 0)
    half = x_ref.shape[0] // 2
    starts = (0, half)
    ys = [lax.dot_general(aT_ref[:, r0:r0 + half], wa_ref[...], (((0,), (0,)), ((), ())),
                          preferred_element_type=F32)
          + _dot(four_ref[r0:r0 + half, :].astype(BF16), wf_ref[...]) for r0 in starts]
    for r0, y in zip(starts, ys):
        x1 = x_ref[r0:r0 + half, :] + gt1 * _rms(y, nw_ref[0, 1:2, :])
        x1_ref[r0:r0 + half, :] = x1
        h_ref[r0:r0 + half, :] = (_rms(x1, nw_ref[0, 2:3, :]) * (1.0 + sc2) + sh2).astype(BF16)


def _outproj(attnT, four, x2d, wa, wf, nw, mod, *, tm):
    n = x2d.shape[0]
    const = lambda i: (0, 0)
    return pl.pallas_call(
        _outproj_kernel,
        out_shape=(jax.ShapeDtypeStruct((n, D_MODEL), F32), jax.ShapeDtypeStruct((n, D_MODEL), BF16)),
        grid=(n // tm,),
        in_specs=[pl.BlockSpec((ATTN_WIDTH, tm), lambda i: (0, i)),
                  pl.BlockSpec((tm, FOURIER_WIDTH), lambda i: (i, 0)),
                  pl.BlockSpec((tm, D_MODEL), lambda i: (i, 0)),
                  pl.BlockSpec((ATTN_WIDTH, D_MODEL), const),
                  pl.BlockSpec((FOURIER_WIDTH, D_MODEL), const),
                  pl.BlockSpec((1, 4, D_MODEL), lambda i: (0, 0, 0)),
                  pl.BlockSpec((1, 8, 6 * D_MODEL), lambda i: (0, 0, 0))],
        out_specs=(pl.BlockSpec((tm, D_MODEL), lambda i: (i, 0)),
                   pl.BlockSpec((tm, D_MODEL), lambda i: (i, 0))),
        compiler_params=_cparams(("parallel",)),
        name="outproj",
    )(attnT, four, x2d, wa, wf, nw, mod)


def _swiglu(x, wg, wu, wd):
    g = _dot(x, wg)
    return _dot((g * jax.nn.sigmoid(g) * _dot(x, wu)).astype(BF16), wd)


def _ffn_kernel(h_ref, x_ref, wg_ref, wu_ref, wd_ref, nw_ref, mod_ref, nw1_ref, mod1_ref, x2_ref, h3_ref):
    gt2 = _mod_rows(mod_ref, 0)[5]
    sh, sc = _mod_rows(mod1_ref, 0)[:2]
    half = x_ref.shape[0] // 2
    starts = (0, half)
    ys = [_swiglu(h_ref[r0:r0 + half, :], wg_ref[...], wu_ref[...], wd_ref[...]) for r0 in starts]
    for r0, y in zip(starts, ys):
        x2 = x_ref[r0:r0 + half, :] + gt2 * _rms(y, nw_ref[0, 3:4, :])
        x2_ref[r0:r0 + half, :] = x2
        h3_ref[r0:r0 + half, :] = (_rms(x2, nw1_ref[0, 0:1, :]) * (1.0 + sc) + sh).astype(BF16)


def _ffn(h, x1, wg, wu, wd, nw, mod, nw1, mod1, *, tm):
    n = h.shape[0]
    nwspec = pl.BlockSpec((1, 4, D_MODEL), lambda i: (0, 0, 0))
    modspec = pl.BlockSpec((1, 8, 6 * D_MODEL), lambda i: (0, 0, 0))
    resident = dict(index_map=lambda i: (0, 0), pipeline_mode=pl.Buffered(1))
    return pl.pallas_call(
        _ffn_kernel,
        out_shape=(jax.ShapeDtypeStruct((n, D_MODEL), F32), jax.ShapeDtypeStruct((n, D_MODEL), BF16)),
        grid=(n // tm,),
        in_specs=[pl.BlockSpec((tm, D_MODEL), lambda i: (i, 0)),
                  pl.BlockSpec((tm, D_MODEL), lambda i: (i, 0)),
                  pl.BlockSpec((D_MODEL, D_FF), **resident),
                  pl.BlockSpec((D_MODEL, D_FF), **resident),
                  pl.BlockSpec((D_FF, D_MODEL), **resident),
                  nwspec, modspec, nwspec, modspec],
        out_specs=(pl.BlockSpec((tm, D_MODEL), lambda i: (i, 0)),
                   pl.BlockSpec((tm, D_MODEL), lambda i: (i, 0))),
        compiler_params=_cparams(("parallel",)),
        name="ffn",
    )(h, x1, wg, wu, wd, nw, mod, nw1, mod1)


EDGE_ROWS = 16


def _convedge_kernel(hf_ref, hl_ref, wc_ref, wu_ref, vf_ref, vl_ref):
    nt = hf_ref.shape[0]
    for h_ref, v_ref in ((hf_ref, vf_ref), (hl_ref, vl_ref)):
        h = h_ref[...].reshape(nt * EDGE_ROWS, D_MODEL)
        v = _dot(h, wc_ref[...]) * _dot(h, wu_ref[...])
        v_ref[...] = v.reshape(nt, EDGE_ROWS, D_MODEL)


def _convedge(h3, w_in_bf, *, tm):
    n = h3.shape[0]
    nt = n // tm
    h3t = h3.reshape(nt, tm, D_MODEL)
    last = tm // EDGE_ROWS - 1
    shape = jax.ShapeDtypeStruct((nt, EDGE_ROWS, D_MODEL), F32)
    return pl.pallas_call(
        _convedge_kernel,
        out_shape=(shape, shape),
        grid=(1,),
        in_specs=[pl.BlockSpec((nt, EDGE_ROWS, D_MODEL), lambda i: (0, 0, 0)),
                  pl.BlockSpec((nt, EDGE_ROWS, D_MODEL), lambda i: (0, last, 0)),
                  pl.BlockSpec((D_MODEL, D_MODEL), lambda i: (0, 1)),
                  pl.BlockSpec((D_MODEL, D_MODEL), lambda i: (0, 2))],
        out_specs=(pl.BlockSpec((nt, EDGE_ROWS, D_MODEL), lambda i: (0, 0, 0)),
                   pl.BlockSpec((nt, EDGE_ROWS, D_MODEL), lambda i: (0, 0, 0))),
        compiler_params=_cparams(("arbitrary",)),
        name="convedge",
    )(h3t, h3t, w_in_bf, w_in_bf)


def _convmix_kernel(h_ref, x_ref, vl_ref, vf_ref, win_ref, cw_ref, wout_ref, nw_ref, mod_ref, r_ref,
                    x3_ref, h4_ref, route_ref, cnt_ref):
    i = pl.program_id(0)
    tm = h_ref.shape[0]
    _, _, gt1, sh2, sc2, _ = _mod_rows(mod_ref, 0)
    z = _dot(h_ref[...], win_ref[...])
    b = z[:, :D_MODEL]
    v = z[:, D_MODEL:2 * D_MODEL] * z[:, 2 * D_MODEL:]
    has_prev = (i > 0).astype(F32)
    has_next = (i < pl.num_programs(0) - 1).astype(F32)
    prev_row = vl_ref[0, EDGE_ROWS - 1:EDGE_ROWS, :] * has_prev
    next_row = vf_ref[0, 0:1, :] * has_next
    rows = lax.broadcasted_iota(jnp.int32, (tm, 1), 0)
    v_dn = jnp.where(rows == 0, prev_row, pltpu.roll(v, 1, axis=0))
    v_up = jnp.where(rows == tm - 1, next_row, pltpu.roll(v, tm - 1, axis=0))
    conv = v_dn * cw_ref[0, 0:1, :] + v * cw_ref[0, 1:2, :] + v_up * cw_ref[0, 2:3, :]
    gated = (b * conv).astype(BF16)
    rhi, rlo = _split_bf16(r_ref[...])
    half = tm // 2
    ys = [_dot(gated[r0:r0 + half], wout_ref[...]) for r0 in (0, half)]
    logit_parts = []
    for r0, y in zip((0, half), ys):
        x3 = x_ref[r0:r0 + half, :] + gt1 * _rms(y, nw_ref[0, 1:2, :])
        x3_ref[r0:r0 + half, :] = x3
        h4 = _rms(x3, nw_ref[0, 2:3, :]) * (1.0 + sc2) + sh2
        h4_ref[r0:r0 + half, :] = _pack_pairs(h4)
        hhi, hlo = _split_bf16(h4)
        logit_parts.append(_dot(hhi, rhi) + (_dot(hlo, rhi) + _dot(hhi, rlo)))
    logits = jnp.concatenate(logit_parts, axis=0)
    lt = logits.T[:N_EXPERTS]
    e = jnp.exp(lt - jnp.max(lt, axis=0, keepdims=True))
    probs = e / jnp.sum(e, axis=0, keepdims=True)
    row = lax.broadcasted_iota(jnp.int32, lt.shape, 0).astype(F32)
    v1 = jnp.max(probs, axis=0, keepdims=True)
    i1 = jnp.min(jnp.where(probs == v1, row, float(N_EXPERTS)), axis=0, keepdims=True)
    rest = jnp.where(row == i1, -1.0, probs)
    v2 = jnp.max(rest, axis=0, keepdims=True)
    i2 = jnp.min(jnp.where(rest == v2, row, float(N_EXPERTS)), axis=0, keepdims=True)
    tot = v1 + v2
    sel = jnp.where(jnp.logical_or(row == i1, row == i2), 1.0, 0.0)
    earlier = lax.broadcasted_iota(jnp.int32, (tm, tm), 0) < lax.broadcasted_iota(jnp.int32, (tm, tm), 1)
    sel16 = jnp.concatenate([sel, jnp.zeros_like(sel)], axis=0).astype(BF16)
    rank = _dot(sel16, jnp.where(earlier, 1.0, 0.0).astype(BF16))[:N_EXPERTS]
    r1 = jnp.sum(jnp.where(row == i1, rank, 0.0), axis=0, keepdims=True)
    r2 = jnp.sum(jnp.where(row == i2, rank, 0.0), axis=0, keepdims=True)
    packed = jnp.concatenate([i1, i2, v1 / tot, v2 / tot, r1, r2, jnp.zeros((LANES - 6, tm), F32)], axis=0)
    route_ref[...] = packed.T
    cnt_ref[0] = jnp.broadcast_to(jnp.sum(sel, axis=1, keepdims=True), cnt_ref.shape[1:])


def _convmix(h3, x2, v_first, v_last, w_in_bf, conv_w, w_out_bf, nw, mod, router_pad, *, tm):
    n = h3.shape[0]
    nt = n // tm
    const = lambda i: (0, 0)
    return pl.pallas_call(
        _convmix_kernel,
        out_shape=(jax.ShapeDtypeStruct((n, D_MODEL), F32), jax.ShapeDtypeStruct((n, D_MODEL // 2), jnp.uint32),
                   jax.ShapeDtypeStruct((n, LANES), F32), jax.ShapeDtypeStruct((nt, 8, LANES), F32)),
        grid=(nt,),
        in_specs=[pl.BlockSpec((tm, D_MODEL), lambda i: (i, 0)),
                  pl.BlockSpec((tm, D_MODEL), lambda i: (i, 0)),
                  pl.BlockSpec((1, EDGE_ROWS, D_MODEL), lambda i: (jnp.maximum(i - 1, 0), 0, 0)),
                  pl.BlockSpec((1, EDGE_ROWS, D_MODEL), lambda i: (jnp.minimum(i + 1, nt - 1), 0, 0)),
                  pl.BlockSpec((D_MODEL, 3 * D_MODEL), const),
                  pl.BlockSpec((1, 3, D_MODEL), lambda i: (0, 0, 0)),
                  pl.BlockSpec((D_MODEL, D_MODEL), const),
                  pl.BlockSpec((1, 4, D_MODEL), lambda i: (0, 0, 0)),
                  pl.BlockSpec((1, 8, 6 * D_MODEL), lambda i: (0, 0, 0)),
                  pl.BlockSpec((D_MODEL, LANES), const)],
        out_specs=(pl.BlockSpec((tm, D_MODEL), lambda i: (i, 0)),
                   pl.BlockSpec((tm, D_MODEL // 2), lambda i: (i, 0)),
                   pl.BlockSpec((tm, LANES), lambda i: (i, 0)),
                   pl.BlockSpec((1, 8, LANES), lambda i: (i, 0, 0))),
        compiler_params=_cparams(("parallel",)),
        name="convmix",
    )(h3, x2, v_last, v_first, w_in_bf, conv_w, w_out_bf, nw, mod, router_pad)


ROW_TILE = TOKEN_TILE


def _count_le(sorted_vals, x):
    return jnp.sum((sorted_vals[None, :] <= x[:, None]).astype(jnp.int32), axis=1)


def _route(route, cnt_tiles, n_row_tiles):
    n = route.shape[0]
    i32 = jnp.int32
    cnt_tile = cnt_tiles[:, :N_EXPERTS, 0].astype(i32)
    cum_end = jnp.cumsum(cnt_tile, axis=0)
    cnt = cum_end[-1]
    cnt_pad = (cnt + ROW_TILE - 1) // ROW_TILE * ROW_TILE
    grp_end = jnp.cumsum(cnt_pad)
    start = grp_end - cnt_pad
    offs = jnp.repeat(start[None, :] + cum_end - cnt_tile, ROW_TILE, axis=0)
    experts = jnp.clip(route[:, :2].astype(i32), 0, N_EXPERTS - 1)
    picked = experts[:, :, None] == jnp.arange(N_EXPERTS, dtype=i32)[None, None, :]
    pos2 = jnp.sum(jnp.where(picked, offs[:, None, :], 0), axis=-1) + route[:, 4:6].astype(i32)
    pos2 = jnp.clip(pos2, 0, n_row_tiles * ROW_TILE - 1)
    r = jnp.arange(n_row_tiles, dtype=i32)
    base = r * ROW_TILE
    tile_valid = base < grp_end[-1]
    n_valid = jnp.clip(grp_end[-1] // ROW_TILE, 1, n_row_tiles)
    tile_exp = jnp.minimum(_count_le(grp_end, base), N_EXPERTS - 1)
    tile_exp = jnp.where(tile_valid, tile_exp, tile_exp[n_valid - 1])
    tile_src = jnp.where(tile_valid, r, n_valid - 1)
    tile_first = jnp.logical_and(tile_valid, base == start[tile_exp])
    tiles = (tile_exp, tile_src, tile_valid.astype(i32), tile_first.astype(i32))
    return pos2.T.reshape(-1), tiles


SC_CHUNK = 128


def _sc_workers():
    info = pltpu.get_tpu_info().sparse_core
    return info.num_cores, info.num_cores * info.num_subcores


def _sc_scatter_rows(rows, idx, n_out):
    n_src, d = rows.shape
    n_cores, n_workers = _sc_workers()
    per_worker = idx.shape[0] // n_workers
    assert idx.shape[0] % (n_workers * SC_CHUNK) == 0 and n_src % per_worker == 0
    idx2d = idx.reshape(-1, SC_CHUNK)
    mesh = plsc.VectorSubcoreMesh(core_axis_name="c", subcore_axis_name="s")

    @functools.partial(pl.kernel, mesh=mesh, out_type=jax.ShapeDtypeStruct((n_out, d), rows.dtype),
                       scratch_types=[pltpu.VMEM((1, SC_CHUNK), jnp.int32), pltpu.VMEM((SC_CHUNK, d), rows.dtype)])
    def scatter(rows_hbm, idx_hbm, out_hbm, idx_v, rows_v):
        wid = lax.axis_index("s") * n_cores + lax.axis_index("c")

        @pl.loop(0, per_worker // SC_CHUNK)
        def _(j):
            a = wid * per_worker + j * SC_CHUNK
            pltpu.sync_copy(idx_hbm.at[pl.ds(a // SC_CHUNK, 1)], idx_v)
            pltpu.sync_copy(rows_hbm.at[pl.ds(lax.rem(a, n_src), SC_CHUNK)], rows_v)
            pltpu.sync_copy(rows_v, out_hbm.at[idx_v.at[0]])

    return scatter(rows, idx2d)


def _sc_gather_rows(table, idx):
    d = table.shape[1]
    n_cores, n_workers = _sc_workers()
    per_worker = idx.shape[0] // n_workers
    assert idx.shape[0] % (n_workers * SC_CHUNK) == 0
    idx2d = idx.reshape(-1, SC_CHUNK)
    mesh = plsc.VectorSubcoreMesh(core_axis_name="c", subcore_axis_name="s")

    @functools.partial(pl.kernel, mesh=mesh, out_type=jax.ShapeDtypeStruct((idx.shape[0], d), table.dtype),
                       scratch_types=[pltpu.VMEM((1, SC_CHUNK), jnp.int32), pltpu.VMEM((SC_CHUNK, d), table.dtype)])
    def gather(table_hbm, idx_hbm, out_hbm, idx_v, rows_v):
        wid = lax.axis_index("s") * n_cores + lax.axis_index("c")

        @pl.loop(0, per_worker // SC_CHUNK)
        def _(j):
            a = wid * per_worker + j * SC_CHUNK
            pltpu.sync_copy(idx_hbm.at[pl.ds(a // SC_CHUNK, 1)], idx_v)
            pltpu.sync_copy(table_hbm.at[idx_v.at[0]], rows_v)
            pltpu.sync_copy(rows_v, out_hbm.at[pl.ds(a, SC_CHUNK)])

    return gather(table, idx2d)


ITEM_IDLE, ITEM_CHUNK, ITEM_TILE = 0, 1, 2


def _expert_items(tiles, n_ch):
    tile_exp, _, tile_valid, tile_first = tiles
    n_row_tiles = tile_exp.shape[0]
    i32 = jnp.int32
    per_tile = jnp.where(tile_first != 0, n_ch, tile_valid)
    off_end = jnp.cumsum(per_tile)
    total = off_end[-1]
    n_items = N_EXPERTS * n_ch + n_row_tiles
    k = jnp.arange(n_items, dtype=i32)
    kk = jnp.minimum(k, total - 1)
    r = jnp.clip(_count_le(off_end, kk), 0, n_row_tiles - 1)
    first = tile_first[r] != 0
    kind = jnp.where(k < total, jnp.where(first, ITEM_CHUNK, ITEM_TILE), ITEM_IDLE).astype(i32)
    chunk = jnp.where(first, kk - (off_end[r] - per_tile[r]), n_ch - 1)
    return r.astype(i32), jnp.clip(chunk, 0, n_ch - 1).astype(i32), tile_exp[r].astype(i32), kind


def _experts_kernel(it_tile, it_chunk, it_exp, it_kind, xs_ref, wg_ref, wu_ref, wd_ref, ys_ref,
                    wg_c, wu_c, wd_c, x_sc, acc_ref):
    k = pl.program_id(0)
    n_ch = wg_c.shape[0]

    @pl.when(it_kind[k] == ITEM_CHUNK)
    def _():
        c = it_chunk[k]
        wg_c[c] = wg_ref[0, 0].astype(BF16)
        wu_c[c] = wu_ref[0, 0].astype(BF16)
        wd_c[c] = wd_ref[0, 0].astype(BF16)

        @pl.when(c == 0)
        def _():
            x_sc[...] = _unpack_pairs(xs_ref[...])
            acc_ref[...] = jnp.zeros(acc_ref.shape, F32)

        acc_ref[...] += _swiglu(x_sc[...], wg_c[c], wu_c[c], wd_c[c])

        @pl.when(c == n_ch - 1)
        def _():
            ys_ref[...] = _pack_pairs(acc_ref[...])

    @pl.when(it_kind[k] == ITEM_TILE)
    def _():
        x = _unpack_pairs(xs_ref[...])
        y = _swiglu(x, wg_c[0], wu_c[0], wd_c[0])
        for c in range(1, n_ch):
            y = y + _swiglu(x, wg_c[c], wu_c[c], wd_c[c])
        ys_ref[...] = _pack_pairs(y)


def _experts(tiles, xs, wg, wu, wd, *, tf):
    n_rows, half = xs.shape
    n_ch = D_EXPERT // tf
    items = _expert_items(tiles, n_ch)
    grid_spec = pltpu.PrefetchScalarGridSpec(
        num_scalar_prefetch=4, grid=(items[0].shape[0],),
        in_specs=[pl.BlockSpec((ROW_TILE, half), lambda k, t, c, e, kind: (t[k], 0)),
                  pl.BlockSpec((1, 1, D_MODEL, tf), lambda k, t, c, e, kind: (0, e[k], 0, c[k])),
                  pl.BlockSpec((1, 1, D_MODEL, tf), lambda k, t, c, e, kind: (0, e[k], 0, c[k])),
                  pl.BlockSpec((1, 1, tf, D_MODEL), lambda k, t, c, e, kind: (0, e[k], c[k], 0))],
        out_specs=pl.BlockSpec((ROW_TILE, half), lambda k, t, c, e, kind: (t[k], 0)),
        scratch_shapes=[pltpu.VMEM((n_ch, D_MODEL, tf), BF16), pltpu.VMEM((n_ch, D_MODEL, tf), BF16),
                        pltpu.VMEM((n_ch, tf, D_MODEL), BF16), pltpu.VMEM((ROW_TILE, D_MODEL), BF16),
                        pltpu.VMEM((ROW_TILE, D_MODEL), F32)])
    return pl.pallas_call(
        _experts_kernel, grid_spec=grid_spec,
        out_shape=jax.ShapeDtypeStruct((n_rows, half), jnp.uint32),
        compiler_params=_cparams(("arbitrary",)),
        name="moe_experts",
    )(*items, xs, wg, wu, wd)


def _combine_kernel(a_ref, b_ref, route_ref, x_ref, nw_ref, mod_ref, o_ref):
    route = route_ref[...]
    y = route[:, 2:3] * _unpack_pairs(a_ref[...]).astype(F32) + route[:, 3:4] * _unpack_pairs(b_ref[...]).astype(F32)
    gt2 = _mod_rows(mod_ref, 0)[5]
    o_ref[...] = x_ref[...] + gt2 * _rms(y, nw_ref[0, 3:4, :])


def _combine(rows, route, x3, nw, mod):
    n = x3.shape[0]
    nt = n // ROW_TILE
    half = rows.shape[1]
    return pl.pallas_call(
        _combine_kernel,
        out_shape=jax.ShapeDtypeStruct((n, D_MODEL), F32),
        grid=(nt,),
        in_specs=[pl.BlockSpec((ROW_TILE, half), lambda i: (i, 0)),
                  pl.BlockSpec((ROW_TILE, half), lambda i: (i + nt, 0)),
                  pl.BlockSpec((ROW_TILE, LANES), lambda i: (i, 0)),
                  pl.BlockSpec((ROW_TILE, D_MODEL), lambda i: (i, 0)),
                  pl.BlockSpec((1, 4, D_MODEL), lambda i: (0, 0, 0)),
                  pl.BlockSpec((1, 8, 6 * D_MODEL), lambda i: (0, 0, 0))],
        out_specs=pl.BlockSpec((ROW_TILE, D_MODEL), lambda i: (i, 0)),
        compiler_params=_cparams(("parallel",)),
        name="moe_combine",
    )(rows, rows, route, x3, nw, mod)


def _moe(h4p, route, cnt_tiles, x3, wg, wu, wd, nw, mod):
    n = h4p.shape[0]
    n_rows = 2 * n + N_EXPERTS * ROW_TILE
    pos, tiles = _route(route, cnt_tiles, n_rows // ROW_TILE)
    xs = _sc_scatter_rows(h4p, pos, n_rows)
    ys = _experts(tiles, xs, wg, wu, wd, tf=EXPERT_CHUNK)
    rows = _sc_gather_rows(ys, pos)
    return _combine(rows, route, x3, nw, mod)


def kernel(x, c, ctx, c_ctx, ada_w, ada_b, norm_w, e_w_in, e_q_gain, e_k_gain, e_w_out, e_ffn_gate,
           e_ffn_up, e_ffn_down, o_w_in, o_conv_w, o_w_out, o_router, o_exp_gate, o_exp_up, o_exp_down):
    assert x.shape[0] == 1 and x.shape[2] == D_MODEL and ada_w.shape[0] == 2
    n = x.shape[1]
    x2d = x[0]
    ctx2d = ctx[0]
    mod = _ada(c, c_ctx, ada_w, ada_b)
    mod0, mod1 = mod[0:1], mod[1:2]
    nw0, nw1 = norm_w[0:1], norm_w[1:2]

    w_in = e_w_in[0].astype(BF16)
    scale = HEAD_DIM ** -0.5 * np.log2(np.e)
    gain = jnp.concatenate([jnp.tile(e_q_gain[0], N_Q_HEADS) * scale,
                            jnp.tile(e_k_gain[0], N_KV_HEADS)])[None, :]
    score_bound = 1.02 * HEAD_DIM * scale * jnp.max(jnp.abs(e_q_gain[0])) * jnp.max(jnp.abs(e_k_gain[0]))

    def mixer_inputs(fast):
        qk_dtype = FP8 if fast else BF16
        qT, k, vT, f = _evenproj(x2d, nw0, mod0, w_in, gain, latent=True, tm=TOKEN_TILE, qk_dtype=qk_dtype)
        kc, vcT = _evenproj(ctx2d, nw0, mod0, w_in, gain, latent=False, tm=ctx2d.shape[0], qk_dtype=qk_dtype)
        return _attention(qT, k, vT, kc, vcT, bounded=fast, tq=ATTN_Q_TILE, tk=ATTN_KEY_TILE), f

    attnT, f = lax.cond(score_bound <= FAST_SCORE_BOUND, functools.partial(mixer_inputs, True),
                        functools.partial(mixer_inputs, False))
    four = _fourier(f)
    w_out = e_w_out[0].astype(BF16)
    x1, h2 = _outproj(attnT, four, x2d, w_out[:ATTN_WIDTH], w_out[ATTN_WIDTH:], nw0, mod0, tm=TOKEN_TILE)
    x2, h3 = _ffn(h2, x1, e_ffn_gate[0].astype(BF16), e_ffn_up[0].astype(BF16),
                  e_ffn_down[0].astype(BF16), nw0, mod0, nw1, mod1, tm=TOKEN_TILE)

    ow_in = o_w_in[0].astype(BF16)
    v_first, v_last = _convedge(h3, ow_in, tm=ROW_TILE)
    router_pad = jnp.pad(o_router[0], ((0, 0), (0, LANES - N_EXPERTS)))
    x3, h4p, route, cnt_tiles = _convmix(h3, x2, v_first, v_last, ow_in, o_conv_w, o_w_out[0].astype(BF16),
                                         nw1, mod1, router_pad, tm=ROW_TILE)
    out = _moe(h4p, route, cnt_tiles, x3, o_exp_gate, o_exp_up, o_exp_down, nw1, mod1)
    return out[None]
```

```python
import functools

import numpy as np
import jax
import jax.numpy as jnp
from jax import lax
from jax.experimental import pallas as pl
from jax.experimental.pallas import tpu as pltpu
from jax.experimental.pallas import tpu_sc as plsc

D_MODEL = 1024
GRID_W = 64
HEAD_DIM = 64
N_Q_HEADS = 12
N_KV_HEADS = 4
Q_PER_KV = N_Q_HEADS // N_KV_HEADS
ATTN_WIDTH = N_Q_HEADS * HEAD_DIM
KV_WIDTH = N_KV_HEADS * HEAD_DIM
QK_WIDTH = ATTN_WIDTH + KV_WIDTH
N_FOURIER_GROUPS = 4
FOURIER_GROUP_DIM = 64
FOURIER_WIDTH = N_FOURIER_GROUPS * FOURIER_GROUP_DIM
EVEN_IN_WIDTH = ATTN_WIDTH + 2 * KV_WIDTH + FOURIER_WIDTH
D_FF = 2816
N_EXPERTS = 8
D_EXPERT = 3584
ONES_ROWS = 16
V_ROWS = HEAD_DIM + ONES_ROWS
FAST_SCORE_BOUND = 24.0
ROPE_THETA = 10000.0
ROPE_HALF = HEAD_DIM // 4
NORM_EPS = 1e-6

LANES = 128
VMEM_LIMIT = 56 * 1024 * 1024

TOKEN_TILE = 512
ATTN_Q_TILE = 256
ATTN_KEY_TILE = 1024
EXPERT_CHUNK = 512

BF16 = jnp.bfloat16
FP8 = jnp.float8_e4m3fn
F32 = jnp.float32


def _cparams(semantics, vmem=VMEM_LIMIT):
    return pltpu.CompilerParams(dimension_semantics=semantics, vmem_limit_bytes=vmem)


def _dot(a, b):
    return jnp.dot(a, b, preferred_element_type=F32)


def _split_bf16(x):
    hi = x.astype(BF16)
    lo = (x - hi.astype(F32)).astype(BF16)
    return hi, lo


def _rms(x, g):
    return x * lax.rsqrt(jnp.mean(x * x, axis=-1, keepdims=True) + NORM_EPS) * g


def _mod_rows(mod_ref, row):
    return [mod_ref[0, row:row + 1, i * D_MODEL:(i + 1) * D_MODEL] for i in range(6)]


def _pack_pairs(x):
    k = x.shape[1] // 2
    bits = lax.bitcast_convert_type(x.astype(BF16).astype(F32), jnp.uint32)
    return (bits[:, :k] >> 16) | (bits[:, k:] & jnp.uint32(0xFFFF0000))


def _unpack_pairs(w):
    lo = lax.bitcast_convert_type(w << 16, F32).astype(BF16)
    hi = lax.bitcast_convert_type(w & jnp.uint32(0xFFFF0000), F32).astype(BF16)
    return jnp.concatenate([lo, hi], axis=1)


@functools.lru_cache(maxsize=None)
def _rope_tables(n_tokens):
    t = np.arange(n_tokens)
    row = (t // GRID_W).astype(np.float64)
    col = (t % GRID_W).astype(np.float64)
    inv = ROPE_THETA ** (-np.arange(ROPE_HALF, dtype=np.float64) / ROPE_HALF)
    ar, ac = row[:, None] * inv, col[:, None] * inv
    cos = np.concatenate([np.cos(ar), np.cos(ar), np.cos(ac), np.cos(ac)], axis=-1)
    sin = np.concatenate([-np.sin(ar), np.sin(ar), -np.sin(ac), np.sin(ac)], axis=-1)
    reps = LANES // HEAD_DIM
    return (np.tile(cos, (1, reps)).astype(np.float32), np.tile(sin, (1, reps)).astype(np.float32))


@functools.lru_cache(maxsize=None)
def _head_matrices():
    head = np.arange(QK_WIDTH) // HEAD_DIM
    red = (head[:, None] == np.arange(LANES)[None, :]).astype(np.float32) / HEAD_DIM
    exp = (np.arange(LANES)[:, None] == head[None, :]).astype(np.float32)
    return red, exp


@functools.lru_cache(maxsize=None)
def _fourier_tables(n_tokens, kb):
    n2 = LANES
    n1 = n_tokens // n2
    c = np.arange(FOURIER_GROUP_DIM)
    ang = 2 * np.pi * np.outer(c, c) / FOURIER_GROUP_DIM
    eye = np.eye(N_FOURIER_GROUPS)
    cs = np.concatenate([np.kron(eye, np.cos(ang)), np.kron(eye, np.sin(ang))], axis=1)
    k1 = np.arange(n1)
    th = 2 * np.pi * np.outer(k1, k1) / n1
    cr, ci = np.cos(th), -np.sin(th)
    base = np.block([[cr, ci], [ci, -cr]])
    psi = 2 * np.pi * np.outer(k1, np.arange(n2)) / n_tokens
    twr = np.cos(psi).reshape(n1, n2 // kb, kb).transpose(1, 0, 2)
    twi = (-np.sin(psi)).reshape(n1, n2 // kb, kb).transpose(1, 0, 2)
    k2 = np.arange(n2)
    ph = 2 * np.pi * np.outer(k2, k2) / n2
    fr, fi = np.cos(ph), -np.sin(ph)
    scale = 1.0 / np.sqrt(n_tokens * FOURIER_GROUP_DIM)
    m3 = np.stack([fr, -fi], axis=-1) * scale
    wb = np.einsum('knr,uv->kunrv', m3, np.eye(kb)).reshape(n2 * kb, n2 * 2 * kb)
    f32 = np.float32
    return cs.astype(f32), base.astype(f32), twr.astype(f32), twi.astype(f32), wb.astype(f32)


def _ada_kernel(cb_ref, w_ref, b_ref, o_ref):
    tn = o_ref.shape[-1]
    o_ref[...] = jnp.zeros(o_ref.shape, F32)
    for r in range(2):
        cb = cb_ref[r]
        s = cb * jax.nn.sigmoid(cb)
        for j in range(tn // LANES):
            sl = slice(j * LANES, (j + 1) * LANES)
            col = jnp.sum(s * w_ref[0, :, sl], axis=0, keepdims=True)
            o_ref[0, r:r + 1, sl] = col + b_ref[0, :, sl]


def _ada(c, c_ctx, ada_w, ada_b):
    depth = ada_w.shape[0]
    n = ada_w.shape[-1]
    tn = 1536
    cb = jnp.stack([jnp.broadcast_to(c[0][:, None], (D_MODEL, LANES)),
                    jnp.broadcast_to(c_ctx[:, None], (D_MODEL, LANES))])
    return pl.pallas_call(
        _ada_kernel,
        out_shape=jax.ShapeDtypeStruct((depth, 8, n), F32),
        grid=(depth, n // tn),
        in_specs=[pl.BlockSpec((2, D_MODEL, LANES), lambda i, j: (0, 0, 0)),
                  pl.BlockSpec((1, D_MODEL, tn), lambda i, j: (i, 0, j)),
                  pl.BlockSpec((1, 1, tn), lambda i, j: (i, 0, j))],
        out_specs=pl.BlockSpec((1, 8, tn), lambda i, j: (i, 0, j)),
        compiler_params=_cparams(("parallel", "parallel")),
        name="ada",
    )(cb, ada_w, ada_b[:, None, :])


def _evenproj_kernel(*refs, row, latent, qk_dtype):
    if latent:
        (x_ref, nw_ref, mod_ref, w_ref, gain_ref, red_ref, exp_ref, cos_ref, sin_ref,
         qT_ref, k_ref, vT_ref, f_ref) = refs
    else:
        x_ref, nw_ref, mod_ref, w_ref, gain_ref, red_ref, exp_ref, k_ref, vT_ref = refs
    sh, sc = _mod_rows(mod_ref, row)[:2]
    red = red_ref[...].astype(BF16)
    expm = exp_ref[...].astype(BF16)
    tm = x_ref.shape[0]
    n_parts = 2 if latent else 1
    m = tm // n_parts

    def matmuls(r0):
        h = (_rms(x_ref[r0:r0 + m, :], nw_ref[0, 0:1, :]) * (1.0 + sc) + sh).astype(BF16)
        z = _dot(h, w_ref[...])
        zqk = z[:, :QK_WIDTH]
        hi, lo = _split_bf16(zqk * zqk)
        ms = _dot(hi, red) + _dot(lo, red)
        rhi, rlo = _split_bf16(lax.rsqrt(ms + NORM_EPS))
        return z, zqk * (_dot(rhi, expm) + _dot(rlo, expm)) * gain_ref[...]

    def finish(r0, z, yn):
        if latent:
            lane = lax.broadcasted_iota(jnp.int32, (1, LANES), 1)
            first_half = (lane // ROPE_HALF) % 2 == 0
            cos, sin = cos_ref[r0:r0 + m, :], sin_ref[r0:r0 + m, :]
            chunks = []
            for c in range(QK_WIDTH // LANES):
                yc = yn[:, c * LANES:(c + 1) * LANES]
                partner = jnp.where(first_half, pltpu.roll(yc, LANES - ROPE_HALF, axis=1),
                                    pltpu.roll(yc, ROPE_HALF, axis=1))
                chunks.append(yc * cos + partner * sin)
            yn = jnp.concatenate(chunks, axis=1)
            qT_ref[:, r0:r0 + m] = yn[:, :ATTN_WIDTH].T.astype(qk_dtype)
            f_ref[r0:r0 + m, :] = z[:, QK_WIDTH + KV_WIDTH:].astype(BF16)
        for g in range(N_KV_HEADS):
            k_ref[g, r0:r0 + m, :] = yn[:, ATTN_WIDTH + g * HEAD_DIM:ATTN_WIDTH + (g + 1) * HEAD_DIM].astype(qk_dtype)
        vT = z[:, QK_WIDTH:QK_WIDTH + KV_WIDTH].T.astype(BF16)
        ones = jnp.ones((ONES_ROWS, m), BF16)
        for g in range(N_KV_HEADS):
            vT_ref[g * V_ROWS:g * V_ROWS + HEAD_DIM, r0:r0 + m] = vT[g * HEAD_DIM:(g + 1) * HEAD_DIM]
            vT_ref[g * V_ROWS + HEAD_DIM:(g + 1) * V_ROWS, r0:r0 + m] = ones

    parts = [(p * m,) + matmuls(p * m) for p in range(n_parts)]
    for r0, z, yn in parts:
        finish(r0, z, yn)


def _evenproj(x2d, nw, mod, w_bf, gain, *, latent, tm, qk_dtype):
    n = x2d.shape[0]
    red, expm = _head_matrices()
    const = lambda i: (0, 0)
    in_specs = [pl.BlockSpec((tm, D_MODEL), lambda i: (i, 0)),
                pl.BlockSpec((1, 4, D_MODEL), lambda i: (0, 0, 0)),
                pl.BlockSpec((1, 8, 6 * D_MODEL), lambda i: (0, 0, 0)),
                pl.BlockSpec((D_MODEL, EVEN_IN_WIDTH), const),
                pl.BlockSpec((1, QK_WIDTH), const),
                pl.BlockSpec((QK_WIDTH, LANES), const),
                pl.BlockSpec((LANES, QK_WIDTH), const)]
    args = [x2d, nw, mod, w_bf, gain, jnp.asarray(red), jnp.asarray(expm)]
    k_shape = jax.ShapeDtypeStruct((N_KV_HEADS, n, HEAD_DIM), qk_dtype)
    vT_shape = jax.ShapeDtypeStruct((N_KV_HEADS * V_ROWS, n), BF16)
    k_spec = pl.BlockSpec((N_KV_HEADS, tm, HEAD_DIM), lambda i: (0, i, 0))
    vT_spec = pl.BlockSpec((N_KV_HEADS * V_ROWS, tm), lambda i: (0, i))
    if latent:
        cos, sin = _rope_tables(n)
        in_specs += [pl.BlockSpec((tm, LANES), lambda i: (i, 0))] * 2
        args += [jnp.asarray(cos), jnp.asarray(sin)]
        out_shape = (jax.ShapeDtypeStruct((ATTN_WIDTH, n), qk_dtype), k_shape, vT_shape,
                     jax.ShapeDtypeStruct((n, FOURIER_WIDTH), BF16))
        out_specs = (pl.BlockSpec((ATTN_WIDTH, tm), lambda i: (0, i)), k_spec, vT_spec,
                     pl.BlockSpec((tm, FOURIER_WIDTH), lambda i: (i, 0)))
    else:
        out_shape = (k_shape, vT_shape)
        out_specs = (k_spec, vT_spec)
    return pl.pallas_call(
        functools.partial(_evenproj_kernel, row=0 if latent else 1, latent=latent, qk_dtype=qk_dtype),
        out_shape=out_shape, grid=(n // tm,), in_specs=in_specs, out_specs=out_specs,
        compiler_params=_cparams(("parallel",)),
        name="evenproj_lat" if latent else "evenproj_ctx",
    )(*args)


def _visit_all(visit, k_ref, vT_ref, kc_ref, vcT_ref, tk):
    def body(c, carry):
        off = pl.multiple_of(c * tk, tk)
        visit(k_ref[0, pl.ds(off, tk), :], vT_ref[:, pl.ds(off, tk)])
        return carry

    lax.fori_loop(0, k_ref.shape[1] // tk, body, 0)
    visit(kc_ref[0], vcT_ref[...])


def _attn_bounded_kernel(qT_ref, k_ref, vT_ref, kc_ref, vcT_ref, o_ref, acc_sc, *, tk, qw):
    n_tiles = k_ref.shape[1] // tk
    acc_sc[...] = jnp.zeros(acc_sc.shape, F32)
    tiles = [(k_ref.at[0, c * tk:(c + 1) * tk, :], vT_ref.at[:, c * tk:(c + 1) * tk]) for c in range(n_tiles)]
    tiles.append((kc_ref.at[0], vcT_ref))
    steps = [(kt, vt, j, q0) for q0 in range(0, qT_ref.shape[1], qw) for kt, vt in tiles for j in range(Q_PER_KV)]

    def scores(step):
        kt, _, j, q0 = step
        return _dot(kt[...], qT_ref[j * HEAD_DIM:(j + 1) * HEAD_DIM, q0:q0 + qw])

    s = scores(steps[0])
    for n, (_, vt, j, q0) in enumerate(steps):
        s_next = scores(steps[n + 1]) if n + 1 < len(steps) else None
        acc_sc[j, :, q0:q0 + qw] += _dot(vt[...], jnp.exp2(s).astype(BF16))
        s = s_next
    for j in range(Q_PER_KV):
        acc = acc_sc[j]
        o_ref[j * HEAD_DIM:(j + 1) * HEAD_DIM, :] = (acc[:HEAD_DIM] / acc[HEAD_DIM:HEAD_DIM + 1]).astype(BF16)


def _attn_online_kernel(qT_ref, k_ref, vT_ref, kc_ref, vcT_ref, o_ref, m_sc, acc_sc, *, tk):
    m_sc[...] = jnp.full(m_sc.shape, -jnp.inf, F32)
    acc_sc[...] = jnp.zeros(acc_sc.shape, F32)

    def visit(kt, vt):
        for j in range(Q_PER_KV):
            s = _dot(kt, qT_ref[j * HEAD_DIM:(j + 1) * HEAD_DIM, :])
            m_old = m_sc[j]
            m_new = jnp.maximum(m_old, jnp.max(s, axis=0, keepdims=True))
            p = jnp.exp2(s - m_new).astype(BF16)
            acc_sc[j] = jnp.exp2(m_old - m_new) * acc_sc[j] + _dot(vt, p)
            m_sc[j] = m_new

    _visit_all(visit, k_ref, vT_ref, kc_ref, vcT_ref, tk)
    for j in range(Q_PER_KV):
        acc = acc_sc[j]
        o_ref[j * HEAD_DIM:(j + 1) * HEAD_DIM, :] = (acc[:HEAD_DIM] / acc[HEAD_DIM:HEAD_DIM + 1]).astype(BF16)


def _attention(qT, k, vT, kc, vcT, *, bounded, tq, tk, qw):
    n = qT.shape[1]
    n_ctx = kc.shape[1]
    gw = Q_PER_KV * HEAD_DIM
    common = dict(
        out_shape=jax.ShapeDtypeStruct((ATTN_WIDTH, n), BF16),
        grid=(N_KV_HEADS, n // tq),
        in_specs=[pl.BlockSpec((gw, tq), lambda g, i: (g, i)),
                  pl.BlockSpec((1, n, HEAD_DIM), lambda g, i: (g, 0, 0)),
                  pl.BlockSpec((V_ROWS, n), lambda g, i: (g, 0)),
                  pl.BlockSpec((1, n_ctx, HEAD_DIM), lambda g, i: (g, 0, 0)),
                  pl.BlockSpec((V_ROWS, n_ctx), lambda g, i: (g, 0))],
        out_specs=pl.BlockSpec((gw, tq), lambda g, i: (g, i)),
        compiler_params=_cparams(("parallel", "parallel")),
    )
    acc = pltpu.VMEM((Q_PER_KV, V_ROWS, tq), F32)
    if bounded:
        call = pl.pallas_call(functools.partial(_attn_bounded_kernel, tk=tk, qw=qw),
                              scratch_shapes=[acc], name="attn_bounded", **common)
    else:
        call = pl.pallas_call(functools.partial(_attn_online_kernel, tk=tk),
                              scratch_shapes=[pltpu.VMEM((Q_PER_KV, 1, tq), F32), acc],
                              name="attn_online", **common)
    return call(qT, k, vT, kc, vcT)


def _four_a_kernel(f_ref, cs_ref, base_ref, twr_ref, twi_ref, y_ref, *, nb):
    n1 = f_ref.shape[0]
    cs = cs_ref[...].astype(BF16)
    base = base_ref[...].astype(BF16)
    for u in range(nb):
        xb = f_ref[:, u * FOURIER_WIDTH:(u + 1) * FOURIER_WIDTH]
        ab = _dot(xb, cs)
        stacked = jnp.concatenate([ab[:, :FOURIER_WIDTH], ab[:, FOURIER_WIDTH:]], axis=0)
        p = _dot(base, stacked.astype(BF16))
        pr, pi = p[:n1], p[n1:]
        tr = twr_ref[0, :, u:u + 1]
        ti = twi_ref[0, :, u:u + 1]
        y_ref[u, 0] = tr * pr - ti * pi
        y_ref[u, 1] = tr * pi + ti * pr


def _four_b_kernel(y_ref, wb_ref, o_ref):
    n2, _, kb, w = y_ref.shape
    y = y_ref[...].reshape(n2 * 2 * kb, w).astype(BF16)
    o_ref[...] = _dot(wb_ref[...].astype(BF16), y).reshape(n2, kb, w)


def _fourier(f):
    n = f.shape[0]
    n2 = LANES
    n1 = n // n2
    nb = kb = 8
    cs, base, twr, twi, wb = (jnp.asarray(t) for t in _fourier_tables(n, kb))
    f2d = f.reshape(n1, n2 * FOURIER_WIDTH)
    y = pl.pallas_call(
        functools.partial(_four_a_kernel, nb=nb),
        out_shape=jax.ShapeDtypeStruct((n2, 2, n1, FOURIER_WIDTH), F32),
        grid=(n2 // nb,),
        in_specs=[pl.BlockSpec((n1, nb * FOURIER_WIDTH), lambda s: (0, s)),
                  pl.BlockSpec(cs.shape, lambda s: (0, 0)),
                  pl.BlockSpec(base.shape, lambda s: (0, 0)),
                  pl.BlockSpec((1, n1, nb), lambda s: (s, 0, 0)),
                  pl.BlockSpec((1, n1, nb), lambda s: (s, 0, 0))],
        out_specs=pl.BlockSpec((nb, 2, n1, FOURIER_WIDTH), lambda s: (s, 0, 0, 0)),
        compiler_params=_cparams(("parallel",)),
        name="four_a",
    )(f2d, cs, base, twr, twi)
    out = pl.pallas_call(
        _four_b_kernel,
        out_shape=jax.ShapeDtypeStruct((n2, n1, FOURIER_WIDTH), F32),
        grid=(n1 // kb,),
        in_specs=[pl.BlockSpec((n2, 2, kb, FOURIER_WIDTH), lambda s: (0, 0, s, 0)),
                  pl.BlockSpec(wb.shape, lambda s: (0, 0))],
        out_specs=pl.BlockSpec((n2, kb, FOURIER_WIDTH), lambda s: (0, s, 0)),
        compiler_params=_cparams(("parallel",)),
        name="four_b",
    )(y, wb)
    return out.reshape(n, FOURIER_WIDTH)


def _outproj_kernel(aT_ref, four_ref, x_ref, wa_ref, wf_ref, nw_ref, mod_ref, x1_ref, h_ref):
    _, _, gt1, sh2, sc2, _ = _mod_rows(mod_ref, 0)
    half = x_ref.shape[0] // 2
    starts = (0, half)
    ys = [lax.dot_general(aT_ref[:, r0:r0 + half], wa_ref[...], (((0,), (0,)), ((), ())),
                          preferred_element_type=F32)
          + _dot(four_ref[r0:r0 + half, :].astype(BF16), wf_ref[...]) for r0 in starts]
    for r0, y in zip(starts, ys):
        x1 = x_ref[r0:r0 + half, :] + gt1 * _rms(y, nw_ref[0, 1:2, :])
        x1_ref[r0:r0 + half, :] = x1
        h_ref[r0:r0 + half, :] = (_rms(x1, nw_ref[0, 2:3, :]) * (1.0 + sc2) + sh2).astype(BF16)


def _outproj(attnT, four, x2d, wa, wf, nw, mod, *, tm):
    n = x2d.shape[0]
    const = lambda i: (0, 0)
    return pl.pallas_call(
        _outproj_kernel,
        out_shape=(jax.ShapeDtypeStruct((n, D_MODEL), F32), jax.ShapeDtypeStruct((n, D_MODEL), BF16)),
        grid=(n // tm,),
        in_specs=[pl.BlockSpec((ATTN_WIDTH, tm), lambda i: (0, i)),
                  pl.BlockSpec((tm, FOURIER_WIDTH), lambda i: (i, 0)),
                  pl.BlockSpec((tm, D_MODEL), lambda i: (i, 0)),
                  pl.BlockSpec((ATTN_WIDTH, D_MODEL), const),
                  pl.BlockSpec((FOURIER_WIDTH, D_MODEL), const),
                  pl.BlockSpec((1, 4, D_MODEL), lambda i: (0, 0, 0)),
                  pl.BlockSpec((1, 8, 6 * D_MODEL), lambda i: (0, 0, 0))],
        out_specs=(pl.BlockSpec((tm, D_MODEL), lambda i: (i, 0)),
                   pl.BlockSpec((tm, D_MODEL), lambda i: (i, 0))),
        compiler_params=_cparams(("parallel",)),
        name="outproj",
    )(attnT, four, x2d, wa, wf, nw, mod)


def _swiglu(x, wg, wu, wd):
    g = _dot(x, wg)
    return _dot((g * jax.nn.sigmoid(g) * _dot(x, wu)).astype(BF16), wd)


def _ffn_kernel(h_ref, x_ref, wg_ref, wu_ref, wd_ref, nw_ref, mod_ref, nw1_ref, mod1_ref, x2_ref, h3_ref):
    gt2 = _mod_rows(mod_ref, 0)[5]
    sh, sc = _mod_rows(mod1_ref, 0)[:2]
    half = x_ref.shape[0] // 2
    starts = (0, half)
    ys = [_swiglu(h_ref[r0:r0 + half, :], wg_ref[...], wu_ref[...], wd_ref[...]) for r0 in starts]
    for r0, y in zip(starts, ys):
        x2 = x_ref[r0:r0 + half, :] + gt2 * _rms(y, nw_ref[0, 3:4, :])
        x2_ref[r0:r0 + half, :] = x2
        h3_ref[r0:r0 + half, :] = (_rms(x2, nw1_ref[0, 0:1, :]) * (1.0 + sc) + sh).astype(BF16)


def _ffn(h, x1, wg, wu, wd, nw, mod, nw1, mod1, *, tm):
    n = h.shape[0]
    nwspec = pl.BlockSpec((1, 4, D_MODEL), lambda i: (0, 0, 0))
    modspec = pl.BlockSpec((1, 8, 6 * D_MODEL), lambda i: (0, 0, 0))
    resident = dict(index_map=lambda i: (0, 0), pipeline_mode=pl.Buffered(1))
    return pl.pallas_call(
        _ffn_kernel,
        out_shape=(jax.ShapeDtypeStruct((n, D_MODEL), F32), jax.ShapeDtypeStruct((n, D_MODEL), BF16)),
        grid=(n // tm,),
        in_specs=[pl.BlockSpec((tm, D_MODEL), lambda i: (i, 0)),
                  pl.BlockSpec((tm, D_MODEL), lambda i: (i, 0)),
                  pl.BlockSpec((D_MODEL, D_FF), **resident),
                  pl.BlockSpec((D_MODEL, D_FF), **resident),
                  pl.BlockSpec((D_FF, D_MODEL), **resident),
                  nwspec, modspec, nwspec, modspec],
        out_specs=(pl.BlockSpec((tm, D_MODEL), lambda i: (i, 0)),
                   pl.BlockSpec((tm, D_MODEL), lambda i: (i, 0))),
        compiler_params=_cparams(("parallel",)),
        name="ffn",
    )(h, x1, wg, wu, wd, nw, mod, nw1, mod1)


EDGE_ROWS = 16


def _convedge_kernel(hf_ref, hl_ref, wc_ref, wu_ref, vf_ref, vl_ref):
    nt = hf_ref.shape[0]
    for h_ref, v_ref in ((hf_ref, vf_ref), (hl_ref, vl_ref)):
        h = h_ref[...].reshape(nt * EDGE_ROWS, D_MODEL)
        v = _dot(h, wc_ref[...]) * _dot(h, wu_ref[...])
        v_ref[...] = v.reshape(nt, EDGE_ROWS, D_MODEL)


def _convedge(h3, w_in_bf, *, tm):
    n = h3.shape[0]
    nt = n // tm
    h3t = h3.reshape(nt, tm, D_MODEL)
    last = tm // EDGE_ROWS - 1
    shape = jax.ShapeDtypeStruct((nt, EDGE_ROWS, D_MODEL), F32)
    return pl.pallas_call(
        _convedge_kernel,
        out_shape=(shape, shape),
        grid=(1,),
        in_specs=[pl.BlockSpec((nt, EDGE_ROWS, D_MODEL), lambda i: (0, 0, 0)),
                  pl.BlockSpec((nt, EDGE_ROWS, D_MODEL), lambda i: (0, last, 0)),
                  pl.BlockSpec((D_MODEL, D_MODEL), lambda i: (0, 1)),
                  pl.BlockSpec((D_MODEL, D_MODEL), lambda i: (0, 2))],
        out_specs=(pl.BlockSpec((nt, EDGE_ROWS, D_MODEL), lambda i: (0, 0, 0)),
                   pl.BlockSpec((nt, EDGE_ROWS, D_MODEL), lambda i: (0, 0, 0))),
        compiler_params=_cparams(("arbitrary",)),
        name="convedge",
    )(h3t, h3t, w_in_bf, w_in_bf)


def _convmix_kernel(h_ref, x_ref, vl_ref, vf_ref, win_ref, cw_ref, wout_ref, nw_ref, mod_ref, r_ref,
                    x3_ref, h4_ref, route_ref, cnt_ref):
    i = pl.program_id(0)
    tm = h_ref.shape[0]
    _, _, gt1, sh2, sc2, _ = _mod_rows(mod_ref, 0)
    z = _dot(h_ref[...], win_ref[...])
    b = z[:, :D_MODEL]
    v = z[:, D_MODEL:2 * D_MODEL] * z[:, 2 * D_MODEL:]
    has_prev = (i > 0).astype(F32)
    has_next = (i < pl.num_programs(0) - 1).astype(F32)
    prev_row = vl_ref[0, EDGE_ROWS - 1:EDGE_ROWS, :] * has_prev
    next_row = vf_ref[0, 0:1, :] * has_next
    rows = lax.broadcasted_iota(jnp.int32, (tm, 1), 0)
    v_dn = jnp.where(rows == 0, prev_row, pltpu.roll(v, 1, axis=0))
    v_up = jnp.where(rows == tm - 1, next_row, pltpu.roll(v, tm - 1, axis=0))
    conv = v_dn * cw_ref[0, 0:1, :] + v * cw_ref[0, 1:2, :] + v_up * cw_ref[0, 2:3, :]
    gated = (b * conv).astype(BF16)
    rhi, rlo = _split_bf16(r_ref[...])
    half = tm // 2
    ys = [_dot(gated[r0:r0 + half], wout_ref[...]) for r0 in (0, half)]
    logit_parts = []
    for r0, y in zip((0, half), ys):
        x3 = x_ref[r0:r0 + half, :] + gt1 * _rms(y, nw_ref[0, 1:2, :])
        x3_ref[r0:r0 + half, :] = x3
        h4 = _rms(x3, nw_ref[0, 2:3, :]) * (1.0 + sc2) + sh2
        h4_ref[r0:r0 + half, :] = _pack_pairs(h4)
        hhi, hlo = _split_bf16(h4)
        logit_parts.append(_dot(hhi, rhi) + (_dot(hlo, rhi) + _dot(hhi, rlo)))
    logits = jnp.concatenate(logit_parts, axis=0)
    lt = logits.T[:N_EXPERTS]
    e = jnp.exp(lt - jnp.max(lt, axis=0, keepdims=True))
    probs = e / jnp.sum(e, axis=0, keepdims=True)
    row = lax.broadcasted_iota(jnp.int32, lt.shape, 0).astype(F32)
    v1 = jnp.max(probs, axis=0, keepdims=True)
    i1 = jnp.min(jnp.where(probs == v1, row, float(N_EXPERTS)), axis=0, keepdims=True)
    rest = jnp.where(row == i1, -1.0, probs)
    v2 = jnp.max(rest, axis=0, keepdims=True)
    i2 = jnp.min(jnp.where(rest == v2, row, float(N_EXPERTS)), axis=0, keepdims=True)
    tot = v1 + v2
    sel = jnp.where(jnp.logical_or(row == i1, row == i2), 1.0, 0.0)
    earlier = lax.broadcasted_iota(jnp.int32, (tm, tm), 0) < lax.broadcasted_iota(jnp.int32, (tm, tm), 1)
    sel16 = jnp.concatenate([sel, jnp.zeros_like(sel)], axis=0).astype(BF16)
    rank = _dot(sel16, jnp.where(earlier, 1.0, 0.0).astype(BF16))[:N_EXPERTS]
    r1 = jnp.sum(jnp.where(row == i1, rank, 0.0), axis=0, keepdims=True)
    r2 = jnp.sum(jnp.where(row == i2, rank, 0.0), axis=0, keepdims=True)
    packed = jnp.concatenate([i1, i2, v1 / tot, v2 / tot, r1, r2, jnp.zeros((LANES - 6, tm), F32)], axis=0)
    route_ref[...] = packed.T
    cnt_ref[0] = jnp.broadcast_to(jnp.sum(sel, axis=1, keepdims=True), cnt_ref.shape[1:])


def _convmix(h3, x2, v_first, v_last, w_in_bf, conv_w, w_out_bf, nw, mod, router_pad, *, tm):
    n = h3.shape[0]
    nt = n // tm
    const = lambda i: (0, 0)
    return pl.pallas_call(
        _convmix_kernel,
        out_shape=(jax.ShapeDtypeStruct((n, D_MODEL), F32), jax.ShapeDtypeStruct((n, D_MODEL // 2), jnp.uint32),
                   jax.ShapeDtypeStruct((n, LANES), F32), jax.ShapeDtypeStruct((nt, 8, LANES), F32)),
        grid=(nt,),
        in_specs=[pl.BlockSpec((tm, D_MODEL), lambda i: (i, 0)),
                  pl.BlockSpec((tm, D_MODEL), lambda i: (i, 0)),
                  pl.BlockSpec((1, EDGE_ROWS, D_MODEL), lambda i: (jnp.maximum(i - 1, 0), 0, 0)),
                  pl.BlockSpec((1, EDGE_ROWS, D_MODEL), lambda i: (jnp.minimum(i + 1, nt - 1), 0, 0)),
                  pl.BlockSpec((D_MODEL, 3 * D_MODEL), const),
                  pl.BlockSpec((1, 3, D_MODEL), lambda i: (0, 0, 0)),
                  pl.BlockSpec((D_MODEL, D_MODEL), const),
                  pl.BlockSpec((1, 4, D_MODEL), lambda i: (0, 0, 0)),
                  pl.BlockSpec((1, 8, 6 * D_MODEL), lambda i: (0, 0, 0)),
                  pl.BlockSpec((D_MODEL, LANES), const)],
        out_specs=(pl.BlockSpec((tm, D_MODEL), lambda i: (i, 0)),
                   pl.BlockSpec((tm, D_MODEL // 2), lambda i: (i, 0)),
                   pl.BlockSpec((tm, LANES), lambda i: (i, 0)),
                   pl.BlockSpec((1, 8, LANES), lambda i: (i, 0, 0))),
        compiler_params=_cparams(("parallel",)),
        name="convmix",
    )(h3, x2, v_last, v_first, w_in_bf, conv_w, w_out_bf, nw, mod, router_pad)


ROW_TILE = TOKEN_TILE


def _count_le(sorted_vals, x):
    return jnp.sum((sorted_vals[None, :] <= x[:, None]).astype(jnp.int32), axis=1)


def _route(route, cnt_tiles, n_row_tiles):
    n = route.shape[0]
    i32 = jnp.int32
    cnt_tile = cnt_tiles[:, :N_EXPERTS, 0].astype(i32)
    cum_end = jnp.cumsum(cnt_tile, axis=0)
    cnt = cum_end[-1]
    cnt_pad = (cnt + ROW_TILE - 1) // ROW_TILE * ROW_TILE
    grp_end = jnp.cumsum(cnt_pad)
    start = grp_end - cnt_pad
    offs = jnp.repeat(start[None, :] + cum_end - cnt_tile, ROW_TILE, axis=0)
    experts = jnp.clip(route[:, :2].astype(i32), 0, N_EXPERTS - 1)
    picked = experts[:, :, None] == jnp.arange(N_EXPERTS, dtype=i32)[None, None, :]
    pos2 = jnp.sum(jnp.where(picked, offs[:, None, :], 0), axis=-1) + route[:, 4:6].astype(i32)
    pos2 = jnp.clip(pos2, 0, n_row_tiles * ROW_TILE - 1)
    r = jnp.arange(n_row_tiles, dtype=i32)
    base = r * ROW_TILE
    tile_valid = base < grp_end[-1]
    n_valid = jnp.clip(grp_end[-1] // ROW_TILE, 1, n_row_tiles)
    tile_exp = jnp.minimum(_count_le(grp_end, base), N_EXPERTS - 1)
    tile_exp = jnp.where(tile_valid, tile_exp, tile_exp[n_valid - 1])
    tile_src = jnp.where(tile_valid, r, n_valid - 1)
    tile_first = jnp.logical_and(tile_valid, base == start[tile_exp])
    tiles = (tile_exp, tile_src, tile_valid.astype(i32), tile_first.astype(i32))
    return pos2.T.reshape(-1), tiles


SC_CHUNK = 128


def _sc_workers():
    info = pltpu.get_tpu_info().sparse_core
    return info.num_cores, info.num_cores * info.num_subcores


def _sc_scatter_rows(rows, idx, n_out):
    n_src, d = rows.shape
    n_cores, n_workers = _sc_workers()
    per_worker = idx.shape[0] // n_workers
    assert idx.shape[0] % (n_workers * SC_CHUNK) == 0 and n_src % per_worker == 0
    idx2d = idx.reshape(-1, SC_CHUNK)
    mesh = plsc.VectorSubcoreMesh(core_axis_name="c", subcore_axis_name="s")

    @functools.partial(pl.kernel, mesh=mesh, out_type=jax.ShapeDtypeStruct((n_out, d), rows.dtype),
                       scratch_types=[pltpu.VMEM((1, SC_CHUNK), jnp.int32), pltpu.VMEM((SC_CHUNK, d), rows.dtype)])
    def scatter(rows_hbm, idx_hbm, out_hbm, idx_v, rows_v):
        wid = lax.axis_index("s") * n_cores + lax.axis_index("c")

        @pl.loop(0, per_worker // SC_CHUNK)
        def _(j):
            a = wid * per_worker + j * SC_CHUNK
            pltpu.sync_copy(idx_hbm.at[pl.ds(a // SC_CHUNK, 1)], idx_v)
            pltpu.sync_copy(rows_hbm.at[pl.ds(lax.rem(a, n_src), SC_CHUNK)], rows_v)
            pltpu.sync_copy(rows_v, out_hbm.at[idx_v.at[0]])

    return scatter(rows, idx2d)


def _sc_gather_rows(table, idx):
    d = table.shape[1]
    n_cores, n_workers = _sc_workers()
    per_worker = idx.shape[0] // n_workers
    assert idx.shape[0] % (n_workers * SC_CHUNK) == 0
    idx2d = idx.reshape(-1, SC_CHUNK)
    mesh = plsc.VectorSubcoreMesh(core_axis_name="c", subcore_axis_name="s")

    @functools.partial(pl.kernel, mesh=mesh, out_type=jax.ShapeDtypeStruct((idx.shape[0], d), table.dtype),
                       scratch_types=[pltpu.VMEM((1, SC_CHUNK), jnp.int32), pltpu.VMEM((SC_CHUNK, d), table.dtype)])
    def gather(table_hbm, idx_hbm, out_hbm, idx_v, rows_v):
        wid = lax.axis_index("s") * n_cores + lax.axis_index("c")

        @pl.loop(0, per_worker // SC_CHUNK)
        def _(j):
            a = wid * per_worker + j * SC_CHUNK
            pltpu.sync_copy(idx_hbm.at[pl.ds(a // SC_CHUNK, 1)], idx_v)
            pltpu.sync_copy(table_hbm.at[idx_v.at[0]], rows_v)
            pltpu.sync_copy(rows_v, out_hbm.at[pl.ds(a, SC_CHUNK)])

    return gather(table, idx2d)


ITEM_IDLE, ITEM_CHUNK, ITEM_TILE = 0, 1, 2


def _expert_items(tiles, n_ch):
    tile_exp, _, tile_valid, tile_first = tiles
    n_row_tiles = tile_exp.shape[0]
    i32 = jnp.int32
    per_tile = jnp.where(tile_first != 0, n_ch, tile_valid)
    off_end = jnp.cumsum(per_tile)
    total = off_end[-1]
    n_items = N_EXPERTS * n_ch + n_row_tiles
    k = jnp.arange(n_items, dtype=i32)
    kk = jnp.minimum(k, total - 1)
    r = jnp.clip(_count_le(off_end, kk), 0, n_row_tiles - 1)
    first = tile_first[r] != 0
    kind = jnp.where(k < total, jnp.where(first, ITEM_CHUNK, ITEM_TILE), ITEM_IDLE).astype(i32)
    chunk = jnp.where(first, kk - (off_end[r] - per_tile[r]), n_ch - 1)
    return r.astype(i32), jnp.clip(chunk, 0, n_ch - 1).astype(i32), tile_exp[r].astype(i32), kind


def _experts_kernel(it_tile, it_chunk, it_exp, it_kind, xs_ref, wg_ref, wu_ref, wd_ref, ys_ref,
                    wg_c, wu_c, wd_c, x_sc, acc_ref):
    k = pl.program_id(0)
    n_ch = wg_c.shape[0]

    @pl.when(it_kind[k] == ITEM_CHUNK)
    def _():
        c = it_chunk[k]
        wg_c[c] = wg_ref[0, 0].astype(BF16)
        wu_c[c] = wu_ref[0, 0].astype(BF16)
        wd_c[c] = wd_ref[0, 0].astype(BF16)

        @pl.when(c == 0)
        def _():
            x_sc[...] = _unpack_pairs(xs_ref[...])
            acc_ref[...] = jnp.zeros(acc_ref.shape, F32)

        acc_ref[...] += _swiglu(x_sc[...], wg_c[c], wu_c[c], wd_c[c])

        @pl.when(c == n_ch - 1)
        def _():
            ys_ref[...] = _pack_pairs(acc_ref[...])

    @pl.when(it_kind[k] == ITEM_TILE)
    def _():
        x = _unpack_pairs(xs_ref[...])
        y = _swiglu(x, wg_c[0], wu_c[0], wd_c[0])
        for c in range(1, n_ch):
            y = y + _swiglu(x, wg_c[c], wu_c[c], wd_c[c])
        ys_ref[...] = _pack_pairs(y)


def _experts(tiles, xs, wg, wu, wd, *, tf):
    n_rows, half = xs.shape
    n_ch = D_EXPERT // tf
    items = _expert_items(tiles, n_ch)
    grid_spec = pltpu.PrefetchScalarGridSpec(
        num_scalar_prefetch=4, grid=(items[0].shape[0],),
        in_specs=[pl.BlockSpec((ROW_TILE, half), lambda k, t, c, e, kind: (t[k], 0)),
                  pl.BlockSpec((1, 1, D_MODEL, tf), lambda k, t, c, e, kind: (0, e[k], 0, c[k])),
                  pl.BlockSpec((1, 1, D_MODEL, tf), lambda k, t, c, e, kind: (0, e[k], 0, c[k])),
                  pl.BlockSpec((1, 1, tf, D_MODEL), lambda k, t, c, e, kind: (0, e[k], c[k], 0))],
        out_specs=pl.BlockSpec((ROW_TILE, half), lambda k, t, c, e, kind: (t[k], 0)),
        scratch_shapes=[pltpu.VMEM((n_ch, D_MODEL, tf), BF16), pltpu.VMEM((n_ch, D_MODEL, tf), BF16),
                        pltpu.VMEM((n_ch, tf, D_MODEL), BF16), pltpu.VMEM((ROW_TILE, D_MODEL), BF16),
                        pltpu.VMEM((ROW_TILE, D_MODEL), F32)])
    return pl.pallas_call(
        _experts_kernel, grid_spec=grid_spec,
        out_shape=jax.ShapeDtypeStruct((n_rows, half), jnp.uint32),
        compiler_params=_cparams(("arbitrary",)),
        name="moe_experts",
    )(*items, xs, wg, wu, wd)


def _combine_kernel(a_ref, b_ref, route_ref, x_ref, nw_ref, mod_ref, o_ref):
    route = route_ref[...]
    y = route[:, 2:3] * _unpack_pairs(a_ref[...]).astype(F32) + route[:, 3:4] * _unpack_pairs(b_ref[...]).astype(F32)
    gt2 = _mod_rows(mod_ref, 0)[5]
    o_ref[...] = x_ref[...] + gt2 * _rms(y, nw_ref[0, 3:4, :])


def _combine(rows, route, x3, nw, mod):
    n = x3.shape[0]
    nt = n // ROW_TILE
    half = rows.shape[1]
    return pl.pallas_call(
        _combine_kernel,
        out_shape=jax.ShapeDtypeStruct((n, D_MODEL), F32),
        grid=(nt,),
        in_specs=[pl.BlockSpec((ROW_TILE, half), lambda i: (i, 0)),
                  pl.BlockSpec((ROW_TILE, half), lambda i: (i + nt, 0)),
                  pl.BlockSpec((ROW_TILE, LANES), lambda i: (i, 0)),
                  pl.BlockSpec((ROW_TILE, D_MODEL), lambda i: (i, 0)),
                  pl.BlockSpec((1, 4, D_MODEL), lambda i: (0, 0, 0)),
                  pl.BlockSpec((1, 8, 6 * D_MODEL), lambda i: (0, 0, 0))],
        out_specs=pl.BlockSpec((ROW_TILE, D_MODEL), lambda i: (i, 0)),
        compiler_params=_cparams(("parallel",)),
        name="moe_combine",
    )(rows, rows, route, x3, nw, mod)


def _moe(h4p, route, cnt_tiles, x3, wg, wu, wd, nw, mod):
    n = h4p.shape[0]
    n_rows = 2 * n + N_EXPERTS * ROW_TILE
    pos, tiles = _route(route, cnt_tiles, n_rows // ROW_TILE)
    xs = _sc_scatter_rows(h4p, pos, n_rows)
    ys = _experts(tiles, xs, wg, wu, wd, tf=EXPERT_CHUNK)
    rows = _sc_gather_rows(ys, pos)
    return _combine(rows, route, x3, nw, mod)


def kernel(x, c, ctx, c_ctx, ada_w, ada_b, norm_w, e_w_in, e_q_gain, e_k_gain, e_w_out, e_ffn_gate,
           e_ffn_up, e_ffn_down, o_w_in, o_conv_w, o_w_out, o_router, o_exp_gate, o_exp_up, o_exp_down):
    assert x.shape[0] == 1 and x.shape[2] == D_MODEL and ada_w.shape[0] == 2
    n = x.shape[1]
    x2d = x[0]
    ctx2d = ctx[0]
    mod = _ada(c, c_ctx, ada_w, ada_b)
    mod0, mod1 = mod[0:1], mod[1:2]
    nw0, nw1 = norm_w[0:1], norm_w[1:2]

    w_in = e_w_in[0].astype(BF16)
    scale = HEAD_DIM ** -0.5 * np.log2(np.e)
    gain = jnp.concatenate([jnp.tile(e_q_gain[0], N_Q_HEADS) * scale,
                            jnp.tile(e_k_gain[0], N_KV_HEADS)])[None, :]
    score_bound = 1.02 * HEAD_DIM * scale * jnp.max(jnp.abs(e_q_gain[0])) * jnp.max(jnp.abs(e_k_gain[0]))

    def mixer_inputs(fast):
        qk_dtype = FP8 if fast else BF16
        qT, k, vT, f = _evenproj(x2d, nw0, mod0, w_in, gain, latent=True, tm=TOKEN_TILE, qk_dtype=qk_dtype)
        kc, vcT = _evenproj(ctx2d, nw0, mod0, w_in, gain, latent=False, tm=ctx2d.shape[0], qk_dtype=qk_dtype)
        return _attention(qT, k, vT, kc, vcT, bounded=fast, tq=2 * ATTN_Q_TILE, tk=ATTN_KEY_TILE, qw=ATTN_Q_TILE), f

    attnT, f = lax.cond(score_bound <= FAST_SCORE_BOUND, functools.partial(mixer_inputs, True),
                        functools.partial(mixer_inputs, False))
    four = _fourier(f)
    w_out = e_w_out[0].astype(BF16)
    x1, h2 = _outproj(attnT, four, x2d, w_out[:ATTN_WIDTH], w_out[ATTN_WIDTH:], nw0, mod0, tm=TOKEN_TILE)
    x2, h3 = _ffn(h2, x1, e_ffn_gate[0].astype(BF16), e_ffn_up[0].astype(BF16),
                  e_ffn_down[0].astype(BF16), nw0, mod0, nw1, mod1, tm=TOKEN_TILE)

    ow_in = o_w_in[0].astype(BF16)
    v_first, v_last = _convedge(h3, ow_in, tm=ROW_TILE)
    router_pad = jnp.pad(o_router[0], ((0, 0), (0, LANES - N_EXPERTS)))
    x3, h4p, route, cnt_tiles = _convmix(h3, x2, v_first, v_last, ow_in, o_conv_w, o_w_out[0].astype(BF16),
                                         nw1, mod1, router_pad, tm=ROW_TILE)
    out = _moe(h4p, route, cnt_tiles, x3, o_exp_gate, o_exp_up, o_exp_down, nw1, mod1)
    return out[None]
```
